```python
import math
import jax
import jax.numpy as jnp
from jax import lax
import numpy as np


D_MODEL = 2048
BATCH = 1
SEQ = 16384
DEPTH = 4

CHUNK = 64
Q_BLOCK = 128
ROPE_THETA = 10000.0
EPS = 1e-6
NEG = -1e30

A_HEAD_DIM = 128
A_WIDTH = D_MODEL // 4
A_HEADS = A_WIDTH // A_HEAD_DIM
IDX_HEADS = 8
IDX_DIM = 64
TOPK_MAX = 256

S5_GROUP = 16
S5_WIDTH = D_MODEL // 4
S5_GROUPS = S5_WIDTH // S5_GROUP
S5_STATE = 64
S5_DT_MIN = 0.001
S5_DT_MAX = 0.1

C_QK_DIM = 64
C_V_DIM = 2 * C_QK_DIM
C_WIDTH = D_MODEL // 2
C_HEADS = C_WIDTH // C_V_DIM

MIX_WIDTH = A_WIDTH + S5_WIDTH + C_WIDTH
IN_WIDTHS = (A_WIDTH, A_WIDTH, A_WIDTH, IDX_HEADS * IDX_DIM, IDX_DIM, IDX_HEADS, S5_WIDTH, 2 * C_HEADS * C_QK_DIM, 2 * C_HEADS * C_QK_DIM, C_HEADS * C_V_DIM)
IN_WIDTH = 3 * A_WIDTH + IDX_HEADS * IDX_DIM + IDX_DIM + IDX_HEADS + S5_WIDTH + 4 * C_HEADS * C_QK_DIM + C_HEADS * C_V_DIM

MEM_TOKENS = 256
X_HEADS = 4
X_HEAD_DIM = D_MODEL // X_HEADS
D_FF = 4 * D_MODEL

kernel_name = 'hybrid_dsa_s5_diffattn_encoder'


def _rmsnorm(x, g):
    xf = x.astype(jnp.float32)
    xf = xf * lax.rsqrt(jnp.mean(xf * xf, axis=-1, keepdims=True) + EPS)
    return (xf * g.astype(jnp.float32)).astype(x.dtype)


def _rope_tables(L, dim):
    inv = ROPE_THETA ** (-jnp.arange(0, dim, 2, dtype=jnp.float32) / dim)
    ang = jnp.arange(L, dtype=jnp.float32)[:, None] * inv[None, :]
    return jnp.cos(ang), jnp.sin(ang)


def _rope(t, cos, sin):
    shape = (1, cos.shape[0]) + (1,) * (t.ndim - 3) + (cos.shape[1],)
    c = cos.reshape(shape).astype(t.dtype)
    s = sin.reshape(shape).astype(t.dtype)
    t1, t2 = jnp.split(t, 2, axis=-1)
    return jnp.concatenate([t1 * c - t2 * s, t1 * s + t2 * c], axis=-1)


def _split_columns(proj):
    outs, start = [], 0
    for w in IN_WIDTHS:
        outs.append(proj[..., start:start + w])
        start += w
    return outs


def _to_blocks(t):
    B, L = t.shape[:2]
    t = t.reshape((B, L // Q_BLOCK, Q_BLOCK) + t.shape[2:])
    return jnp.moveaxis(t, 1, 0)


def _from_blocks(t):
    t = jnp.moveaxis(t, 0, 1)
    return t.reshape((t.shape[0], t.shape[1] * t.shape[2]) + t.shape[3:])


def _dsa_mixer(q, k, v, q_idx, k_idx, w_idx):
    B, L = q.shape[:2]
    top_k = min(TOPK_MAX, L // 4)
    key_chunk = jnp.arange(L) // CHUNK
    bidx = jnp.arange(B)[:, None, None]
    k_idx_f = k_idx.astype(jnp.float32)

    def block(args):
        qb, qib, wb, qpos = args
        q_chunk = qpos // CHUNK
        allowed = key_chunk[None, :] <= q_chunk[:, None]
        dots = jnp.einsum('bqhd,bkd->bqhk', qib.astype(jnp.float32), k_idx_f) * (IDX_DIM ** -0.5)
        score = jnp.einsum('bqh,bqhk->bqk', wb.astype(jnp.float32) * (IDX_HEADS ** -0.5), jax.nn.relu(dots))
        score = jnp.where(allowed[None], score, NEG)
        _, sel = lax.top_k(score, top_k)
        valid = (sel // CHUNK) <= q_chunk[None, :, None]
        k_sel = k[bidx, sel]
        v_sel = v[bidx, sel]
        s = jnp.einsum('bqhd,bqkhd->bhqk', qb, k_sel).astype(jnp.float32) * (A_HEAD_DIM ** -0.5)
        s = jnp.where(valid[:, None], s, NEG)
        p = jax.nn.softmax(s, axis=-1).astype(v.dtype)
        return jnp.einsum('bhqk,bqkhd->bqhd', p, v_sel)

    qpos = jnp.arange(L).reshape(L // Q_BLOCK, Q_BLOCK)
    out = lax.map(block, (_to_blocks(q), _to_blocks(q_idx), _to_blocks(w_idx), qpos))
    return _from_blocks(out)


def _s5_mixer(u, a_re, a_im, log_dt, b_re, b_im, c_re, c_im, d_skip, w_glu, b_glu):
    f32 = jnp.float32
    B, L = u.shape[:2]
    uf = u.astype(f32).reshape(B, L, S5_GROUPS, S5_GROUP)
    a_re = a_re.astype(f32)
    a_im = a_im.astype(f32)
    dt = jnp.exp(log_dt.astype(f32))[:, None]
    mag = jnp.exp(a_re * dt)
    lb_re = mag * jnp.cos(a_im * dt)
    lb_im = mag * jnp.sin(a_im * dt)
    den = a_re * a_re + a_im * a_im
    f_re = ((lb_re - 1.0) * a_re + lb_im * a_im) / den
    f_im = (lb_im * a_re - (lb_re - 1.0) * a_im) / den
    b_re = b_re.astype(f32)
    b_im = b_im.astype(f32)
    bb_re = f_re[..., None] * b_re - f_im[..., None] * b_im
    bb_im = f_re[..., None] * b_im + f_im[..., None] * b_re
    bu_re = jnp.einsum('gnp,blgp->blgn', bb_re, uf)
    bu_im = jnp.einsum('gnp,blgp->blgn', bb_im, uf)
    al_re = jnp.broadcast_to(lb_re, bu_re.shape)
    al_im = jnp.broadcast_to(lb_im, bu_im.shape)

    def combine(e1, e2):
        a1r, a1i, b1r, b1i = e1
        a2r, a2i, b2r, b2i = e2
        return (a2r * a1r - a2i * a1i,
                a2r * a1i + a2i * a1r,
                a2r * b1r - a2i * b1i + b2r,
                a2r * b1i + a2i * b1r + b2i)

    _, _, x_re, x_im = lax.associative_scan(combine, (al_re, al_im, bu_re, bu_im), axis=1)
    y = (jnp.einsum('gpn,blgn->blgp', c_re.astype(f32), x_re)
         - jnp.einsum('gpn,blgn->blgp', c_im.astype(f32), x_im)
         + d_skip.astype(f32) * uf)
    y = jax.nn.gelu(y.reshape(B, L, S5_WIDTH))
    y = y * jax.nn.sigmoid(y @ w_glu.astype(f32) + b_glu.astype(f32))
    return y.astype(u.dtype)


def _diff_mixer(q, k, v, lam, sub_gain, lam_init):
    B, L = q.shape[:2]
    q = q.reshape(B, L, C_HEADS, 2, C_QK_DIM)
    k = k.reshape(B, L, C_HEADS, 2, C_QK_DIM)
    key_chunk = jnp.arange(L) // CHUNK

    def block(args):
        qb, qpos = args
        allowed = key_chunk[None, :] <= (qpos // CHUNK)[:, None]
        s = jnp.einsum('bqhcd,bkhcd->bchqk', qb, k).astype(jnp.float32) * (C_QK_DIM ** -0.5)
        s = jnp.where(allowed, s, NEG)
        p = jax.nn.softmax(s, axis=-1)
        diff = (p[:, 0] - lam * p[:, 1]).astype(v.dtype)
        return jnp.einsum('bhqk,bkhd->bqhd', diff, v)

    qpos = jnp.arange(L).reshape(L // Q_BLOCK, Q_BLOCK)
    o = _from_blocks(lax.map(block, (_to_blocks(q), qpos)))
    o = _rmsnorm(o, sub_gain) * (1.0 - lam_init)
    return o.reshape(B, L, C_WIDTH)


def _cross_attn(h, mem_n, wq, wk, wv, wo):
    B, L = h.shape[:2]
    M = mem_n.shape[1]
    q = (h @ wq).reshape(B, L, X_HEADS, X_HEAD_DIM)
    k = (mem_n @ wk).reshape(B, M, X_HEADS, X_HEAD_DIM)
    v = (mem_n @ wv).reshape(B, M, X_HEADS, X_HEAD_DIM)
    s = jnp.einsum('bqhd,bkhd->bhqk', q, k).astype(jnp.float32) * (X_HEAD_DIM ** -0.5)
    p = jax.nn.softmax(s, axis=-1).astype(h.dtype)
    o = jnp.einsum('bhqk,bkhd->bqhd', p, v).reshape(B, L, D_MODEL)
    return o @ wo


def setup_inputs(seed: int = 0) -> dict:
    key = jax.random.key(seed)
    ks = jax.random.split(key, 32)
    f32 = jnp.float32
    G, N, P = S5_GROUPS, S5_STATE, S5_GROUP

    def nrm(k, shape, scale):
        return scale * jax.random.normal(k, shape, f32)

    def gain(k, shape):
        return 1.0 + 0.01 * jax.random.normal(k, shape, f32)

    return {
        'x': nrm(ks[0], (BATCH, SEQ, D_MODEL), 1.0),
        'mem': nrm(ks[1], (BATCH, MEM_TOKENS, D_MODEL), 1.0),
        'norm_mix': gain(ks[2], (DEPTH, D_MODEL)),
        'w_in': nrm(ks[3], (DEPTH, D_MODEL, IN_WIDTH), D_MODEL ** -0.5),
        's5_a_re': -0.5 + nrm(ks[4], (DEPTH, G, N), 0.01),
        's5_a_im': math.pi * jnp.arange(N, dtype=f32) + nrm(ks[5], (DEPTH, G, N), 0.01),
        's5_log_dt': jax.random.uniform(ks[6], (DEPTH, G), f32, math.log(S5_DT_MIN), math.log(S5_DT_MAX)),
        's5_b_re': nrm(ks[7], (DEPTH, G, N, P), (2 * P) ** -0.5),
        's5_b_im': nrm(ks[8], (DEPTH, G, N, P), (2 * P) ** -0.5),
        's5_c_re': nrm(ks[9], (DEPTH, G, P, N), N ** -0.5),
        's5_c_im': nrm(ks[10], (DEPTH, G, P, N), N ** -0.5),
        's5_d': nrm(ks[11], (DEPTH, G, P), 1.0),
        's5_w_glu': nrm(ks[12], (DEPTH, S5_WIDTH, S5_WIDTH), S5_WIDTH ** -0.5),
        's5_b_glu': nrm(ks[13], (DEPTH, S5_WIDTH), 0.01),
        'diff_lam_q1': nrm(ks[14], (DEPTH, C_QK_DIM), 0.1),
        'diff_lam_k1': nrm(ks[15], (DEPTH, C_QK_DIM), 0.1),
        'diff_lam_q2': nrm(ks[16], (DEPTH, C_QK_DIM), 0.1),
        'diff_lam_k2': nrm(ks[17], (DEPTH, C_QK_DIM), 0.1),
        'diff_subln': gain(ks[18], (DEPTH, C_V_DIM)),
        'w_out': nrm(ks[19], (DEPTH, MIX_WIDTH, D_MODEL), MIX_WIDTH ** -0.5),
        'norm_xattn': gain(ks[20], (DEPTH, D_MODEL)),
        'norm_mem': gain(ks[21], (DEPTH, D_MODEL)),
        'xattn_q': nrm(ks[22], (DEPTH, D_MODEL, D_MODEL), D_MODEL ** -0.5),
        'xattn_k': nrm(ks[23], (DEPTH, D_MODEL, D_MODEL), D_MODEL ** -0.5),
        'xattn_v': nrm(ks[24], (DEPTH, D_MODEL, D_MODEL), D_MODEL ** -0.5),
        'xattn_o': nrm(ks[25], (DEPTH, D_MODEL, D_MODEL), D_MODEL ** -0.5),
        'norm_mlp': gain(ks[26], (DEPTH, D_MODEL)),
        'w_ff1': nrm(ks[27], (DEPTH, D_MODEL, D_FF), D_MODEL ** -0.5),
        'w_ff2': nrm(ks[28], (DEPTH, D_FF, D_MODEL), D_FF ** -0.5),
        'norm_final': gain(ks[29], (D_MODEL,)),
    }


def reference(x, mem, norm_mix, w_in, s5_a_re, s5_a_im, s5_log_dt, s5_b_re, s5_b_im, s5_c_re, s5_c_im, s5_d, s5_w_glu, s5_b_glu, diff_lam_q1, diff_lam_k1, diff_lam_q2, diff_lam_k2, diff_subln, w_out, norm_xattn, norm_mem, xattn_q, xattn_k, xattn_v, xattn_o, norm_mlp, w_ff1, w_ff2, norm_final):
    f32 = jnp.float32
    B, L, _ = x.shape
    cos_a, sin_a = _rope_tables(L, A_HEAD_DIM)
    cos_i, sin_i = _rope_tables(L, IDX_DIM)
    cos_c, sin_c = _rope_tables(L, C_QK_DIM)
    for l in range(DEPTH):
        h = _rmsnorm(x, norm_mix[l])
        q_a, k_a, v_a, q_i, k_i, w_i, u_s, q_c, k_c, v_c = _split_columns(h @ w_in[l])

        q_a = _rope(q_a.reshape(B, L, A_HEADS, A_HEAD_DIM), cos_a, sin_a)
        k_a = _rope(k_a.reshape(B, L, A_HEADS, A_HEAD_DIM), cos_a, sin_a)
        v_a = v_a.reshape(B, L, A_HEADS, A_HEAD_DIM)
        q_i = _rope(q_i.reshape(B, L, IDX_HEADS, IDX_DIM), cos_i, sin_i)
        k_i = _rope(k_i, cos_i, sin_i)
        o_a = _dsa_mixer(q_a, k_a, v_a, q_i, k_i, w_i).reshape(B, L, A_WIDTH)

        o_s = _s5_mixer(u_s, s5_a_re[l], s5_a_im[l], s5_log_dt[l], s5_b_re[l], s5_b_im[l], s5_c_re[l], s5_c_im[l], s5_d[l], s5_w_glu[l], s5_b_glu[l])

        q_c = _rope(q_c.reshape(B, L, 2 * C_HEADS, C_QK_DIM), cos_c, sin_c)
        k_c = _rope(k_c.reshape(B, L, 2 * C_HEADS, C_QK_DIM), cos_c, sin_c)
        v_c = v_c.reshape(B, L, C_HEADS, C_V_DIM)
        lam_init = 0.8 - 0.6 * math.exp(-0.3 * l)
        lam = (jnp.exp(jnp.sum(diff_lam_q1[l].astype(f32) * diff_lam_k1[l].astype(f32)))
               - jnp.exp(jnp.sum(diff_lam_q2[l].astype(f32) * diff_lam_k2[l].astype(f32)))
               + lam_init)
        o_c = _diff_mixer(q_c, k_c, v_c, lam, diff_subln[l], lam_init)

        x = x + jnp.concatenate([o_a, o_s, o_c], axis=-1) @ w_out[l]

        x = x + _cross_attn(_rmsnorm(x, norm_xattn[l]), _rmsnorm(mem, norm_mem[l]), xattn_q[l], xattn_k[l], xattn_v[l], xattn_o[l])

        h = _rmsnorm(x, norm_mlp[l])
        x = x + jnp.square(jax.nn.relu(h @ w_ff1[l])) @ w_ff2[l]
    return _rmsnorm(x, norm_final)
```

```python
import functools
import math

import jax
import jax.numpy as jnp
from jax import lax
from jax.experimental import pallas as pl
from jax.experimental.pallas import tpu as pltpu

F32 = jnp.float32
BF16 = jnp.bfloat16

D_MODEL = 2048
CHUNK = 64
CHUNK_SHIFT = CHUNK.bit_length() - 1
assert 1 << CHUNK_SHIFT == CHUNK
ROPE_THETA = 10000.0
EPS = 1e-6
NEG = -1e30

A_HEAD_DIM = 128
A_WIDTH = D_MODEL // 4
A_HEADS = A_WIDTH // A_HEAD_DIM
IDX_HEADS = 8
IDX_DIM = 64
TOPK_MAX = 256

S5_GROUP = 16
S5_WIDTH = D_MODEL // 4
S5_GROUPS = S5_WIDTH // S5_GROUP
S5_STATE = 64
S5_NSTATE = S5_GROUPS * S5_STATE

C_QK_DIM = 64
C_V_DIM = 2 * C_QK_DIM
C_WIDTH = D_MODEL // 2
C_HEADS = C_WIDTH // C_V_DIM

X_HEADS = 4
X_HEAD_DIM = D_MODEL // X_HEADS
D_FF = 4 * D_MODEL

LANES = 128
SUBLANES = 8
VMEM_LIMIT = 56 * 1024 * 1024

KIW_W = LANES
OFF_QA = 0
OFF_KA = OFF_QA + A_WIDTH
OFF_VA = OFF_KA + A_WIDTH
OFF_QI = OFF_VA + A_WIDTH
OFF_KIW = OFF_QI + IDX_HEADS * IDX_DIM
OFF_US = OFF_KIW + KIW_W
OFF_QC = OFF_US + S5_WIDTH
OFF_KC = OFF_QC + C_WIDTH
OFF_VC = OFF_KC + C_WIDTH
IN_PACKED = OFF_VC + C_WIDTH
IN_SRC_SPLIT = 3 * A_WIDTH + IDX_HEADS * IDX_DIM + IDX_DIM + IDX_HEADS

INT_MIN = -2 ** 31
INT_MAX = 2 ** 31 - 1


def _cparams(sem):
    return pltpu.CompilerParams(dimension_semantics=sem, vmem_limit_bytes=VMEM_LIMIT)


def _resident(shape, index_map):
    return pl.BlockSpec(shape, index_map, pipeline_mode=pl.Buffered(1))


def _rms(x, g):
    inv = lax.rsqrt(jnp.mean(x * x, axis=-1, keepdims=True) + EPS)
    return x * inv * g


def _dot(a, b):
    return jnp.dot(a, b, preferred_element_type=F32)


def _dot_nt(a, b):
    return lax.dot_general(a, b, (((1,), (1,)), ((), ())), preferred_element_type=F32)


def _inproj_body(x_ref, g_ref, w_ref, c128_ref, s128_ref, c64_ref, sa64_ref, sb64_ref,
                 c64k_ref, sa64k_ref, sb64k_ref,
                 qa_ref, ka_ref, va_ref, qi_ref, kiw_ref, us_ref, qc_ref, kc_ref, vc_ref):
    hn = _rms(x_ref[...], g_ref[...]).astype(BF16)
    c128, s128 = c128_ref[...], s128_ref[...]
    c64, sa64, sb64 = c64_ref[...], sa64_ref[...], sb64_ref[...]

    def rope128(t):
        return t * c128 + pltpu.roll(t, 64, 1) * s128

    def rope64(t):
        return t * c64 + pltpu.roll(t, 96, 1) * sa64 + pltpu.roll(t, 32, 1) * sb64

    def rope64_kiw(t):
        return t * c64k_ref[...] + pltpu.roll(t, 96, 1) * sa64k_ref[...] + pltpu.roll(t, 32, 1) * sb64k_ref[...]

    def emit(off, width, out_ref, fn, scale=None):
        step = 512 if width >= 512 else width
        for c0 in range(0, width, step):
            t = _dot(hn, w_ref[:, off + c0:off + c0 + step])
            for s0 in range(0, step, LANES):
                v = t[:, s0:s0 + LANES]
                if fn is not None:
                    v = fn(v)
                if scale is not None:
                    v = v * scale
                out_ref[:, c0 + s0:c0 + s0 + LANES] = v.astype(out_ref.dtype)

    emit(OFF_QA, A_WIDTH, qa_ref, rope128)
    emit(OFF_KA, A_WIDTH, ka_ref, rope128)
    emit(OFF_VA, A_WIDTH, va_ref, None)
    emit(OFF_QI, IDX_HEADS * IDX_DIM, qi_ref, rope64)
    emit(OFF_KIW, KIW_W, kiw_ref, rope64_kiw)
    emit(OFF_US, S5_WIDTH, us_ref, None)
    emit(OFF_QC, C_WIDTH, qc_ref, rope64, C_QK_DIM ** -0.5)
    emit(OFF_KC, C_WIDTH, kc_ref, rope64)
    emit(OFF_VC, C_WIDTH, vc_ref, None)


def _in_proj(x, g, w_packed, layer, tabs, tm):
    L = x.shape[0]
    row = lambda i: (i, 0)
    tab_spec = pl.BlockSpec((tm, LANES), row)
    outs = [
        (A_WIDTH, BF16), (A_WIDTH, BF16), (A_WIDTH, BF16), (IDX_HEADS * IDX_DIM, BF16),
        (KIW_W, F32), (S5_WIDTH, F32), (C_WIDTH, BF16), (C_WIDTH, BF16), (C_WIDTH, BF16),
    ]
    return pl.pallas_call(
        _inproj_body,
        grid=(L // tm,),
        in_specs=[
            pl.BlockSpec((tm, D_MODEL), row),
            _resident((None, 1, D_MODEL), lambda i: (layer, 0, 0)),
            _resident((None, D_MODEL, IN_PACKED), lambda i: (layer, 0, 0)),
        ] + [tab_spec] * 8 + [
        ],
        out_specs=[pl.BlockSpec((tm, w), row) for w, _ in outs],
        out_shape=[jax.ShapeDtypeStruct((L, w), dt) for w, dt in outs],
        compiler_params=_cparams(("parallel",)),
        name="in_proj",
    )(x, g, w_packed, *tabs)


def _rope_tables(L):
    pos = jnp.arange(L, dtype=F32)[:, None]

    def cs(dim):
        inv = ROPE_THETA ** (-jnp.arange(0, dim, 2, dtype=F32) / dim)
        ang = pos * inv[None, :]
        return jnp.cos(ang), jnp.sin(ang)

    c, s = cs(A_HEAD_DIM)
    c128 = jnp.concatenate([c, c], axis=1)
    s128 = jnp.concatenate([-s, s], axis=1)
    c, s = cs(IDX_DIM)
    z = jnp.zeros_like(s)
    c64 = jnp.tile(jnp.concatenate([c, c], axis=1), (1, 2))
    sa64 = jnp.tile(jnp.concatenate([-s, z], axis=1), (1, 2))
    sb64 = jnp.tile(jnp.concatenate([z, s], axis=1), (1, 2))
    lane = jnp.arange(LANES)[None, :]
    c64k = jnp.where(lane < IDX_DIM, c64, 1.0)
    sa64k = jnp.where(lane < IDX_DIM, sa64, 0.0)
    sb64k = jnp.where(lane < IDX_DIM, sb64, 0.0)
    return (c128, s128, c64, sa64, sb64), (c64k, sa64k, sb64k)


def _sortable(v):
    k = lax.bitcast_convert_type(v, jnp.int32)
    return k ^ ((k >> 31) & jnp.int32(INT_MAX))


def _dsa_index_body(qi_ref, kiw_ref, ke_ref, ko_ref, bias_ref,
                    skey_ref, wb_ref, lo_ref, cut_ref, *, tq, half, top_k, n_tiles, idx_bits):
    i = pl.program_id(0)
    n_live = i + 1
    q_pos = i * tq + lax.broadcasted_iota(jnp.int32, (tq, LANES), 0)
    lane = lax.broadcasted_iota(jnp.int32, (tq, LANES), 1)
    lane_half = lax.broadcasted_iota(jnp.int32, (half, LANES), 1)
    slabs = tq // LANES

    w_scale = (IDX_HEADS ** -0.5) * (IDX_DIM ** -0.5)
    kiw = kiw_ref[...]
    for h in range(IDX_HEADS):
        col = IDX_DIM + h
        wb_ref[h] = jnp.broadcast_to(kiw[:, col:col + 1] * w_scale, (tq, LANES))

    def score_tile(jt, _):
        ke = ke_ref[jt]
        ko = ko_ref[jt]
        acc = [jnp.zeros((tq, LANES), F32) for _ in range(slabs)]
        for hp in range(IDX_HEADS // 2):
            qp = qi_ref[:, hp * LANES:(hp + 1) * LANES]
            d0 = _dot(qp, ke)
            d1 = _dot(qp, ko)
            w0 = wb_ref[2 * hp]
            w1 = wb_ref[2 * hp + 1]
            for c in range(slabs):
                sl = slice(c * LANES, (c + 1) * LANES)
                acc[c] = acc[c] + w0 * jnp.maximum(d0[:, sl], 0.0) + w1 * jnp.maximum(d1[:, sl], 0.0)
        for c in range(slabs):
            k_pos = jt * tq + c * LANES + lane
            allowed = (k_pos >> CHUNK_SHIFT) <= (q_pos >> CHUNK_SHIFT)
            skey_ref[jt, :, c * LANES:(c + 1) * LANES] = _sortable(jnp.where(allowed, acc[c], NEG))
        return 0

    lax.fori_loop(0, n_live, score_tile, 0)

    def count(r0, pred):
        def body(jt, acc):
            t = skey_ref[jt, r0:r0 + half, :]
            for c in range(slabs):
                k_pos = jt * tq + c * LANES + lane_half
                acc = acc + jnp.where(pred(t[:, c * LANES:(c + 1) * LANES], k_pos), 1.0, 0.0)
            return acc
        acc = lax.fori_loop(0, n_live, body, jnp.zeros((half, LANES), F32))
        return jnp.sum(acc, axis=1, keepdims=True)

    kf = float(top_k)
    for r0 in range(0, tq, half):
        def bit_step(b, carry):
            lo, cnt_lo = carry
            cand = lo + (jnp.int32(1) << (31 - b))
            cb = jnp.broadcast_to(cand, (half, LANES))
            cnt = count(r0, lambda t, _: t >= cb)
            ok = cnt >= kf
            return jnp.where(ok, cand, lo), jnp.where(ok, cnt, cnt_lo)

        lo0 = jnp.full((half, 1), INT_MIN, jnp.int32)
        cnt0 = jnp.full((half, 1), 1.0, F32) * (n_live * tq).astype(F32)
        lo, cnt_lo = lax.fori_loop(0, 32, bit_step, (lo0, cnt0))
        lo_ref[r0:r0 + half, :] = lo
        cut_ref[r0:r0 + half, :] = jnp.full((half, 1), INT_MAX, jnp.int32)

        @pl.when(jnp.max(cnt_lo) > kf)
        def _():
            lob = jnp.broadcast_to(lo, (half, LANES))
            need = kf - count(r0, lambda t, _: t > lob)

            def idx_step(b, below):
                cand = below + (jnp.int32(1) << (idx_bits - 1 - b))
                cb = jnp.broadcast_to(cand, (half, LANES))
                cnt = count(r0, lambda t, p: (t == lob) & (p <= cb))
                return jnp.where(cnt < need, cand, below)

            below = lax.fori_loop(0, idx_bits, idx_step, jnp.full((half, 1), -1, jnp.int32))
            cut_ref[r0:r0 + half, :] = jnp.where(cnt_lo > kf, below + 1, INT_MAX)

    lob = jnp.broadcast_to(lo_ref[...], (tq, LANES))
    cutb = jnp.broadcast_to(cut_ref[...], (tq, LANES))

    def write_tile(jt, _):
        for c in range(slabs):
            sl = slice(c * LANES, (c + 1) * LANES)
            t = skey_ref[jt, :, sl]
            k_pos = jt * tq + c * LANES + lane
            allowed = (k_pos >> CHUNK_SHIFT) <= (q_pos >> CHUNK_SHIFT)
            sel = ((t > lob) | ((t == lob) & (k_pos <= cutb))) & allowed
            bias_ref[jt, :, sl] = jnp.where(sel, 0.0, NEG).astype(BF16)
        return 0

    lax.fori_loop(0, n_live, write_tile, 0)

    def blank_tile(jt, _):
        bias_ref[jt] = jnp.full((tq, tq), NEG, BF16)
        return 0

    lax.fori_loop(n_live, n_tiles, blank_tile, 0)


def _dsa_index(qi, kiw, ke, ko, tq):
    L = qi.shape[0]
    n_tiles = L // tq
    top_k = min(TOPK_MAX, L // 4)
    body = functools.partial(_dsa_index_body, tq=tq, half=min(tq, 128), top_k=top_k,
                             n_tiles=n_tiles, idx_bits=max(1, (L - 1).bit_length()))
    return pl.pallas_call(
        body,
        grid=(n_tiles,),
        in_specs=[
            pl.BlockSpec((tq, IDX_HEADS * IDX_DIM), lambda i: (i, 0)),
            pl.BlockSpec((tq, KIW_W), lambda i: (i, 0)),
            _resident((n_tiles, LANES, tq), lambda i: (0, 0, 0)),
            _resident((n_tiles, LANES, tq), lambda i: (0, 0, 0)),
        ],
        out_specs=pl.BlockSpec((None, n_tiles, tq, tq), lambda i: (i, 0, 0, 0)),
        out_shape=jax.ShapeDtypeStruct((n_tiles, n_tiles, tq, tq), BF16),
        scratch_shapes=[
            pltpu.VMEM((n_tiles, tq, tq), jnp.int32),
            pltpu.VMEM((IDX_HEADS, tq, LANES), F32),
            pltpu.VMEM((tq, 1), jnp.int32),
            pltpu.VMEM((tq, 1), jnp.int32),
        ],
        compiler_params=_cparams(("parallel",)),
        name="dsa_index",
    )(qi, kiw, ke, ko)


def _flash_update(s, v_tile, m_ref, l_ref, acc_ref, slot):
    m_prev = m_ref[slot]
    m_new = jnp.maximum(m_prev, jnp.max(s, axis=1, keepdims=True))
    alpha = jnp.exp(m_prev - m_new)
    p = jnp.exp(s - m_new)
    l_ref[slot] = alpha * l_ref[slot] + jnp.sum(p, axis=1, keepdims=True)
    acc_ref[slot] = alpha * acc_ref[slot] + _dot(p.astype(BF16), v_tile)
    m_ref[slot] = m_new


def _flash_init(m_ref, l_ref, acc_ref):
    m_ref[...] = jnp.full(m_ref.shape, NEG, F32)
    l_ref[...] = jnp.zeros(l_ref.shape, F32)
    acc_ref[...] = jnp.zeros(acc_ref.shape, F32)


def _dsa_attn_body(q_ref, k_ref, v_ref, bias_ref, o_ref, m_ref, l_ref, acc_ref, *, tq):
    i = pl.program_id(0)
    _flash_init(m_ref, l_ref, acc_ref)
    scale = A_HEAD_DIM ** -0.5

    def tile(jt, _):
        rows = pl.ds(pl.multiple_of(jt * tq, tq), tq)
        b = bias_ref[jt].astype(F32)
        for h in range(A_HEADS):
            hs = slice(h * A_HEAD_DIM, (h + 1) * A_HEAD_DIM)
            s = _dot_nt(q_ref[:, hs], k_ref[rows, hs]) * scale + b
            _flash_update(s, v_ref[rows, hs], m_ref, l_ref, acc_ref, h)
        return 0

    lax.fori_loop(0, i + 1, tile, 0)
    for h in range(A_HEADS):
        hs = slice(h * A_HEAD_DIM, (h + 1) * A_HEAD_DIM)
        o_ref[:, hs] = (acc_ref[h] / l_ref[h]).astype(o_ref.dtype)


def _dsa_attn(q, k, v, bias, tq):
    L = q.shape[0]
    n_tiles = L // tq
    return pl.pallas_call(
        functools.partial(_dsa_attn_body, tq=tq),
        grid=(n_tiles,),
        in_specs=[
            pl.BlockSpec((tq, A_WIDTH), lambda i: (i, 0)),
            _resident((L, A_WIDTH), lambda i: (0, 0)),
            _resident((L, A_WIDTH), lambda i: (0, 0)),
            pl.BlockSpec((None, n_tiles, tq, tq), lambda i: (i, 0, 0, 0)),
        ],
        out_specs=pl.BlockSpec((tq, A_WIDTH), lambda i: (i, 0)),
        out_shape=jax.ShapeDtypeStruct((L, A_WIDTH), BF16),
        scratch_shapes=[
            pltpu.VMEM((A_HEADS, tq, 1), F32),
            pltpu.VMEM((A_HEADS, tq, 1), F32),
            pltpu.VMEM((A_HEADS, tq, A_HEAD_DIM), F32),
        ],
        compiler_params=_cparams(("parallel",)),
        name="dsa_attn",
    )(q, k, v, bias)


def _diff_attn_body(lq1_ref, lk1_ref, lq2_ref, lk2_ref, g_ref, q_ref, k_ref, v_ref, o_ref,
                    m_ref, l_ref, acc_ref, *, tq, lam_init):
    i = pl.program_id(1)
    _flash_init(m_ref, l_ref, acc_ref)
    q = q_ref[...]
    lane = lax.broadcasted_iota(jnp.int32, q.shape, 1)
    zero = jnp.zeros_like(q)
    qs = (jnp.where(lane < C_QK_DIM, q, zero), jnp.where(lane >= C_QK_DIM, q, zero))

    def step(jt, mask):
        rows = pl.ds(pl.multiple_of(jt * tq, tq), tq)
        kt = k_ref[rows, :]
        vt = v_ref[rows, :]
        for c in range(2):
            s = _dot_nt(qs[c], kt)
            if mask is not None:
                s = jnp.where(mask, s, NEG)
            _flash_update(s, vt, m_ref, l_ref, acc_ref, c)

    def full_tile(jt, _):
        step(jt, None)
        return 0

    lax.fori_loop(0, i, full_tile, 0)
    r = lax.broadcasted_iota(jnp.int32, (tq, tq), 0)
    c = lax.broadcasted_iota(jnp.int32, (tq, tq), 1)
    step(i, (c >> CHUNK_SHIFT) <= (r >> CHUNK_SHIFT))

    lam = (jnp.exp(jnp.sum(lq1_ref[...] * lk1_ref[...], axis=1, keepdims=True))
           - jnp.exp(jnp.sum(lq2_ref[...] * lk2_ref[...], axis=1, keepdims=True)) + lam_init)
    o = acc_ref[0] / l_ref[0] - lam * (acc_ref[1] / l_ref[1])
    o_ref[...] = (_rms(o, g_ref[...]) * (1.0 - lam_init)).astype(o_ref.dtype)


def _diff_attn(q, k, v, lam_vecs, sub_gain, layer, lam_init, tq):
    L = q.shape[0]
    vec = lambda w: _resident((None, 1, w), lambda h, i: (layer, 0, 0))
    return pl.pallas_call(
        functools.partial(_diff_attn_body, tq=tq, lam_init=lam_init),
        grid=(C_HEADS, L // tq),
        in_specs=[
            vec(C_QK_DIM), vec(C_QK_DIM), vec(C_QK_DIM), vec(C_QK_DIM), vec(C_V_DIM),
            pl.BlockSpec((tq, C_V_DIM), lambda h, i: (i, h)),
            pl.BlockSpec((L, C_V_DIM), lambda h, i: (0, h)),
            pl.BlockSpec((L, C_V_DIM), lambda h, i: (0, h)),
        ],
        out_specs=pl.BlockSpec((tq, C_V_DIM), lambda h, i: (i, h)),
        out_shape=jax.ShapeDtypeStruct((L, C_WIDTH), BF16),
        scratch_shapes=[
            pltpu.VMEM((2, tq, 1), F32),
            pltpu.VMEM((2, tq, 1), F32),
            pltpu.VMEM((2, tq, C_V_DIM), F32),
        ],
        compiler_params=_cparams(("parallel", "parallel")),
        name="diff_attn",
    )(*lam_vecs, sub_gain, q, k, v)


def _s5_prep_body(are_ref, aim_ref, ldt_ref, bre_ref, bim_ref, bb_ref, lvl_ref, pw_ref):
    a_re, a_im = are_ref[...], aim_ref[...]
    dt = jnp.exp(ldt_ref[...])
    mag = jnp.exp(a_re * dt)
    lb_re = mag * jnp.cos(a_im * dt)
    lb_im = mag * jnp.sin(a_im * dt)
    den = a_re * a_re + a_im * a_im
    f_re = ((lb_re - 1.0) * a_re + lb_im * a_im) / den
    f_im = (lb_im * a_re - (lb_re - 1.0) * a_im) / den
    b_re, b_im = bre_ref[...], bim_ref[...]
    bb_ref[:, :S5_NSTATE] = (f_re * b_re - f_im * b_im).astype(BF16)
    bb_ref[:, S5_NSTATE:] = (f_re * b_im + f_im * b_re).astype(BF16)

    pows = [(lb_re, lb_im)]
    for _ in range(SUBLANES - 1):
        pr, pi = pows[-1]
        pows.append((pr * lb_re - pi * lb_im, pr * lb_im + pi * lb_re))
    row = lax.broadcasted_iota(jnp.int32, (SUBLANES, S5_NSTATE), 0)
    zero = jnp.zeros((SUBLANES, S5_NSTATE), F32)
    for lvl in range(3):
        sh = 1 << lvl
        pr, pi = pows[sh - 1]
        lvl_ref[lvl, 0] = jnp.where(row >= sh, pr, zero)
        lvl_ref[lvl, 1] = jnp.where(row >= sh, pi, zero)
    cr, ci = zero, zero
    for r in range(SUBLANES):
        cr = jnp.where(row == r, pows[r][0], cr)
        ci = jnp.where(row == r, pows[r][1], ci)
    pw_ref[0] = cr
    pw_ref[1] = ci


def _s5_prep(a_re, a_im, log_dt, b_re_bd, b_im_bd):
    depth = a_re.shape[0]
    vec = pl.BlockSpec((None, 1, S5_NSTATE), lambda l: (l, 0, 0))
    mat = pl.BlockSpec((None, S5_WIDTH, S5_NSTATE), lambda l: (l, 0, 0))
    return pl.pallas_call(
        _s5_prep_body,
        grid=(depth,),
        in_specs=[vec, vec, vec, mat, mat],
        out_specs=[
            pl.BlockSpec((None, S5_WIDTH, 2 * S5_NSTATE), lambda l: (l, 0, 0)),
            pl.BlockSpec((None, 3, 2, SUBLANES, S5_NSTATE), lambda l: (l, 0, 0, 0, 0)),
            pl.BlockSpec((None, 2, SUBLANES, S5_NSTATE), lambda l: (l, 0, 0, 0)),
        ],
        out_shape=[
            jax.ShapeDtypeStruct((depth, S5_WIDTH, 2 * S5_NSTATE), BF16),
            jax.ShapeDtypeStruct((depth, 3, 2, SUBLANES, S5_NSTATE), F32),
            jax.ShapeDtypeStruct((depth, 2, SUBLANES, S5_NSTATE), F32),
        ],
        compiler_params=_cparams(("parallel",)),
        name="s5_prep",
    )(a_re, a_im, log_dt, b_re_bd, b_im_bd)


S5_LANE_CHUNK = 512


def _s5_body(u_ref, bb_ref, lvl_ref, pw_ref, cre_ref, cim_ref, d_ref, wg_ref, bg_ref, o_ref,
             x_ref, carry_ref, *, tl):
    @pl.when(pl.program_id(0) == 0)
    def _():
        carry_ref[...] = jnp.zeros(carry_ref.shape, F32)

    u = u_ref[...]
    x_ref[...] = _dot(u.astype(BF16), bb_ref[...])

    for c0 in range(0, S5_NSTATE, S5_LANE_CHUNK):
        re_sl = slice(c0, c0 + S5_LANE_CHUNK)
        im_sl = slice(S5_NSTATE + c0, S5_NSTATE + c0 + S5_LANE_CHUNK)

        def block(t, carry, re_sl=re_sl, im_sl=im_sl):
            c_re, c_im = carry
            rows = pl.ds(pl.multiple_of(t * SUBLANES, SUBLANES), SUBLANES)
            re = x_ref[rows, re_sl]
            im = x_ref[rows, im_sl]
            for lvl in range(3):
                sh = 1 << lvl
                s_re = pltpu.roll(re, sh, 0)
                s_im = pltpu.roll(im, sh, 0)
                a_re = lvl_ref[lvl, 0, :, re_sl]
                a_im = lvl_ref[lvl, 1, :, re_sl]
                re, im = re + (a_re * s_re - a_im * s_im), im + (a_re * s_im + a_im * s_re)
            p_re = pw_ref[0, :, re_sl]
            p_im = pw_ref[1, :, re_sl]
            re, im = re + (p_re * c_re - p_im * c_im), im + (p_re * c_im + p_im * c_re)
            x_ref[rows, re_sl] = re
            x_ref[rows, im_sl] = im
            last = SUBLANES - 1
            return (jnp.broadcast_to(re[last:, :], re.shape), jnp.broadcast_to(im[last:, :], im.shape))

        carry = lax.fori_loop(0, tl // SUBLANES, block,
                              (carry_ref[0, :, re_sl], carry_ref[1, :, re_sl]))
        carry_ref[0, :, re_sl] = carry[0]
        carry_ref[1, :, re_sl] = carry[1]

    y = (_dot(x_ref[:, :S5_NSTATE].astype(BF16), cre_ref[...])
         - _dot(x_ref[:, S5_NSTATE:].astype(BF16), cim_ref[...])
         + d_ref[...] * u)
    y = 0.5 * y * (1.0 + jnp.tanh(math.sqrt(2.0 / math.pi) * (y + 0.044715 * (y * y * y))))
    z = _dot(y.astype(BF16), wg_ref[...]) + bg_ref[...]
    o_ref[...] = (y * (1.0 / (1.0 + jnp.exp(-z)))).astype(o_ref.dtype)


def _s5_mix(u, bb, lvl, pw, c_re_bd, c_im_bd, d_row, w_glu, b_glu, layer, tl):
    L = u.shape[0]
    lay = lambda *rest: (lambda t: (layer,) + rest)
    return pl.pallas_call(
        functools.partial(_s5_body, tl=tl),
        grid=(L // tl,),
        in_specs=[
            pl.BlockSpec((tl, S5_WIDTH), lambda t: (t, 0)),
            _resident((None, S5_WIDTH, 2 * S5_NSTATE), lay(0, 0)),
            _resident((None, 3, 2, SUBLANES, S5_NSTATE), lay(0, 0, 0, 0)),
            _resident((None, 2, SUBLANES, S5_NSTATE), lay(0, 0, 0)),
            _resident((None, S5_NSTATE, S5_WIDTH), lay(0, 0)),
            _resident((None, S5_NSTATE, S5_WIDTH), lay(0, 0)),
            _resident((None, 1, S5_WIDTH), lay(0, 0)),
            _resident((None, S5_WIDTH, S5_WIDTH), lay(0, 0)),
            _resident((None, 1, S5_WIDTH), lay(0, 0)),
        ],
        out_specs=pl.BlockSpec((tl, S5_WIDTH), lambda t: (t, 0)),
        out_shape=jax.ShapeDtypeStruct((L, S5_WIDTH), BF16),
        scratch_shapes=[
            pltpu.VMEM((tl, 2 * S5_NSTATE), F32),
            pltpu.VMEM((2, SUBLANES, S5_NSTATE), F32),
        ],
        compiler_params=_cparams(("arbitrary",)),
        name="s5_mix",
    )(u, bb, lvl, pw, c_re_bd, c_im_bd, d_row, w_glu, b_glu)


def _outproj_body(x_ref, oa_ref, os_ref, oc_ref, w_ref, o_ref):
    a0, a1, a2 = A_WIDTH, A_WIDTH + S5_WIDTH, A_WIDTH + S5_WIDTH + C_WIDTH
    o_ref[...] = (x_ref[...] + _dot(oa_ref[...], w_ref[:a0, :]) + _dot(os_ref[...], w_ref[a0:a1, :])
                  + _dot(oc_ref[...], w_ref[a1:a2, :]))


def _out_proj(x, o_a, o_s, o_c, w_out, layer, tm):
    L = x.shape[0]
    row = lambda i: (i, 0)
    return pl.pallas_call(
        _outproj_body,
        grid=(L // tm,),
        in_specs=[
            pl.BlockSpec((tm, D_MODEL), row),
            pl.BlockSpec((tm, A_WIDTH), row),
            pl.BlockSpec((tm, S5_WIDTH), row),
            pl.BlockSpec((tm, C_WIDTH), row),
            _resident((None, D_MODEL, D_MODEL), lambda i: (layer, 0, 0)),
        ],
        out_specs=pl.BlockSpec((tm, D_MODEL), row),
        out_shape=jax.ShapeDtypeStruct((L, D_MODEL), F32),
        compiler_params=_cparams(("parallel",)),
        name="out_proj",
    )(x, o_a, o_s, o_c, w_out)


def _norm_matmul_body(x_ref, g_ref, w_ref, o_ref):
    o_ref[...] = _dot(_rms(x_ref[...], g_ref[...]).astype(BF16), w_ref[...]).astype(o_ref.dtype)


def _norm_matmul(x, g, w, layer, tn):
    M, K = x.shape
    N = w.shape[-1]
    return pl.pallas_call(
        _norm_matmul_body,
        grid=(N // tn,),
        in_specs=[
            _resident((M, K), lambda j: (0, 0)),
            _resident((None, 1, K), lambda j: (layer, 0, 0)),
            pl.BlockSpec((None, K, tn), lambda j: (layer, 0, j)),
        ],
        out_specs=pl.BlockSpec((M, tn), lambda j: (0, j)),
        out_shape=jax.ShapeDtypeStruct((M, N), BF16),
        compiler_params=_cparams(("parallel",)),
        name="mem_proj",
    )(x, g, w)


def _xattn_body(x_ref, g_ref, wq_ref, k_ref, v_ref, wo_ref, o_ref):
    x = x_ref[...]
    q = _dot(_rms(x, g_ref[...]).astype(BF16), wq_ref[...]).astype(BF16)
    scale = X_HEAD_DIM ** -0.5
    out = x
    for h in range(X_HEADS):
        hs = slice(h * X_HEAD_DIM, (h + 1) * X_HEAD_DIM)
        s = _dot_nt(q[:, hs], k_ref[:, hs]) * scale
        p = jnp.exp(s - jnp.max(s, axis=1, keepdims=True))
        p = p / jnp.sum(p, axis=1, keepdims=True)
        o_h = _dot(p.astype(BF16), v_ref[:, hs]).astype(BF16)
        out = out + _dot(o_h, wo_ref[hs, :])
    o_ref[...] = out


def _cross_attn(x, g, wq, k_mem, v_mem, wo, layer, tm):
    L = x.shape[0]
    M = k_mem.shape[0]
    row = lambda i: (i, 0)
    return pl.pallas_call(
        _xattn_body,
        grid=(L // tm,),
        in_specs=[
            pl.BlockSpec((tm, D_MODEL), row),
            _resident((None, 1, D_MODEL), lambda i: (layer, 0, 0)),
            _resident((None, D_MODEL, D_MODEL), lambda i: (layer, 0, 0)),
            _resident((M, D_MODEL), lambda i: (0, 0)),
            _resident((M, D_MODEL), lambda i: (0, 0)),
            _resident((None, D_MODEL, D_MODEL), lambda i: (layer, 0, 0)),
        ],
        out_specs=pl.BlockSpec((tm, D_MODEL), row),
        out_shape=jax.ShapeDtypeStruct((L, D_MODEL), F32),
        compiler_params=_cparams(("parallel",)),
        name="cross_attn",
    )(x, g, wq, k_mem, v_mem, wo)


def _mlp_body(x_ref, g_ref, w1_ref, w2_ref, o_ref, hn_ref):
    @pl.when(pl.program_id(1) == 0)
    def _():
        x = x_ref[...]
        hn_ref[...] = _rms(x, g_ref[...]).astype(BF16)
        o_ref[...] = x

    a = jnp.maximum(_dot(hn_ref[...], w1_ref[...]), 0.0)
    o_ref[...] += _dot((a * a).astype(BF16), w2_ref[...])


def _mlp(x, g, w1, w2, layer, tm, tf):
    L = x.shape[0]
    return pl.pallas_call(
        _mlp_body,
        grid=(L // tm, D_FF // tf),
        in_specs=[
            pl.BlockSpec((tm, D_MODEL), lambda i, f: (i, 0)),
            _resident((None, 1, D_MODEL), lambda i, f: (layer, 0, 0)),
            pl.BlockSpec((None, D_MODEL, tf), lambda i, f: (layer, 0, f)),
            pl.BlockSpec((None, tf, D_MODEL), lambda i, f: (layer, f, 0)),
        ],
        out_specs=pl.BlockSpec((tm, D_MODEL), lambda i, f: (i, 0)),
        out_shape=jax.ShapeDtypeStruct((L, D_MODEL), F32),
        scratch_shapes=[pltpu.VMEM((tm, D_MODEL), BF16)],
        compiler_params=_cparams(("parallel", "arbitrary")),
        name="mlp",
    )(x, g, w1, w2)


def _final_norm_body(x_ref, g_ref, o_ref):
    o_ref[...] = _rms(x_ref[...], g_ref[...])


def _final_norm(x, g, tm):
    L = x.shape[0]
    return pl.pallas_call(
        _final_norm_body,
        grid=(L // tm,),
        in_specs=[pl.BlockSpec((tm, D_MODEL), lambda i: (i, 0)),
                  _resident((1, D_MODEL), lambda i: (0, 0))],
        out_specs=pl.BlockSpec((tm, D_MODEL), lambda i: (i, 0)),
        out_shape=jax.ShapeDtypeStruct((L, D_MODEL), F32),
        compiler_params=_cparams(("parallel",)),
        name="final_norm",
    )(x, g)


def _block_diag_in(b):
    eye = jnp.eye(S5_GROUPS, dtype=b.dtype)
    t = jnp.transpose(b, (0, 1, 3, 2))
    bd = t[:, :, :, None, :] * eye[None, :, None, :, None]
    return bd.reshape(b.shape[0], S5_WIDTH, S5_NSTATE)


def _block_diag_out(c):
    eye = jnp.eye(S5_GROUPS, dtype=c.dtype)
    t = jnp.transpose(c, (0, 1, 3, 2))
    bd = t[:, :, :, None, :] * eye[None, :, None, :, None]
    return bd.reshape(c.shape[0], S5_NSTATE, S5_WIDTH)


def _pick_tile(n, want):
    t = min(n, want)
    assert n % t == 0, (n, t)
    return t


def kernel(x, mem, norm_mix, w_in, s5_a_re, s5_a_im, s5_log_dt, s5_b_re, s5_b_im, s5_c_re, s5_c_im, s5_d, s5_w_glu, s5_b_glu, diff_lam_q1, diff_lam_k1, diff_lam_q2, diff_lam_k2, diff_subln, w_out, norm_xattn, norm_mem, xattn_q, xattn_k, xattn_v, xattn_o, norm_mlp, w_ff1, w_ff2, norm_final):
    B, L, _ = x.shape
    assert B == 1
    depth = w_in.shape[0]
    tq = _pick_tile(L, 256)
    n_tiles = L // tq

    pad = jnp.zeros((depth, D_MODEL, KIW_W - IDX_DIM - IDX_HEADS), w_in.dtype)
    w_in_p = jnp.concatenate([w_in[..., :IN_SRC_SPLIT], pad, w_in[..., IN_SRC_SPLIT:]], axis=-1).astype(BF16)
    w_out_b = w_out.astype(BF16)
    wq_b, wk_b, wv_b, wo_b = (w.astype(BF16) for w in (xattn_q, xattn_k, xattn_v, xattn_o))
    w1_b, w2_b = w_ff1.astype(BF16), w_ff2.astype(BF16)
    wg_b = s5_w_glu.astype(BF16)
    row3 = lambda a: a.reshape(depth, 1, -1)

    tabs, tabs_k = _rope_tables(L)

    rep = lambda a: row3(a.astype(F32))
    a_re, a_im = rep(s5_a_re), rep(s5_a_im)
    log_dt = row3(jnp.broadcast_to(s5_log_dt.astype(F32)[:, :, None], (depth, S5_GROUPS, S5_STATE)))
    bb, lvl, pw = _s5_prep(a_re, a_im, log_dt, _block_diag_in(s5_b_re.astype(F32)),
                           _block_diag_in(s5_b_im.astype(F32)))
    c_re_bd = _block_diag_out(s5_c_re).astype(BF16)
    c_im_bd = _block_diag_out(s5_c_im).astype(BF16)
    d_row = row3(s5_d.astype(F32))
    lam_vecs = tuple(row3(v.astype(F32)) for v in (diff_lam_q1, diff_lam_k1, diff_lam_q2, diff_lam_k2))

    xs = x[0]
    mem2 = mem[0]
    for l in range(depth):
        qa, ka, va, qi, kiw, us, qc, kc, vc = _in_proj(xs, row3(norm_mix), w_in_p, l, tabs + tabs_k, _pick_tile(L, 256))

        ki_t = jnp.transpose(kiw[:, :IDX_DIM]).astype(BF16)
        ki_t = jnp.transpose(ki_t.reshape(IDX_DIM, n_tiles, tq), (1, 0, 2))
        zeros = jnp.zeros_like(ki_t)
        ke = jnp.concatenate([ki_t, zeros], axis=1)
        ko = jnp.concatenate([zeros, ki_t], axis=1)
        bias = _dsa_index(qi, kiw, ke, ko, tq)
        o_a = _dsa_attn(qa, ka, va, bias, tq)

        o_s = _s5_mix(us, bb, lvl, pw, c_re_bd, c_im_bd, d_row, wg_b, row3(s5_b_glu.astype(F32)), l,
                      _pick_tile(L, 512))

        lam_init = 0.8 - 0.6 * math.exp(-0.3 * l)
        o_c = _diff_attn(qc, kc, vc, lam_vecs, row3(diff_subln.astype(F32)), l, lam_init, tq)

        xs = _out_proj(xs, o_a, o_s, o_c, w_out_b, l, _pick_tile(L, 512))

        k_mem = _norm_matmul(mem2, row3(norm_mem), wk_b, l, 512)
        v_mem = _norm_matmul(mem2, row3(norm_mem), wv_b, l, 512)
        xs = _cross_attn(xs, row3(norm_xattn), wq_b, k_mem, v_mem, wo_b, l, _pick_tile(L, 512))

        xs = _mlp(xs, row3(norm_mlp), w1_b, w2_b, l, _pick_tile(L, 512), 512)

    return _final_norm(xs, norm_final.reshape(1, -1), _pick_tile(L, 512))[None]
```

```python
import functools
import math

import jax
import jax.numpy as jnp
from jax import lax
from jax.experimental import pallas as pl
from jax.experimental.pallas import tpu as pltpu

F32 = jnp.float32
BF16 = jnp.bfloat16

D_MODEL = 2048
CHUNK = 64
CHUNK_SHIFT = CHUNK.bit_length() - 1
assert 1 << CHUNK_SHIFT == CHUNK
ROPE_THETA = 10000.0
EPS = 1e-6
NEG = -1e30

A_HEAD_DIM = 128
A_WIDTH = D_MODEL // 4
A_HEADS = A_WIDTH // A_HEAD_DIM
IDX_HEADS = 8
IDX_DIM = 64
TOPK_MAX = 256

S5_GROUP = 16
S5_WIDTH = D_MODEL // 4
S5_GROUPS = S5_WIDTH // S5_GROUP
S5_STATE = 64
S5_NSTATE = S5_GROUPS * S5_STATE

C_QK_DIM = 64
C_V_DIM = 2 * C_QK_DIM
C_WIDTH = D_MODEL // 2
C_HEADS = C_WIDTH // C_V_DIM

X_HEADS = 4
X_HEAD_DIM = D_MODEL // X_HEADS
D_FF = 4 * D_MODEL

LANES = 128
SUBLANES = 8
VMEM_LIMIT = 56 * 1024 * 1024

KIW_W = LANES
OFF_QA = 0
OFF_KA = OFF_QA + A_WIDTH
OFF_VA = OFF_KA + A_WIDTH
OFF_QI = OFF_VA + A_WIDTH
OFF_KIW = OFF_QI + IDX_HEADS * IDX_DIM
OFF_US = OFF_KIW + KIW_W
OFF_QC = OFF_US + S5_WIDTH
OFF_KC = OFF_QC + C_WIDTH
OFF_VC = OFF_KC + C_WIDTH
IN_PACKED = OFF_VC + C_WIDTH
IN_SRC_SPLIT = 3 * A_WIDTH + IDX_HEADS * IDX_DIM + IDX_DIM + IDX_HEADS

LOG2E = math.log2(math.e)
INT_MIN = -2 ** 31
INT_MAX = 2 ** 31 - 1


def _cparams(sem):
    return pltpu.CompilerParams(dimension_semantics=sem, vmem_limit_bytes=VMEM_LIMIT)


def _resident(shape, index_map):
    return pl.BlockSpec(shape, index_map, pipeline_mode=pl.Buffered(1))


def _rms(x, g):
    inv = lax.rsqrt(jnp.mean(x * x, axis=-1, keepdims=True) + EPS)
    return x * inv * g


def _dot(a, b):
    return jnp.dot(a, b, preferred_element_type=F32)


def _dot_nt(a, b):
    return lax.dot_general(a, b, (((1,), (1,)), ((), ())), preferred_element_type=F32)


def _inproj_body(x_ref, g_ref, w_ref, c128_ref, s128_ref, c64_ref, sa64_ref, sb64_ref,
                 c64k_ref, sa64k_ref, sb64k_ref,
                 qa_ref, ka_ref, va_ref, qi_ref, kiw_ref, us_ref, qc_ref, kc_ref, vc_ref):
    hn = _rms(x_ref[...], g_ref[...]).astype(BF16)
    c128, s128 = c128_ref[...], s128_ref[...]
    c64, sa64, sb64 = c64_ref[...], sa64_ref[...], sb64_ref[...]

    def rope128(t):
        return t * c128 + pltpu.roll(t, 64, 1) * s128

    def rope64(t):
        return t * c64 + pltpu.roll(t, 96, 1) * sa64 + pltpu.roll(t, 32, 1) * sb64

    def rope64_kiw(t):
        return t * c64k_ref[...] + pltpu.roll(t, 96, 1) * sa64k_ref[...] + pltpu.roll(t, 32, 1) * sb64k_ref[...]

    def emit(off, width, out_ref, fn, scale=None):
        step = 512 if width >= 512 else width
        for c0 in range(0, width, step):
            t = _dot(hn, w_ref[:, off + c0:off + c0 + step])
            for s0 in range(0, step, LANES):
                v = t[:, s0:s0 + LANES]
                if fn is not None:
                    v = fn(v)
                if scale is not None:
                    v = v * scale
                out_ref[:, c0 + s0:c0 + s0 + LANES] = v.astype(out_ref.dtype)

    emit(OFF_QA, A_WIDTH, qa_ref, rope128, A_HEAD_DIM ** -0.5 * LOG2E)
    emit(OFF_KA, A_WIDTH, ka_ref, rope128)
    emit(OFF_VA, A_WIDTH, va_ref, None)
    emit(OFF_QI, IDX_HEADS * IDX_DIM, qi_ref, rope64)
    emit(OFF_KIW, KIW_W, kiw_ref, rope64_kiw)
    emit(OFF_US, S5_WIDTH, us_ref, None)
    emit(OFF_QC, C_WIDTH, qc_ref, rope64, C_QK_DIM ** -0.5 * LOG2E)
    emit(OFF_KC, C_WIDTH, kc_ref, rope64)
    emit(OFF_VC, C_WIDTH, vc_ref, None)


def _in_proj(x, g, w_packed, layer, tabs, tm):
    L = x.shape[0]
    row = lambda i: (i, 0)
    tab_spec = pl.BlockSpec((tm, LANES), row)
    outs = [
        (A_WIDTH, BF16), (A_WIDTH, BF16), (A_WIDTH, BF16), (IDX_HEADS * IDX_DIM, BF16),
        (KIW_W, F32), (S5_WIDTH, F32), (C_WIDTH, BF16), (C_WIDTH, BF16), (C_WIDTH, BF16),
    ]
    return pl.pallas_call(
        _inproj_body,
        grid=(L // tm,),
        in_specs=[
            pl.BlockSpec((tm, D_MODEL), row),
            _resident((None, 1, D_MODEL), lambda i: (layer, 0, 0)),
            _resident((None, D_MODEL, IN_PACKED), lambda i: (layer, 0, 0)),
        ] + [tab_spec] * 8 + [
        ],
        out_specs=[pl.BlockSpec((tm, w), row) for w, _ in outs],
        out_shape=[jax.ShapeDtypeStruct((L, w), dt) for w, dt in outs],
        compiler_params=_cparams(("parallel",)),
        name="in_proj",
    )(x, g, w_packed, *tabs)


def _rope_tables(L):
    pos = jnp.arange(L, dtype=F32)[:, None]

    def cs(dim):
        inv = ROPE_THETA ** (-jnp.arange(0, dim, 2, dtype=F32) / dim)
        ang = pos * inv[None, :]
        return jnp.cos(ang), jnp.sin(ang)

    c, s = cs(A_HEAD_DIM)
    c128 = jnp.concatenate([c, c], axis=1)
    s128 = jnp.concatenate([-s, s], axis=1)
    c, s = cs(IDX_DIM)
    z = jnp.zeros_like(s)
    c64 = jnp.tile(jnp.concatenate([c, c], axis=1), (1, 2))
    sa64 = jnp.tile(jnp.concatenate([-s, z], axis=1), (1, 2))
    sb64 = jnp.tile(jnp.concatenate([z, s], axis=1), (1, 2))
    lane = jnp.arange(LANES)[None, :]
    c64k = jnp.where(lane < IDX_DIM, c64, 1.0)
    sa64k = jnp.where(lane < IDX_DIM, sa64, 0.0)
    sb64k = jnp.where(lane < IDX_DIM, sb64, 0.0)
    return (c128, s128, c64, sa64, sb64), (c64k, sa64k, sb64k)


def _sortable(v):
    k = lax.bitcast_convert_type(v, jnp.int32)
    return k ^ ((k >> 31) & jnp.int32(INT_MAX))


COUNT_ROWS = 32


def _dsa_index_body(qit_ref, wt_ref, ke_ref, ko_ref, bias_ref, skey_ref,
                    *, tq, top_k, n_tiles, idx_bits):
    i = pl.program_id(0)
    n_live = i + 1
    q_pos = i * tq + lax.broadcasted_iota(jnp.int32, (tq, tq), 1)
    k_off = lax.broadcasted_iota(jnp.int32, (tq, tq), 0)
    q_pos_c = i * tq + lax.broadcasted_iota(jnp.int32, (COUNT_ROWS, tq), 1)
    k_off_c = lax.broadcasted_iota(jnp.int32, (COUNT_ROWS, tq), 0)
    wv = wt_ref[...] * ((IDX_HEADS ** -0.5) * (IDX_DIM ** -0.5))

    def score_tile(jt, _):
        rows = pl.ds(pl.multiple_of(jt * tq, tq), tq)
        ke = ke_ref[rows, :]
        ko = ko_ref[rows, :]
        acc = jnp.zeros((tq, tq), F32)
        for hp in range(IDX_HEADS // 2):
            q_pair = qit_ref[hp * LANES:(hp + 1) * LANES, :]
            d0 = _dot(ke, q_pair)
            d1 = _dot(ko, q_pair)
            acc = (acc + wv[2 * hp:2 * hp + 1, :] * jnp.maximum(d0, 0.0)
                   + wv[2 * hp + 1:2 * hp + 2, :] * jnp.maximum(d1, 0.0))
        allowed = ((jt * tq + k_off) >> CHUNK_SHIFT) <= (q_pos >> CHUNK_SHIFT)
        skey_ref[jt] = _sortable(jnp.where(allowed, acc, NEG))
        return 0

    lax.fori_loop(0, n_live, score_tile, 0)

    def count(pred):
        def body(jt, acc):
            for r in range(0, tq, COUNT_ROWS):
                t = skey_ref[jt, r:r + COUNT_ROWS, :]
                acc = acc + jnp.where(pred(t, jt * tq + r + k_off_c), 1.0, 0.0)
            return acc
        acc = lax.fori_loop(0, n_live, body, jnp.zeros((COUNT_ROWS, tq), F32))
        return jnp.sum(acc, axis=0, keepdims=True)

    kf = float(top_k)

    def bit_step(b, carry):
        lo, cnt_lo = carry
        cand = lo + (jnp.int32(1) << (31 - b))
        cnt = count(lambda t, _: t >= cand)
        ok = cnt >= kf
        return jnp.where(ok, cand, lo), jnp.where(ok, cnt, cnt_lo)

    lo0 = jnp.full((1, tq), INT_MIN, jnp.int32)
    cnt0 = jnp.full((1, tq), 1.0, F32) * (n_live * tq).astype(F32)
    lo, cnt_lo = lax.fori_loop(0, 32, bit_step, (lo0, cnt0))

    def tie_cut():
        need = kf - count(lambda t, _: t > lo)

        def idx_step(b, below):
            cand = below + (jnp.int32(1) << (idx_bits - 1 - b))
            cnt = count(lambda t, p: (t == lo) & (p <= cand))
            return jnp.where(cnt < need, cand, below)

        below = lax.fori_loop(0, idx_bits, idx_step, jnp.full((1, tq), -1, jnp.int32))
        return jnp.where(cnt_lo > kf, below + 1, INT_MAX)

    cut = lax.cond(jnp.max(cnt_lo) > kf, tie_cut, lambda: jnp.full((1, tq), INT_MAX, jnp.int32))

    def write_tile(jt, _):
        t = skey_ref[jt]
        k_pos = jt * tq + k_off
        allowed = (k_pos >> CHUNK_SHIFT) <= (q_pos >> CHUNK_SHIFT)
        sel = ((t > lo) | ((t == lo) & (k_pos <= cut))) & allowed
        bias_ref[jt] = jnp.where(sel, 0.0, NEG).astype(BF16)
        return 0

    lax.fori_loop(0, n_live, write_tile, 0)

    def blank_tile(jt, _):
        bias_ref[jt] = jnp.full((tq, tq), NEG, BF16)
        return 0

    lax.fori_loop(n_live, n_tiles, blank_tile, 0)


def _dsa_index(qi_t, w_t, ke, ko, tq):
    L = qi_t.shape[1]
    n_tiles = L // tq
    top_k = min(TOPK_MAX, L // 4)
    body = functools.partial(_dsa_index_body, tq=tq, top_k=top_k, n_tiles=n_tiles,
                             idx_bits=max(1, (L - 1).bit_length()))
    return pl.pallas_call(
        body,
        grid=(n_tiles,),
        in_specs=[
            pl.BlockSpec((IDX_HEADS * IDX_DIM, tq), lambda i: (0, i)),
            pl.BlockSpec((IDX_HEADS, tq), lambda i: (0, i)),
            _resident((L, LANES), lambda i: (0, 0)),
            _resident((L, LANES), lambda i: (0, 0)),
        ],
        out_specs=pl.BlockSpec((None, n_tiles, tq, tq), lambda i: (i, 0, 0, 0)),
        out_shape=jax.ShapeDtypeStruct((n_tiles, n_tiles, tq, tq), BF16),
        scratch_shapes=[pltpu.VMEM((n_tiles, tq, tq), jnp.int32)],
        compiler_params=_cparams(("parallel",)),
        name="dsa_index",
    )(qi_t, w_t, ke, ko)


def _flash_update(s, vt_tile, m_ref, l_ref, acc_ref, slot):
    m_prev = m_ref[slot]
    m_new = jnp.maximum(m_prev, jnp.max(s, axis=0, keepdims=True))
    alpha = jnp.exp2(m_prev - m_new)
    p = jnp.exp2(s - m_new)
    l_ref[slot] = alpha * l_ref[slot] + jnp.sum(p, axis=0, keepdims=True)
    acc_ref[slot] = alpha * acc_ref[slot] + _dot(vt_tile, p.astype(BF16))
    m_ref[slot] = m_new


def _flash_init(m_ref, l_ref, acc_ref):
    m_ref[...] = jnp.full(m_ref.shape, NEG, F32)
    l_ref[...] = jnp.zeros(l_ref.shape, F32)
    acc_ref[...] = jnp.zeros(acc_ref.shape, F32)


def _dsa_attn_body(qt_ref, k_ref, vt_ref, bias_ref, o_ref, m_ref, l_ref, acc_ref, *, tq, tk):
    i = pl.program_id(0)
    _flash_init(m_ref, l_ref, acc_ref)

    def tile(jt, _):
        keys = pl.ds(pl.multiple_of(jt * tk, tk), tk)
        b = bias_ref[keys, :].astype(F32)
        for h in range(A_HEADS):
            hs = slice(h * A_HEAD_DIM, (h + 1) * A_HEAD_DIM)
            s = _dot(k_ref[keys, hs], qt_ref[hs, :]) + b
            _flash_update(s, vt_ref[hs, keys], m_ref, l_ref, acc_ref, h)
        return 0

    lax.fori_loop(0, ((i + 1) * tq + tk - 1) // tk, tile, 0)
    for h in range(A_HEADS):
        hs = slice(h * A_HEAD_DIM, (h + 1) * A_HEAD_DIM)
        o_ref[:, hs] = jnp.transpose(acc_ref[h] / l_ref[h]).astype(o_ref.dtype)


def _dsa_attn(q_t, k, v_t, bias, tq, tk):
    L = k.shape[0]
    n_tiles = L // tq
    return pl.pallas_call(
        functools.partial(_dsa_attn_body, tq=tq, tk=tk),
        grid=(n_tiles,),
        in_specs=[
            pl.BlockSpec((A_WIDTH, tq), lambda i: (0, i)),
            _resident((L, A_WIDTH), lambda i: (0, 0)),
            _resident((A_WIDTH, L), lambda i: (0, 0)),
            pl.BlockSpec((None, L, tq), lambda i: (i, 0, 0)),
        ],
        out_specs=pl.BlockSpec((tq, A_WIDTH), lambda i: (i, 0)),
        out_shape=jax.ShapeDtypeStruct((L, A_WIDTH), BF16),
        scratch_shapes=[
            pltpu.VMEM((A_HEADS, 1, tq), F32),
            pltpu.VMEM((A_HEADS, 1, tq), F32),
            pltpu.VMEM((A_HEADS, A_HEAD_DIM, tq), F32),
        ],
        compiler_params=_cparams(("parallel",)),
        name="dsa_attn",
    )(q_t, k, v_t, bias)


def _diff_attn_body(lq1_ref, lk1_ref, lq2_ref, lk2_ref, g_ref, qt_ref, k_ref, vt_ref, o_ref,
                    m_ref, l_ref, acc_ref, *, tq, lam_init):
    i = pl.program_id(1)
    _flash_init(m_ref, l_ref, acc_ref)
    qt = qt_ref[...]
    row = lax.broadcasted_iota(jnp.int32, qt.shape, 0)
    zero = jnp.zeros_like(qt)
    qts = (jnp.where(row < C_QK_DIM, qt, zero), jnp.where(row >= C_QK_DIM, qt, zero))

    def step(jt, mask):
        start = pl.multiple_of(jt * tq, tq)
        kt = k_ref[pl.ds(start, tq), :]
        vt = vt_ref[:, pl.ds(start, tq)]
        for c in range(2):
            s = _dot(kt, qts[c])
            if mask is not None:
                s = jnp.where(mask, s, NEG)
            _flash_update(s, vt, m_ref, l_ref, acc_ref, c)

    def full_tile(jt, _):
        step(jt, None)
        return 0

    lax.fori_loop(0, i, full_tile, 0)
    kk = lax.broadcasted_iota(jnp.int32, (tq, tq), 0)
    qq = lax.broadcasted_iota(jnp.int32, (tq, tq), 1)
    step(i, (kk >> CHUNK_SHIFT) <= (qq >> CHUNK_SHIFT))

    lam = (jnp.exp(jnp.sum(lq1_ref[...] * lk1_ref[...], axis=1, keepdims=True))
           - jnp.exp(jnp.sum(lq2_ref[...] * lk2_ref[...], axis=1, keepdims=True)) + lam_init)
    o = acc_ref[0] / l_ref[0] - lam * (acc_ref[1] / l_ref[1])
    inv = lax.rsqrt(jnp.mean(o * o, axis=0, keepdims=True) + EPS)
    o_ref[...] = (jnp.transpose(o * inv) * g_ref[...] * (1.0 - lam_init)).astype(o_ref.dtype)


def _diff_attn(q_t, k, v_t, lam_vecs, sub_gain, layer, lam_init, tq):
    L = k.shape[0]
    vec = lambda w: _resident((None, 1, w), lambda h, i: (layer, 0, 0))
    return pl.pallas_call(
        functools.partial(_diff_attn_body, tq=tq, lam_init=lam_init),
        grid=(C_HEADS, L // tq),
        in_specs=[
            vec(C_QK_DIM), vec(C_QK_DIM), vec(C_QK_DIM), vec(C_QK_DIM), vec(C_V_DIM),
            pl.BlockSpec((C_V_DIM, tq), lambda h, i: (h, i)),
            pl.BlockSpec((L, C_V_DIM), lambda h, i: (0, h)),
            pl.BlockSpec((C_V_DIM, L), lambda h, i: (h, 0)),
        ],
        out_specs=pl.BlockSpec((tq, C_V_DIM), lambda h, i: (i, h)),
        out_shape=jax.ShapeDtypeStruct((L, C_WIDTH), BF16),
        scratch_shapes=[
            pltpu.VMEM((2, 1, tq), F32),
            pltpu.VMEM((2, 1, tq), F32),
            pltpu.VMEM((2, C_V_DIM, tq), F32),
        ],
        compiler_params=_cparams(("parallel", "parallel")),
        name="diff_attn",
    )(*lam_vecs, sub_gain, q_t, k, v_t)


def _s5_prep_body(are_ref, aim_ref, ldt_ref, bre_ref, bim_ref, bb_ref, lvl_ref, pw_ref):
    a_re, a_im = are_ref[...], aim_ref[...]
    dt = jnp.exp(ldt_ref[...])
    mag = jnp.exp(a_re * dt)
    lb_re = mag * jnp.cos(a_im * dt)
    lb_im = mag * jnp.sin(a_im * dt)
    den = a_re * a_re + a_im * a_im
    f_re = ((lb_re - 1.0) * a_re + lb_im * a_im) / den
    f_im = (lb_im * a_re - (lb_re - 1.0) * a_im) / den
    b_re, b_im = bre_ref[...], bim_ref[...]
    bb_ref[:, :S5_NSTATE] = (f_re * b_re - f_im * b_im).astype(BF16)
    bb_ref[:, S5_NSTATE:] = (f_re * b_im + f_im * b_re).astype(BF16)

    pows = [(lb_re, lb_im)]
    for _ in range(SUBLANES - 1):
        pr, pi = pows[-1]
        pows.append((pr * lb_re - pi * lb_im, pr * lb_im + pi * lb_re))
    row = lax.broadcasted_iota(jnp.int32, (SUBLANES, S5_NSTATE), 0)
    zero = jnp.zeros((SUBLANES, S5_NSTATE), F32)
    for lvl in range(3):
        sh = 1 << lvl
        pr, pi = pows[sh - 1]
        lvl_ref[lvl, 0] = jnp.where(row >= sh, pr, zero)
        lvl_ref[lvl, 1] = jnp.where(row >= sh, pi, zero)
    cr, ci = zero, zero
    for r in range(SUBLANES):
        cr = jnp.where(row == r, pows[r][0], cr)
        ci = jnp.where(row == r, pows[r][1], ci)
    pw_ref[0] = cr
    pw_ref[1] = ci


def _s5_prep(a_re, a_im, log_dt, b_re_bd, b_im_bd):
    depth = a_re.shape[0]
    vec = pl.BlockSpec((None, 1, S5_NSTATE), lambda l: (l, 0, 0))
    mat = pl.BlockSpec((None, S5_WIDTH, S5_NSTATE), lambda l: (l, 0, 0))
    return pl.pallas_call(
        _s5_prep_body,
        grid=(depth,),
        in_specs=[vec, vec, vec, mat, mat],
        out_specs=[
            pl.BlockSpec((None, S5_WIDTH, 2 * S5_NSTATE), lambda l: (l, 0, 0)),
            pl.BlockSpec((None, 3, 2, SUBLANES, S5_NSTATE), lambda l: (l, 0, 0, 0, 0)),
            pl.BlockSpec((None, 2, SUBLANES, S5_NSTATE), lambda l: (l, 0, 0, 0)),
        ],
        out_shape=[
            jax.ShapeDtypeStruct((depth, S5_WIDTH, 2 * S5_NSTATE), BF16),
            jax.ShapeDtypeStruct((depth, 3, 2, SUBLANES, S5_NSTATE), F32),
            jax.ShapeDtypeStruct((depth, 2, SUBLANES, S5_NSTATE), F32),
        ],
        compiler_params=_cparams(("parallel",)),
        name="s5_prep",
    )(a_re, a_im, log_dt, b_re_bd, b_im_bd)


S5_LANE_CHUNK = 512


def _s5_body(u_ref, bb_ref, lvl_ref, pw_ref, cre_ref, cim_ref, d_ref, wg_ref, bg_ref, o_ref,
             x_ref, carry_ref, *, tl):
    @pl.when(pl.program_id(0) == 0)
    def _():
        carry_ref[...] = jnp.zeros(carry_ref.shape, F32)

    u = u_ref[...]
    x_ref[...] = _dot(u.astype(BF16), bb_ref[...])

    for c0 in range(0, S5_NSTATE, S5_LANE_CHUNK):
        re_sl = slice(c0, c0 + S5_LANE_CHUNK)
        im_sl = slice(S5_NSTATE + c0, S5_NSTATE + c0 + S5_LANE_CHUNK)

        def block(t, carry, re_sl=re_sl, im_sl=im_sl):
            c_re, c_im = carry
            rows = pl.ds(pl.multiple_of(t * SUBLANES, SUBLANES), SUBLANES)
            re = x_ref[rows, re_sl]
            im = x_ref[rows, im_sl]
            for lvl in range(3):
                sh = 1 << lvl
                s_re = pltpu.roll(re, sh, 0)
                s_im = pltpu.roll(im, sh, 0)
                a_re = lvl_ref[lvl, 0, :, re_sl]
                a_im = lvl_ref[lvl, 1, :, re_sl]
                re, im = re + (a_re * s_re - a_im * s_im), im + (a_re * s_im + a_im * s_re)
            p_re = pw_ref[0, :, re_sl]
            p_im = pw_ref[1, :, re_sl]
            re, im = re + (p_re * c_re - p_im * c_im), im + (p_re * c_im + p_im * c_re)
            x_ref[rows, re_sl] = re
            x_ref[rows, im_sl] = im
            last = SUBLANES - 1
            return (jnp.broadcast_to(re[last:, :], re.shape), jnp.broadcast_to(im[last:, :], im.shape))

        carry = lax.fori_loop(0, tl // SUBLANES, block,
                              (carry_ref[0, :, re_sl], carry_ref[1, :, re_sl]))
        carry_ref[0, :, re_sl] = carry[0]
        carry_ref[1, :, re_sl] = carry[1]

    y = (_dot(x_ref[:, :S5_NSTATE].astype(BF16), cre_ref[...])
         - _dot(x_ref[:, S5_NSTATE:].astype(BF16), cim_ref[...])
         + d_ref[...] * u)
    y = 0.5 * y * (1.0 + jnp.tanh(math.sqrt(2.0 / math.pi) * (y + 0.044715 * (y * y * y))))
    z = _dot(y.astype(BF16), wg_ref[...]) + bg_ref[...]
    o_ref[...] = (y * (1.0 / (1.0 + jnp.exp(-z)))).astype(o_ref.dtype)


def _s5_mix(u, bb, lvl, pw, c_re_bd, c_im_bd, d_row, w_glu, b_glu, layer, tl):
    L = u.shape[0]
    lay = lambda *rest: (lambda t: (layer,) + rest)
    return pl.pallas_call(
        functools.partial(_s5_body, tl=tl),
        grid=(L // tl,),
        in_specs=[
            pl.BlockSpec((tl, S5_WIDTH), lambda t: (t, 0)),
            _resident((None, S5_WIDTH, 2 * S5_NSTATE), lay(0, 0)),
            _resident((None, 3, 2, SUBLANES, S5_NSTATE), lay(0, 0, 0, 0)),
            _resident((None, 2, SUBLANES, S5_NSTATE), lay(0, 0, 0)),
            _resident((None, S5_NSTATE, S5_WIDTH), lay(0, 0)),
            _resident((None, S5_NSTATE, S5_WIDTH), lay(0, 0)),
            _resident((None, 1, S5_WIDTH), lay(0, 0)),
            _resident((None, S5_WIDTH, S5_WIDTH), lay(0, 0)),
            _resident((None, 1, S5_WIDTH), lay(0, 0)),
        ],
        out_specs=pl.BlockSpec((tl, S5_WIDTH), lambda t: (t, 0)),
        out_shape=jax.ShapeDtypeStruct((L, S5_WIDTH), BF16),
        scratch_shapes=[
            pltpu.VMEM((tl, 2 * S5_NSTATE), F32),
            pltpu.VMEM((2, SUBLANES, S5_NSTATE), F32),
        ],
        compiler_params=_cparams(("arbitrary",)),
        name="s5_mix",
    )(u, bb, lvl, pw, c_re_bd, c_im_bd, d_row, w_glu, b_glu)


def _outproj_body(x_ref, oa_ref, os_ref, oc_ref, w_ref, o_ref):
    a0, a1, a2 = A_WIDTH, A_WIDTH + S5_WIDTH, A_WIDTH + S5_WIDTH + C_WIDTH
    o_ref[...] = (x_ref[...] + _dot(oa_ref[...], w_ref[:a0, :]) + _dot(os_ref[...], w_ref[a0:a1, :])
                  + _dot(oc_ref[...], w_ref[a1:a2, :]))


def _out_proj(x, o_a, o_s, o_c, w_out, layer, tm):
    L = x.shape[0]
    row = lambda i: (i, 0)
    return pl.pallas_call(
        _outproj_body,
        grid=(L // tm,),
        in_specs=[
            pl.BlockSpec((tm, D_MODEL), row),
            pl.BlockSpec((tm, A_WIDTH), row),
            pl.BlockSpec((tm, S5_WIDTH), row),
            pl.BlockSpec((tm, C_WIDTH), row),
            _resident((None, D_MODEL, D_MODEL), lambda i: (layer, 0, 0)),
        ],
        out_specs=pl.BlockSpec((tm, D_MODEL), row),
        out_shape=jax.ShapeDtypeStruct((L, D_MODEL), F32),
        compiler_params=_cparams(("parallel",)),
        name="out_proj",
    )(x, o_a, o_s, o_c, w_out)


def _norm_matmul_body(x_ref, g_ref, w_ref, o_ref):
    o_ref[...] = _dot(_rms(x_ref[...], g_ref[...]).astype(BF16), w_ref[...]).astype(o_ref.dtype)


def _norm_matmul(x, g, w, layer, tn):
    M, K = x.shape
    N = w.shape[-1]
    return pl.pallas_call(
        _norm_matmul_body,
        grid=(N // tn,),
        in_specs=[
            _resident((M, K), lambda j: (0, 0)),
            _resident((None, 1, K), lambda j: (layer, 0, 0)),
            pl.BlockSpec((None, K, tn), lambda j: (layer, 0, j)),
        ],
        out_specs=pl.BlockSpec((M, tn), lambda j: (0, j)),
        out_shape=jax.ShapeDtypeStruct((M, N), BF16),
        compiler_params=_cparams(("parallel",)),
        name="mem_proj",
    )(x, g, w)


def _xattn_body(x_ref, g_ref, wq_ref, k_ref, v_ref, wo_ref, o_ref):
    x = x_ref[...]
    q = _dot(_rms(x, g_ref[...]).astype(BF16), wq_ref[...]).astype(BF16)
    scale = X_HEAD_DIM ** -0.5
    out = x
    for h in range(X_HEADS):
        hs = slice(h * X_HEAD_DIM, (h + 1) * X_HEAD_DIM)
        s = _dot_nt(q[:, hs], k_ref[:, hs]) * scale
        p = jnp.exp(s - jnp.max(s, axis=1, keepdims=True))
        p = p / jnp.sum(p, axis=1, keepdims=True)
        o_h = _dot(p.astype(BF16), v_ref[:, hs]).astype(BF16)
        out = out + _dot(o_h, wo_ref[hs, :])
    o_ref[...] = out


def _cross_attn(x, g, wq, k_mem, v_mem, wo, layer, tm):
    L = x.shape[0]
    M = k_mem.shape[0]
    row = lambda i: (i, 0)
    return pl.pallas_call(
        _xattn_body,
        grid=(L // tm,),
        in_specs=[
            pl.BlockSpec((tm, D_MODEL), row),
            _resident((None, 1, D_MODEL), lambda i: (layer, 0, 0)),
            _resident((None, D_MODEL, D_MODEL), lambda i: (layer, 0, 0)),
            _resident((M, D_MODEL), lambda i: (0, 0)),
            _resident((M, D_MODEL), lambda i: (0, 0)),
            _resident((None, D_MODEL, D_MODEL), lambda i: (layer, 0, 0)),
        ],
        out_specs=pl.BlockSpec((tm, D_MODEL), row),
        out_shape=jax.ShapeDtypeStruct((L, D_MODEL), F32),
        compiler_params=_cparams(("parallel",)),
        name="cross_attn",
    )(x, g, wq, k_mem, v_mem, wo)


def _mlp_body(x_ref, g_ref, w1_ref, w2_ref, o_ref, hn_ref):
    @pl.when(pl.program_id(1) == 0)
    def _():
        x = x_ref[...]
        hn_ref[...] = _rms(x, g_ref[...]).astype(BF16)
        o_ref[...] = x

    a = jnp.maximum(_dot(hn_ref[...], w1_ref[...]), 0.0)
    o_ref[...] += _dot((a * a).astype(BF16), w2_ref[...])


def _mlp(x, g, w1, w2, layer, tm, tf):
    L = x.shape[0]
    return pl.pallas_call(
        _mlp_body,
        grid=(L // tm, D_FF // tf),
        in_specs=[
            pl.BlockSpec((tm, D_MODEL), lambda i, f: (i, 0)),
            _resident((None, 1, D_MODEL), lambda i, f: (layer, 0, 0)),
            pl.BlockSpec((None, D_MODEL, tf), lambda i, f: (layer, 0, f)),
            pl.BlockSpec((None, tf, D_MODEL), lambda i, f: (layer, f, 0)),
        ],
        out_specs=pl.BlockSpec((tm, D_MODEL), lambda i, f: (i, 0)),
        out_shape=jax.ShapeDtypeStruct((L, D_MODEL), F32),
        scratch_shapes=[pltpu.VMEM((tm, D_MODEL), BF16)],
        compiler_params=_cparams(("parallel", "arbitrary")),
        name="mlp",
    )(x, g, w1, w2)


def _final_norm_body(x_ref, g_ref, o_ref):
    o_ref[...] = _rms(x_ref[...], g_ref[...])


def _final_norm(x, g, tm):
    L = x.shape[0]
    return pl.pallas_call(
        _final_norm_body,
        grid=(L // tm,),
        in_specs=[pl.BlockSpec((tm, D_MODEL), lambda i: (i, 0)),
                  _resident((1, D_MODEL), lambda i: (0, 0))],
        out_specs=pl.BlockSpec((tm, D_MODEL), lambda i: (i, 0)),
        out_shape=jax.ShapeDtypeStruct((L, D_MODEL), F32),
        compiler_params=_cparams(("parallel",)),
        name="final_norm",
    )(x, g)


def _block_diag_in(b):
    eye = jnp.eye(S5_GROUPS, dtype=b.dtype)
    t = jnp.transpose(b, (0, 1, 3, 2))
    bd = t[:, :, :, None, :] * eye[None, :, None, :, None]
    return bd.reshape(b.shape[0], S5_WIDTH, S5_NSTATE)


def _block_diag_out(c):
    eye = jnp.eye(S5_GROUPS, dtype=c.dtype)
    t = jnp.transpose(c, (0, 1, 3, 2))
    bd = t[:, :, :, None, :] * eye[None, :, None, :, None]
    return bd.reshape(c.shape[0], S5_NSTATE, S5_WIDTH)


def _pick_tile(n, want):
    t = min(n, want)
    assert n % t == 0, (n, t)
    return t


def kernel(x, mem, norm_mix, w_in, s5_a_re, s5_a_im, s5_log_dt, s5_b_re, s5_b_im, s5_c_re, s5_c_im, s5_d, s5_w_glu, s5_b_glu, diff_lam_q1, diff_lam_k1, diff_lam_q2, diff_lam_k2, diff_subln, w_out, norm_xattn, norm_mem, xattn_q, xattn_k, xattn_v, xattn_o, norm_mlp, w_ff1, w_ff2, norm_final):
    B, L, _ = x.shape
    assert B == 1
    depth = w_in.shape[0]
    tq = _pick_tile(L, 256)

    pad = jnp.zeros((depth, D_MODEL, KIW_W - IDX_DIM - IDX_HEADS), w_in.dtype)
    w_in_p = jnp.concatenate([w_in[..., :IN_SRC_SPLIT], pad, w_in[..., IN_SRC_SPLIT:]], axis=-1).astype(BF16)
    w_out_b = w_out.astype(BF16)
    wq_b, wk_b, wv_b, wo_b = (w.astype(BF16) for w in (xattn_q, xattn_k, xattn_v, xattn_o))
    w1_b, w2_b = w_ff1.astype(BF16), w_ff2.astype(BF16)
    wg_b = s5_w_glu.astype(BF16)
    row3 = lambda a: a.reshape(depth, 1, -1)

    tabs, tabs_k = _rope_tables(L)

    rep = lambda a: row3(a.astype(F32))
    a_re, a_im = rep(s5_a_re), rep(s5_a_im)
    log_dt = row3(jnp.broadcast_to(s5_log_dt.astype(F32)[:, :, None], (depth, S5_GROUPS, S5_STATE)))
    bb, lvl, pw = _s5_prep(a_re, a_im, log_dt, _block_diag_in(s5_b_re.astype(F32)),
                           _block_diag_in(s5_b_im.astype(F32)))
    c_re_bd = _block_diag_out(s5_c_re).astype(BF16)
    c_im_bd = _block_diag_out(s5_c_im).astype(BF16)
    d_row = row3(s5_d.astype(F32))
    lam_vecs = tuple(row3(v.astype(F32)) for v in (diff_lam_q1, diff_lam_k1, diff_lam_q2, diff_lam_k2))

    xs = x[0]
    mem2 = mem[0]
    for l in range(depth):
        qa, ka, va, qi, kiw, us, qc, kc, vc = _in_proj(xs, row3(norm_mix), w_in_p, l, tabs + tabs_k, _pick_tile(L, 256))

        k_idx = kiw[:, :IDX_DIM].astype(BF16)
        zeros = jnp.zeros_like(k_idx)
        ke = jnp.concatenate([k_idx, zeros], axis=1)
        ko = jnp.concatenate([zeros, k_idx], axis=1)
        w_t = jnp.transpose(kiw[:, IDX_DIM:IDX_DIM + IDX_HEADS])
        bias = _dsa_index(jnp.transpose(qi), w_t, ke, ko, tq).reshape(L // tq, L, tq)
        o_a = _dsa_attn(jnp.transpose(qa), ka, jnp.transpose(va), bias, tq, _pick_tile(L, 1024))

        o_s = _s5_mix(us, bb, lvl, pw, c_re_bd, c_im_bd, d_row, wg_b, row3(s5_b_glu.astype(F32)), l,
                      _pick_tile(L, 512))

        lam_init = 0.8 - 0.6 * math.exp(-0.3 * l)
        o_c = _diff_attn(jnp.transpose(qc), kc, jnp.transpose(vc), lam_vecs, row3(diff_subln.astype(F32)), l,
                         lam_init, _pick_tile(L, 1024))

        xs = _out_proj(xs, o_a, o_s, o_c, w_out_b, l, _pick_tile(L, 512))

        k_mem = _norm_matmul(mem2, row3(norm_mem), wk_b, l, 512)
        v_mem = _norm_matmul(mem2, row3(norm_mem), wv_b, l, 512)
        xs = _cross_attn(xs, row3(norm_xattn), wq_b, k_mem, v_mem, wo_b, l, _pick_tile(L, 512))

        xs = _mlp(xs, row3(norm_mlp), w1_b, w2_b, l, _pick_tile(L, 512), 512)

    return _final_norm(xs, norm_final.reshape(1, -1), _pick_tile(L, 512))[None]
```

```python
import functools
import math

import jax
import jax.numpy as jnp
from jax import lax
from jax.experimental import pallas as pl
from jax.experimental.pallas import tpu as pltpu

F32 = jnp.float32
BF16 = jnp.bfloat16

D_MODEL = 2048
CHUNK = 64
CHUNK_SHIFT = CHUNK.bit_length() - 1
assert 1 << CHUNK_SHIFT == CHUNK
ROPE_THETA = 10000.0
EPS = 1e-6
NEG = -1e30

A_HEAD_DIM = 128
A_WIDTH = D_MODEL // 4
A_HEADS = A_WIDTH // A_HEAD_DIM
IDX_HEADS = 8
IDX_DIM = 64
TOPK_MAX = 256

S5_GROUP = 16
S5_WIDTH = D_MODEL // 4
S5_GROUPS = S5_WIDTH // S5_GROUP
S5_STATE = 64
S5_NSTATE = S5_GROUPS * S5_STATE

C_QK_DIM = 64
C_V_DIM = 2 * C_QK_DIM
C_WIDTH = D_MODEL // 2
C_HEADS = C_WIDTH // C_V_DIM

X_HEADS = 4
X_HEAD_DIM = D_MODEL // X_HEADS
D_FF = 4 * D_MODEL

LANES = 128
SUBLANES = 8
VMEM_LIMIT = 56 * 1024 * 1024

KIW_W = LANES
OFF_QA = 0
OFF_KA = OFF_QA + A_WIDTH
OFF_VA = OFF_KA + A_WIDTH
OFF_QI = OFF_VA + A_WIDTH
OFF_KIW = OFF_QI + IDX_HEADS * IDX_DIM
OFF_US = OFF_KIW + KIW_W
OFF_QC = OFF_US + S5_WIDTH
OFF_KC = OFF_QC + C_WIDTH
OFF_VC = OFF_KC + C_WIDTH
IN_PACKED = OFF_VC + C_WIDTH
IN_SRC_SPLIT = 3 * A_WIDTH + IDX_HEADS * IDX_DIM + IDX_DIM + IDX_HEADS

LOG2E = math.log2(math.e)
INT_MIN = -2 ** 31
INT_MAX = 2 ** 31 - 1


def _cparams(sem):
    return pltpu.CompilerParams(dimension_semantics=sem, vmem_limit_bytes=VMEM_LIMIT)


def _resident(shape, index_map):
    return pl.BlockSpec(shape, index_map, pipeline_mode=pl.Buffered(1))


def _rms(x, g):
    inv = lax.rsqrt(jnp.mean(x * x, axis=-1, keepdims=True) + EPS)
    return x * inv * g


def _dot(a, b):
    return jnp.dot(a, b, preferred_element_type=F32)


def _dot_nt(a, b):
    return lax.dot_general(a, b, (((1,), (1,)), ((), ())), preferred_element_type=F32)


def _inproj_body(x_ref, g_ref, w_ref, c128_ref, s128_ref, c64_ref, sa64_ref, sb64_ref,
                 c64k_ref, sa64k_ref, sb64k_ref,
                 qa_ref, ka_ref, va_ref, qi_ref, kiw_ref, us_ref, qc_ref, kc_ref, vc_ref):
    hn = _rms(x_ref[...], g_ref[...]).astype(BF16)
    c128, s128 = c128_ref[...], s128_ref[...]
    c64, sa64, sb64 = c64_ref[...], sa64_ref[...], sb64_ref[...]

    def rope128(t):
        return t * c128 + pltpu.roll(t, 64, 1) * s128

    def rope64(t):
        return t * c64 + pltpu.roll(t, 96, 1) * sa64 + pltpu.roll(t, 32, 1) * sb64

    def rope64_kiw(t):
        return t * c64k_ref[...] + pltpu.roll(t, 96, 1) * sa64k_ref[...] + pltpu.roll(t, 32, 1) * sb64k_ref[...]

    def emit(off, width, out_ref, fn, scale=None):
        step = 512 if width >= 512 else width
        for c0 in range(0, width, step):
            t = _dot(hn, w_ref[:, off + c0:off + c0 + step])
            for s0 in range(0, step, LANES):
                v = t[:, s0:s0 + LANES]
                if fn is not None:
                    v = fn(v)
                if scale is not None:
                    v = v * scale
                out_ref[:, c0 + s0:c0 + s0 + LANES] = v.astype(out_ref.dtype)

    emit(OFF_QA, A_WIDTH, qa_ref, rope128, A_HEAD_DIM ** -0.5 * LOG2E)
    emit(OFF_KA, A_WIDTH, ka_ref, rope128)
    emit(OFF_VA, A_WIDTH, va_ref, None)
    emit(OFF_QI, IDX_HEADS * IDX_DIM, qi_ref, rope64)
    emit(OFF_KIW, KIW_W, kiw_ref, rope64_kiw)
    emit(OFF_US, S5_WIDTH, us_ref, None)
    emit(OFF_QC, C_WIDTH, qc_ref, rope64, C_QK_DIM ** -0.5 * LOG2E)
    emit(OFF_KC, C_WIDTH, kc_ref, rope64)
    emit(OFF_VC, C_WIDTH, vc_ref, None)


def _in_proj(x, g, w_packed, layer, tabs, tm):
    L = x.shape[0]
    row = lambda i: (i, 0)
    tab_spec = pl.BlockSpec((tm, LANES), row)
    outs = [
        (A_WIDTH, BF16), (A_WIDTH, BF16), (A_WIDTH, BF16), (IDX_HEADS * IDX_DIM, BF16),
        (KIW_W, F32), (S5_WIDTH, F32), (C_WIDTH, BF16), (C_WIDTH, BF16), (C_WIDTH, BF16),
    ]
    return pl.pallas_call(
        _inproj_body,
        grid=(L // tm,),
        in_specs=[
            pl.BlockSpec((tm, D_MODEL), row),
            _resident((None, 1, D_MODEL), lambda i: (layer, 0, 0)),
            _resident((None, D_MODEL, IN_PACKED), lambda i: (layer, 0, 0)),
        ] + [tab_spec] * 8 + [
        ],
        out_specs=[pl.BlockSpec((tm, w), row) for w, _ in outs],
        out_shape=[jax.ShapeDtypeStruct((L, w), dt) for w, dt in outs],
        compiler_params=_cparams(("parallel",)),
        name="in_proj",
    )(x, g, w_packed, *tabs)


def _rope_tables(L):
    pos = jnp.arange(L, dtype=F32)[:, None]

    def cs(dim):
        inv = ROPE_THETA ** (-jnp.arange(0, dim, 2, dtype=F32) / dim)
        ang = pos * inv[None, :]
        return jnp.cos(ang), jnp.sin(ang)

    c, s = cs(A_HEAD_DIM)
    c128 = jnp.concatenate([c, c], axis=1)
    s128 = jnp.concatenate([-s, s], axis=1)
    c, s = cs(IDX_DIM)
    z = jnp.zeros_like(s)
    c64 = jnp.tile(jnp.concatenate([c, c], axis=1), (1, 2))
    sa64 = jnp.tile(jnp.concatenate([-s, z], axis=1), (1, 2))
    sb64 = jnp.tile(jnp.concatenate([z, s], axis=1), (1, 2))
    lane = jnp.arange(LANES)[None, :]
    c64k = jnp.where(lane < IDX_DIM, c64, 1.0)
    sa64k = jnp.where(lane < IDX_DIM, sa64, 0.0)
    sb64k = jnp.where(lane < IDX_DIM, sb64, 0.0)
    return (c128, s128, c64, sa64, sb64), (c64k, sa64k, sb64k)


def _sortable(v):
    k = lax.bitcast_convert_type(v, jnp.int32)
    return k ^ ((k >> 31) & jnp.int32(INT_MAX))


COUNT_ROWS = 32


def _dsa_index_body(qit_ref, wt_ref, ke_ref, ko_ref, bias_ref, skey_ref, gmax_ref, *, tq, top_k, n_tiles):
    i = pl.program_id(0)
    n_live = i + 1
    q_pos = i * tq + lax.broadcasted_iota(jnp.int32, (tq, tq), 1)
    k_off = lax.broadcasted_iota(jnp.int32, (tq, tq), 0)
    wv = wt_ref[...] * ((IDX_HEADS ** -0.5) * (IDX_DIM ** -0.5))
    gmax_ref[...] = jnp.full((tq, tq), INT_MIN, jnp.int32)

    def score_tile(jt, _):
        rows = pl.ds(pl.multiple_of(jt * tq, tq), tq)
        ke = ke_ref[rows, :]
        ko = ko_ref[rows, :]
        acc = jnp.zeros((tq, tq), F32)
        for hp in range(IDX_HEADS // 2):
            q_pair = qit_ref[hp * LANES:(hp + 1) * LANES, :]
            d0 = _dot(ke, q_pair)
            d1 = _dot(ko, q_pair)
            acc = (acc + wv[2 * hp:2 * hp + 1, :] * jnp.maximum(d0, 0.0)
                   + wv[2 * hp + 1:2 * hp + 2, :] * jnp.maximum(d1, 0.0))
        allowed = ((jt * tq + k_off) >> CHUNK_SHIFT) <= (q_pos >> CHUNK_SHIFT)
        key = _sortable(jnp.where(allowed, acc, NEG))
        skey_ref[jt] = key
        gmax_ref[...] = jnp.maximum(gmax_ref[...], key)
        return 0

    lax.fori_loop(0, n_live, score_tile, 0)

    def count(pred):
        def body(jt, acc):
            for r in range(0, tq, COUNT_ROWS):
                acc = acc + jnp.where(pred(skey_ref[jt, r:r + COUNT_ROWS, :]), 1.0, 0.0)
            return acc
        acc = lax.fori_loop(0, n_live, body, jnp.zeros((COUNT_ROWS, tq), F32))
        return jnp.sum(acc, axis=0, keepdims=True)

    kf = float(top_k)

    gmax = gmax_ref[...]
    lo0 = jnp.min(gmax, axis=0, keepdims=True)
    hi0 = jnp.max(gmax, axis=0, keepdims=True)

    def narrowing(carry):
        lo, hi = carry
        return jnp.max(jnp.where(hi != lo, 1.0, 0.0)) > 0.0

    def bisect(carry):
        lo, hi = carry
        mid = (lo >> 1) + (hi >> 1) + (((lo & 1) + (hi & 1) + 1) >> 1)
        ok = count(lambda t: t >= mid) >= kf
        return jnp.where(ok, mid, lo), jnp.where(ok, hi, mid - 1)

    lo, _ = lax.while_loop(narrowing, bisect, (lo0, hi0))

    need = kf - count(lambda t: t > lo)
    tri = jnp.where(k_off >= lax.broadcasted_iota(jnp.int32, (tq, tq), 1), 1.0, 0.0).astype(BF16)

    def write_tile(jt, seen):
        t = skey_ref[jt]
        k_pos = jt * tq + k_off
        allowed = (k_pos >> CHUNK_SHIFT) <= (q_pos >> CHUNK_SHIFT)
        tied = t == lo
        rank = seen + _dot(tri, jnp.where(tied, 1.0, 0.0).astype(BF16))
        sel = ((t > lo) | (tied & (rank <= need))) & allowed
        bias_ref[jt] = jnp.where(sel, 0.0, NEG).astype(BF16)
        return rank[tq - 1:tq, :]

    lax.fori_loop(0, n_live, write_tile, jnp.zeros((1, tq), F32))

    def blank_tile(jt, _):
        bias_ref[jt] = jnp.full((tq, tq), NEG, BF16)
        return 0

    lax.fori_loop(n_live, n_tiles, blank_tile, 0)


def _dsa_index(qi_t, w_t, ke, ko, tq):
    L = qi_t.shape[1]
    n_tiles = L // tq
    top_k = min(TOPK_MAX, L // 4)
    assert top_k <= tq
    body = functools.partial(_dsa_index_body, tq=tq, top_k=top_k, n_tiles=n_tiles)
    return pl.pallas_call(
        body,
        grid=(n_tiles,),
        in_specs=[
            pl.BlockSpec((IDX_HEADS * IDX_DIM, tq), lambda i: (0, i)),
            pl.BlockSpec((IDX_HEADS, tq), lambda i: (0, i)),
            _resident((L, LANES), lambda i: (0, 0)),
            _resident((L, LANES), lambda i: (0, 0)),
        ],
        out_specs=pl.BlockSpec((None, n_tiles, tq, tq), lambda i: (i, 0, 0, 0)),
        out_shape=jax.ShapeDtypeStruct((n_tiles, n_tiles, tq, tq), BF16),
        scratch_shapes=[pltpu.VMEM((n_tiles, tq, tq), jnp.int32), pltpu.VMEM((tq, tq), jnp.int32)],
        compiler_params=_cparams(("parallel",)),
        name="dsa_index",
    )(qi_t, w_t, ke, ko)


FLASH_TILE = 256
ONES_ROWS = 16


def _values_t(v, heads):
    L = v.shape[0]
    vt = jnp.transpose(v).reshape(heads, -1, L)
    return jnp.concatenate([vt, jnp.ones((heads, ONES_ROWS, L), v.dtype)], axis=1)


class _Flash:
    def __init__(self, s_ref, m_ref, acc_ref):
        self.s_ref, self.m_ref, self.acc_ref = s_ref, m_ref, acc_ref
        m_ref[...] = jnp.full(m_ref.shape, NEG, F32)
        acc_ref[...] = jnp.zeros(acc_ref.shape, F32)

    def step(self, cur, nxt):
        for c, n in zip(cur, nxt):
            if c is not None:
                buf, slot, vt = c
                s = self.s_ref[buf]
                m_prev = self.m_ref[slot]
                m_new = jnp.maximum(m_prev, jnp.max(s, axis=0, keepdims=True))
                alpha = jnp.exp2(m_prev - m_new)
                p = jnp.exp2(s - m_new).astype(BF16)
                self.m_ref[slot] = m_new
            if n is not None:
                buf_n, kt, qt, add, mask = n
                s_n = _dot(kt, qt)
                if add is not None:
                    s_n = s_n + add
                if mask is not None:
                    s_n = jnp.where(mask, s_n, NEG)
                self.s_ref[buf_n] = s_n
            if c is not None:
                self.acc_ref[slot] = alpha * self.acc_ref[slot] + _dot(vt, p)


def _dsa_attn_body(qt_ref, k_ref, vt_ref, bias_ref, o_ref, s_ref, m_ref, acc_ref, *, tq, tiles_per_step):
    i = pl.program_id(0)
    flash = _Flash(s_ref, m_ref, acc_ref)
    tk = FLASH_TILE
    dv = A_HEAD_DIM
    heads = [slice(h * dv, (h + 1) * dv) for h in range(A_HEADS)]

    def key_rows(step, u):
        return pl.ds(pl.multiple_of((step * tiles_per_step + u) * tk, tk), tk)

    def cur_side(step):
        return [(u * A_HEADS + h, h, vt_ref[h, :, key_rows(step, u)])
                for u in range(tiles_per_step) for h in range(A_HEADS)]

    def nxt_side(step):
        out = []
        for u in range(tiles_per_step):
            rows = key_rows(step, u)
            b = bias_ref[rows, :].astype(F32)
            out += [(u * A_HEADS + h, k_ref[rows, hs], qt_ref[hs, :], b, None) for h, hs in enumerate(heads)]
        return out

    idle = [None] * (tiles_per_step * A_HEADS)
    n_live = ((i + 1) * tq + tk - 1) // tk
    n_steps = (n_live + tiles_per_step - 1) // tiles_per_step
    flash.step(idle, nxt_side(0))

    def body(step, _):
        flash.step(cur_side(step), nxt_side(step + 1))
        return 0

    lax.fori_loop(0, n_steps - 1, body, 0)
    flash.step(cur_side(n_steps - 1), idle)
    for h, hs in enumerate(heads):
        o_ref[:, hs] = jnp.transpose(acc_ref[h, :dv, :] / acc_ref[h, dv:dv + 1, :]).astype(o_ref.dtype)


def _dsa_attn(q_t, k, v_t, bias, tq):
    L = k.shape[0]
    n_tiles = L // tq
    tiles_per_step = 2 if (L // FLASH_TILE) % 2 == 0 else 1
    return pl.pallas_call(
        functools.partial(_dsa_attn_body, tq=tq, tiles_per_step=tiles_per_step),
        grid=(n_tiles,),
        in_specs=[
            pl.BlockSpec((A_WIDTH, tq), lambda i: (0, i)),
            _resident((L, A_WIDTH), lambda i: (0, 0)),
            _resident((A_HEADS, A_HEAD_DIM + ONES_ROWS, L), lambda i: (0, 0, 0)),
            pl.BlockSpec((None, L, tq), lambda i: (i, 0, 0)),
        ],
        out_specs=pl.BlockSpec((tq, A_WIDTH), lambda i: (i, 0)),
        out_shape=jax.ShapeDtypeStruct((L, A_WIDTH), BF16),
        scratch_shapes=[
            pltpu.VMEM((tiles_per_step * A_HEADS, FLASH_TILE, tq), F32),
            pltpu.VMEM((A_HEADS, 1, tq), F32),
            pltpu.VMEM((A_HEADS, A_HEAD_DIM + ONES_ROWS, tq), F32),
        ],
        compiler_params=_cparams(("parallel",)),
        name="dsa_attn",
    )(q_t, k, v_t, bias)


def _diff_attn_body(lq1_ref, lk1_ref, lq2_ref, lk2_ref, g_ref, qt_ref, k_ref, vt_ref, o_ref,
                    s_ref, m_ref, acc_ref, *, tq, lam_init):
    i = pl.program_id(1)
    flash = _Flash(s_ref, m_ref, acc_ref)
    dv = C_V_DIM
    tk = FLASH_TILE
    n_chunks = tq // FLASH_TILE
    n_full = i * n_chunks
    qt = qt_ref[...]
    row = lax.broadcasted_iota(jnp.int32, qt.shape, 0)
    zero = jnp.zeros_like(qt)
    qts = (jnp.where(row < C_QK_DIM, qt, zero), jnp.where(row >= C_QK_DIM, qt, zero))
    kk = lax.broadcasted_iota(jnp.int32, (tk, FLASH_TILE), 0)
    qq = lax.broadcasted_iota(jnp.int32, (tk, FLASH_TILE), 1)
    streams = [(c * n_chunks + qc, qc, qts[c][:, qc * FLASH_TILE:(qc + 1) * FLASH_TILE])
               for qc in range(n_chunks) for c in range(2)]

    def tile_start(jt):
        return pl.multiple_of(jt * tk, tk)

    def cur_side(jt, first_chunk=0):
        vt = vt_ref[:, pl.ds(tile_start(jt), tk)]
        return [(slot, slot, vt) if qc >= first_chunk else None for slot, qc, _ in streams]

    def nxt_side(jt, first_chunk=0, mask_of=lambda qc: None):
        kt = k_ref[pl.ds(tile_start(jt), tk), :]
        return [(slot, kt, q, None, mask_of(qc)) if qc >= first_chunk else None for slot, qc, q in streams]

    idle = [None] * len(streams)

    def first_mask(qc):
        q_pos = (i * n_chunks + qc) * FLASH_TILE + qq
        return (kk >> CHUNK_SHIFT) <= (q_pos >> CHUNK_SHIFT)

    flash.step(idle, nxt_side(0, mask_of=first_mask))

    def body(jt, _):
        flash.step(cur_side(jt), nxt_side(jt + 1))
        return 0

    lax.fori_loop(0, jnp.maximum(n_full - 1, 0), body, 0)

    diag = lambda d: (lambda qc: ((kk >> CHUNK_SHIFT) <= (qq >> CHUNK_SHIFT)) if qc == d else None)

    @pl.when(i > 0)
    def _():
        flash.step(cur_side(n_full - 1), nxt_side(n_full, mask_of=diag(0)))

    for d in range(1, n_chunks):
        flash.step(cur_side(n_full + d - 1, d - 1), nxt_side(n_full + d, d, diag(d)))
    flash.step(cur_side(n_full + n_chunks - 1, n_chunks - 1), idle)

    lam = (jnp.exp(jnp.sum(lq1_ref[...] * lk1_ref[...], axis=1, keepdims=True))
           - jnp.exp(jnp.sum(lq2_ref[...] * lk2_ref[...], axis=1, keepdims=True)) + lam_init)
    for qc in range(n_chunks):
        s0, s1 = qc, n_chunks + qc
        o = (acc_ref[s0, :dv, :] / acc_ref[s0, dv:dv + 1, :]
             - lam * (acc_ref[s1, :dv, :] / acc_ref[s1, dv:dv + 1, :]))
        inv = lax.rsqrt(jnp.mean(o * o, axis=0, keepdims=True) + EPS)
        o_ref[qc * FLASH_TILE:(qc + 1) * FLASH_TILE, :] = (
            jnp.transpose(o * inv) * g_ref[...] * (1.0 - lam_init)).astype(o_ref.dtype)


def _diff_attn(q_t, k, v_t, lam_vecs, sub_gain, layer, lam_init, tq):
    L = k.shape[0]
    vec = lambda w: _resident((None, 1, w), lambda h, i: (layer, 0, 0))
    return pl.pallas_call(
        functools.partial(_diff_attn_body, tq=tq, lam_init=lam_init),
        grid=(C_HEADS, L // tq),
        in_specs=[
            vec(C_QK_DIM), vec(C_QK_DIM), vec(C_QK_DIM), vec(C_QK_DIM), vec(C_V_DIM),
            pl.BlockSpec((C_V_DIM, tq), lambda h, i: (h, i)),
            pl.BlockSpec((L, C_V_DIM), lambda h, i: (0, h)),
            pl.BlockSpec((None, C_V_DIM + ONES_ROWS, L), lambda h, i: (h, 0, 0)),
        ],
        out_specs=pl.BlockSpec((tq, C_V_DIM), lambda h, i: (i, h)),
        out_shape=jax.ShapeDtypeStruct((L, C_WIDTH), BF16),
        scratch_shapes=[
            pltpu.VMEM((2 * (tq // FLASH_TILE), FLASH_TILE, FLASH_TILE), F32),
            pltpu.VMEM((2 * (tq // FLASH_TILE), 1, FLASH_TILE), F32),
            pltpu.VMEM((2 * (tq // FLASH_TILE), C_V_DIM + ONES_ROWS, FLASH_TILE), F32),
        ],
        compiler_params=_cparams(("parallel", "parallel")),
        name="diff_attn",
    )(*lam_vecs, sub_gain, q_t, k, v_t)


def _s5_prep_body(are_ref, aim_ref, ldt_ref, bre_ref, bim_ref, bb_ref, lvl_ref, pw_ref):
    a_re, a_im = are_ref[...], aim_ref[...]
    dt = jnp.exp(ldt_ref[...])
    mag = jnp.exp(a_re * dt)
    lb_re = mag * jnp.cos(a_im * dt)
    lb_im = mag * jnp.sin(a_im * dt)
    den = a_re * a_re + a_im * a_im
    f_re = ((lb_re - 1.0) * a_re + lb_im * a_im) / den
    f_im = (lb_im * a_re - (lb_re - 1.0) * a_im) / den
    b_re, b_im = bre_ref[...], bim_ref[...]
    bb_ref[:, :S5_NSTATE] = (f_re * b_re - f_im * b_im).astype(BF16)
    bb_ref[:, S5_NSTATE:] = (f_re * b_im + f_im * b_re).astype(BF16)

    pows = [(lb_re, lb_im)]
    for _ in range(SUBLANES - 1):
        pr, pi = pows[-1]
        pows.append((pr * lb_re - pi * lb_im, pr * lb_im + pi * lb_re))
    row = lax.broadcasted_iota(jnp.int32, (SUBLANES, S5_NSTATE), 0)
    zero = jnp.zeros((SUBLANES, S5_NSTATE), F32)
    for lvl in range(3):
        sh = 1 << lvl
        pr, pi = pows[sh - 1]
        lvl_ref[lvl, 0] = jnp.where(row >= sh, pr, zero)
        lvl_ref[lvl, 1] = jnp.where(row >= sh, pi, zero)
    cr, ci = zero, zero
    for r in range(SUBLANES):
        cr = jnp.where(row == r, pows[r][0], cr)
        ci = jnp.where(row == r, pows[r][1], ci)
    pw_ref[0] = cr
    pw_ref[1] = ci


def _s5_prep(a_re, a_im, log_dt, b_re_bd, b_im_bd):
    depth = a_re.shape[0]
    vec = pl.BlockSpec((None, 1, S5_NSTATE), lambda l: (l, 0, 0))
    mat = pl.BlockSpec((None, S5_WIDTH, S5_NSTATE), lambda l: (l, 0, 0))
    return pl.pallas_call(
        _s5_prep_body,
        grid=(depth,),
        in_specs=[vec, vec, vec, mat, mat],
        out_specs=[
            pl.BlockSpec((None, S5_WIDTH, 2 * S5_NSTATE), lambda l: (l, 0, 0)),
            pl.BlockSpec((None, 3, 2, SUBLANES, S5_NSTATE), lambda l: (l, 0, 0, 0, 0)),
            pl.BlockSpec((None, 2, SUBLANES, S5_NSTATE), lambda l: (l, 0, 0, 0)),
        ],
        out_shape=[
            jax.ShapeDtypeStruct((depth, S5_WIDTH, 2 * S5_NSTATE), BF16),
            jax.ShapeDtypeStruct((depth, 3, 2, SUBLANES, S5_NSTATE), F32),
            jax.ShapeDtypeStruct((depth, 2, SUBLANES, S5_NSTATE), F32),
        ],
        compiler_params=_cparams(("parallel",)),
        name="s5_prep",
    )(a_re, a_im, log_dt, b_re_bd, b_im_bd)


S5_LANE_CHUNK = 512


def _s5_body(u_ref, bb_ref, lvl_ref, pw_ref, cre_ref, cim_ref, d_ref, wg_ref, bg_ref, o_ref,
             x_ref, carry_ref, *, tl):
    @pl.when(pl.program_id(0) == 0)
    def _():
        carry_ref[...] = jnp.zeros(carry_ref.shape, F32)

    u = u_ref[...]
    x_ref[...] = _dot(u.astype(BF16), bb_ref[...])

    for c0 in range(0, S5_NSTATE, S5_LANE_CHUNK):
        re_sl = slice(c0, c0 + S5_LANE_CHUNK)
        im_sl = slice(S5_NSTATE + c0, S5_NSTATE + c0 + S5_LANE_CHUNK)

        def block(t, carry, re_sl=re_sl, im_sl=im_sl):
            c_re, c_im = carry
            rows = pl.ds(pl.multiple_of(t * SUBLANES, SUBLANES), SUBLANES)
            re = x_ref[rows, re_sl]
            im = x_ref[rows, im_sl]
            for lvl in range(3):
                sh = 1 << lvl
                s_re = pltpu.roll(re, sh, 0)
                s_im = pltpu.roll(im, sh, 0)
                a_re = lvl_ref[lvl, 0, :, re_sl]
                a_im = lvl_ref[lvl, 1, :, re_sl]
                re, im = re + (a_re * s_re - a_im * s_im), im + (a_re * s_im + a_im * s_re)
            p_re = pw_ref[0, :, re_sl]
            p_im = pw_ref[1, :, re_sl]
            re, im = re + (p_re * c_re - p_im * c_im), im + (p_re * c_im + p_im * c_re)
            x_ref[rows, re_sl] = re
            x_ref[rows, im_sl] = im
            last = SUBLANES - 1
            return (jnp.broadcast_to(re[last:, :], re.shape), jnp.broadcast_to(im[last:, :], im.shape))

        carry = lax.fori_loop(0, tl // SUBLANES, block,
                              (carry_ref[0, :, re_sl], carry_ref[1, :, re_sl]))
        carry_ref[0, :, re_sl] = carry[0]
        carry_ref[1, :, re_sl] = carry[1]

    y = (_dot(x_ref[:, :S5_NSTATE].astype(BF16), cre_ref[...])
         - _dot(x_ref[:, S5_NSTATE:].astype(BF16), cim_ref[...])
         + d_ref[...] * u)
    y = 0.5 * y * (1.0 + jnp.tanh(math.sqrt(2.0 / math.pi) * (y + 0.044715 * (y * y * y))))
    z = _dot(y.astype(BF16), wg_ref[...]) + bg_ref[...]
    o_ref[...] = (y * (1.0 / (1.0 + jnp.exp(-z)))).astype(o_ref.dtype)


def _s5_mix(u, bb, lvl, pw, c_re_bd, c_im_bd, d_row, w_glu, b_glu, layer, tl):
    L = u.shape[0]
    lay = lambda *rest: (lambda t: (layer,) + rest)
    return pl.pallas_call(
        functools.partial(_s5_body, tl=tl),
        grid=(L // tl,),
        in_specs=[
            pl.BlockSpec((tl, S5_WIDTH), lambda t: (t, 0)),
            _resident((None, S5_WIDTH, 2 * S5_NSTATE), lay(0, 0)),
            _resident((None, 3, 2, SUBLANES, S5_NSTATE), lay(0, 0, 0, 0)),
            _resident((None, 2, SUBLANES, S5_NSTATE), lay(0, 0, 0)),
            _resident((None, S5_NSTATE, S5_WIDTH), lay(0, 0)),
            _resident((None, S5_NSTATE, S5_WIDTH), lay(0, 0)),
            _resident((None, 1, S5_WIDTH), lay(0, 0)),
            _resident((None, S5_WIDTH, S5_WIDTH), lay(0, 0)),
            _resident((None, 1, S5_WIDTH), lay(0, 0)),
        ],
        out_specs=pl.BlockSpec((tl, S5_WIDTH), lambda t: (t, 0)),
        out_shape=jax.ShapeDtypeStruct((L, S5_WIDTH), BF16),
        scratch_shapes=[
            pltpu.VMEM((tl, 2 * S5_NSTATE), F32),
            pltpu.VMEM((2, SUBLANES, S5_NSTATE), F32),
        ],
        compiler_params=_cparams(("arbitrary",)),
        name="s5_mix",
    )(u, bb, lvl, pw, c_re_bd, c_im_bd, d_row, w_glu, b_glu)


def _outproj_body(x_ref, oa_ref, os_ref, oc_ref, w_ref, o_ref):
    a0, a1, a2 = A_WIDTH, A_WIDTH + S5_WIDTH, A_WIDTH + S5_WIDTH + C_WIDTH
    o_ref[...] = (x_ref[...] + _dot(oa_ref[...], w_ref[:a0, :]) + _dot(os_ref[...], w_ref[a0:a1, :])
                  + _dot(oc_ref[...], w_ref[a1:a2, :]))


def _out_proj(x, o_a, o_s, o_c, w_out, layer, tm):
    L = x.shape[0]
    row = lambda i: (i, 0)
    return pl.pallas_call(
        _outproj_body,
        grid=(L // tm,),
        in_specs=[
            pl.BlockSpec((tm, D_MODEL), row),
            pl.BlockSpec((tm, A_WIDTH), row),
            pl.BlockSpec((tm, S5_WIDTH), row),
            pl.BlockSpec((tm, C_WIDTH), row),
            _resident((None, D_MODEL, D_MODEL), lambda i: (layer, 0, 0)),
        ],
        out_specs=pl.BlockSpec((tm, D_MODEL), row),
        out_shape=jax.ShapeDtypeStruct((L, D_MODEL), F32),
        compiler_params=_cparams(("parallel",)),
        name="out_proj",
    )(x, o_a, o_s, o_c, w_out)


def _norm_matmul_body(x_ref, g_ref, w_ref, o_ref):
    o_ref[...] = _dot(_rms(x_ref[...], g_ref[...]).astype(BF16), w_ref[...]).astype(o_ref.dtype)


def _norm_matmul(x, g, w, layer, tn):
    M, K = x.shape
    N = w.shape[-1]
    return pl.pallas_call(
        _norm_matmul_body,
        grid=(N // tn,),
        in_specs=[
            _resident((M, K), lambda j: (0, 0)),
            _resident((None, 1, K), lambda j: (layer, 0, 0)),
            pl.BlockSpec((None, K, tn), lambda j: (layer, 0, j)),
        ],
        out_specs=pl.BlockSpec((M, tn), lambda j: (0, j)),
        out_shape=jax.ShapeDtypeStruct((M, N), BF16),
        compiler_params=_cparams(("parallel",)),
        name="mem_proj",
    )(x, g, w)


def _xattn_body(x_ref, g_ref, wq_ref, k_ref, v_ref, wo_ref, o_ref):
    x = x_ref[...]
    q = _dot(_rms(x, g_ref[...]).astype(BF16), wq_ref[...]).astype(BF16)
    scale = X_HEAD_DIM ** -0.5
    out = x
    for h in range(X_HEADS):
        hs = slice(h * X_HEAD_DIM, (h + 1) * X_HEAD_DIM)
        s = _dot_nt(q[:, hs], k_ref[:, hs]) * scale
        p = jnp.exp(s - jnp.max(s, axis=1, keepdims=True))
        p = p / jnp.sum(p, axis=1, keepdims=True)
        o_h = _dot(p.astype(BF16), v_ref[:, hs]).astype(BF16)
        out = out + _dot(o_h, wo_ref[hs, :])
    o_ref[...] = out


def _cross_attn(x, g, wq, k_mem, v_mem, wo, layer, tm):
    L = x.shape[0]
    M = k_mem.shape[0]
    row = lambda i: (i, 0)
    return pl.pallas_call(
        _xattn_body,
        grid=(L // tm,),
        in_specs=[
            pl.BlockSpec((tm, D_MODEL), row),
            _resident((None, 1, D_MODEL), lambda i: (layer, 0, 0)),
            _resident((None, D_MODEL, D_MODEL), lambda i: (layer, 0, 0)),
            _resident((M, D_MODEL), lambda i: (0, 0)),
            _resident((M, D_MODEL), lambda i: (0, 0)),
            _resident((None, D_MODEL, D_MODEL), lambda i: (layer, 0, 0)),
        ],
        out_specs=pl.BlockSpec((tm, D_MODEL), row),
        out_shape=jax.ShapeDtypeStruct((L, D_MODEL), F32),
        compiler_params=_cparams(("parallel",)),
        name="cross_attn",
    )(x, g, wq, k_mem, v_mem, wo)


def _mlp_body(x_ref, g_ref, w1_ref, w2_ref, o_ref, hn_ref):
    @pl.when(pl.program_id(1) == 0)
    def _():
        x = x_ref[...]
        hn_ref[...] = _rms(x, g_ref[...]).astype(BF16)
        o_ref[...] = x

    a = jnp.maximum(_dot(hn_ref[...], w1_ref[...]), 0.0)
    o_ref[...] += _dot((a * a).astype(BF16), w2_ref[...])


def _mlp(x, g, w1, w2, layer, tm, tf):
    L = x.shape[0]
    return pl.pallas_call(
        _mlp_body,
        grid=(L // tm, D_FF // tf),
        in_specs=[
            pl.BlockSpec((tm, D_MODEL), lambda i, f: (i, 0)),
            _resident((None, 1, D_MODEL), lambda i, f: (layer, 0, 0)),
            pl.BlockSpec((None, D_MODEL, tf), lambda i, f: (layer, 0, f)),
            pl.BlockSpec((None, tf, D_MODEL), lambda i, f: (layer, f, 0)),
        ],
        out_specs=pl.BlockSpec((tm, D_MODEL), lambda i, f: (i, 0)),
        out_shape=jax.ShapeDtypeStruct((L, D_MODEL), F32),
        scratch_shapes=[pltpu.VMEM((tm, D_MODEL), BF16)],
        compiler_params=_cparams(("parallel", "arbitrary")),
        name="mlp",
    )(x, g, w1, w2)


def _final_norm_body(x_ref, g_ref, o_ref):
    o_ref[...] = _rms(x_ref[...], g_ref[...])


def _final_norm(x, g, tm):
    L = x.shape[0]
    return pl.pallas_call(
        _final_norm_body,
        grid=(L // tm,),
        in_specs=[pl.BlockSpec((tm, D_MODEL), lambda i: (i, 0)),
                  _resident((1, D_MODEL), lambda i: (0, 0))],
        out_specs=pl.BlockSpec((tm, D_MODEL), lambda i: (i, 0)),
        out_shape=jax.ShapeDtypeStruct((L, D_MODEL), F32),
        compiler_params=_cparams(("parallel",)),
        name="final_norm",
    )(x, g)


def _block_diag_in(b):
    eye = jnp.eye(S5_GROUPS, dtype=b.dtype)
    t = jnp.transpose(b, (0, 1, 3, 2))
    bd = t[:, :, :, None, :] * eye[None, :, None, :, None]
    return bd.reshape(b.shape[0], S5_WIDTH, S5_NSTATE)


def _block_diag_out(c):
    eye = jnp.eye(S5_GROUPS, dtype=c.dtype)
    t = jnp.transpose(c, (0, 1, 3, 2))
    bd = t[:, :, :, None, :] * eye[None, :, None, :, None]
    return bd.reshape(c.shape[0], S5_NSTATE, S5_WIDTH)


def _pick_tile(n, want):
    t = min(n, want)
    assert n % t == 0, (n, t)
    return t


def kernel(x, mem, norm_mix, w_in, s5_a_re, s5_a_im, s5_log_dt, s5_b_re, s5_b_im, s5_c_re, s5_c_im, s5_d, s5_w_glu, s5_b_glu, diff_lam_q1, diff_lam_k1, diff_lam_q2, diff_lam_k2, diff_subln, w_out, norm_xattn, norm_mem, xattn_q, xattn_k, xattn_v, xattn_o, norm_mlp, w_ff1, w_ff2, norm_final):
    B, L, _ = x.shape
    assert B == 1
    depth = w_in.shape[0]
    tq = _pick_tile(L, 256)

    pad = jnp.zeros((depth, D_MODEL, KIW_W - IDX_DIM - IDX_HEADS), w_in.dtype)
    w_in_p = jnp.concatenate([w_in[..., :IN_SRC_SPLIT], pad, w_in[..., IN_SRC_SPLIT:]], axis=-1).astype(BF16)
    w_out_b = w_out.astype(BF16)
    wq_b, wk_b, wv_b, wo_b = (w.astype(BF16) for w in (xattn_q, xattn_k, xattn_v, xattn_o))
    w1_b, w2_b = w_ff1.astype(BF16), w_ff2.astype(BF16)
    wg_b = s5_w_glu.astype(BF16)
    row3 = lambda a: a.reshape(depth, 1, -1)

    tabs, tabs_k = _rope_tables(L)

    rep = lambda a: row3(a.astype(F32))
    a_re, a_im = rep(s5_a_re), rep(s5_a_im)
    log_dt = row3(jnp.broadcast_to(s5_log_dt.astype(F32)[:, :, None], (depth, S5_GROUPS, S5_STATE)))
    bb, lvl, pw = _s5_prep(a_re, a_im, log_dt, _block_diag_in(s5_b_re.astype(F32)),
                           _block_diag_in(s5_b_im.astype(F32)))
    c_re_bd = _block_diag_out(s5_c_re).astype(BF16)
    c_im_bd = _block_diag_out(s5_c_im).astype(BF16)
    d_row = row3(s5_d.astype(F32))
    lam_vecs = tuple(row3(v.astype(F32)) for v in (diff_lam_q1, diff_lam_k1, diff_lam_q2, diff_lam_k2))

    xs = x[0]
    mem2 = mem[0]
    for l in range(depth):
        qa, ka, va, qi, kiw, us, qc, kc, vc = _in_proj(xs, row3(norm_mix), w_in_p, l, tabs + tabs_k, _pick_tile(L, 256))

        k_idx = kiw[:, :IDX_DIM].astype(BF16)
        zeros = jnp.zeros_like(k_idx)
        ke = jnp.concatenate([k_idx, zeros], axis=1)
        ko = jnp.concatenate([zeros, k_idx], axis=1)
        w_t = jnp.transpose(kiw[:, IDX_DIM:IDX_DIM + IDX_HEADS])
        bias = _dsa_index(jnp.transpose(qi), w_t, ke, ko, tq).reshape(L // tq, L, tq)
        o_a = _dsa_attn(jnp.transpose(qa), ka, _values_t(va, A_HEADS), bias, tq)

        o_s = _s5_mix(us, bb, lvl, pw, c_re_bd, c_im_bd, d_row, wg_b, row3(s5_b_glu.astype(F32)), l,
                      _pick_tile(L, 512))

        lam_init = 0.8 - 0.6 * math.exp(-0.3 * l)
        o_c = _diff_attn(jnp.transpose(qc), kc, _values_t(vc, C_HEADS), lam_vecs, row3(diff_subln.astype(F32)), l,
                         lam_init, _pick_tile(L, 1024))

        xs = _out_proj(xs, o_a, o_s, o_c, w_out_b, l, _pick_tile(L, 512))

        k_mem = _norm_matmul(mem2, row3(norm_mem), wk_b, l, 512)
        v_mem = _norm_matmul(mem2, row3(norm_mem), wv_b, l, 512)
        xs = _cross_attn(xs, row3(norm_xattn), wq_b, k_mem, v_mem, wo_b, l, _pick_tile(L, 512))

        xs = _mlp(xs, row3(norm_mlp), w1_b, w2_b, l, _pick_tile(L, 512), 512)

    return _final_norm(xs, norm_final.reshape(1, -1), _pick_tile(L, 512))[None]
```

```python
import functools
import math

import jax
import jax.numpy as jnp
from jax import lax
from jax.experimental import pallas as pl
from jax.experimental.pallas import tpu as pltpu

F32 = jnp.float32
BF16 = jnp.bfloat16

D_MODEL = 2048
CHUNK = 64
CHUNK_SHIFT = CHUNK.bit_length() - 1
assert 1 << CHUNK_SHIFT == CHUNK
ROPE_THETA = 10000.0
EPS = 1e-6
NEG = -1e30

A_HEAD_DIM = 128
A_WIDTH = D_MODEL // 4
A_HEADS = A_WIDTH // A_HEAD_DIM
IDX_HEADS = 8
IDX_DIM = 64
TOPK_MAX = 256

S5_GROUP = 16
S5_WIDTH = D_MODEL // 4
S5_GROUPS = S5_WIDTH // S5_GROUP
S5_STATE = 64
S5_NSTATE = S5_GROUPS * S5_STATE

C_QK_DIM = 64
C_V_DIM = 2 * C_QK_DIM
C_WIDTH = D_MODEL // 2
C_HEADS = C_WIDTH // C_V_DIM

X_HEADS = 4
X_HEAD_DIM = D_MODEL // X_HEADS
D_FF = 4 * D_MODEL

LANES = 128
SUBLANES = 8
VMEM_LIMIT = 56 * 1024 * 1024

KIW_W = LANES
OFF_QA = 0
OFF_KA = OFF_QA + A_WIDTH
OFF_VA = OFF_KA + A_WIDTH
OFF_QI = OFF_VA + A_WIDTH
OFF_KIW = OFF_QI + IDX_HEADS * IDX_DIM
OFF_US = OFF_KIW + KIW_W
OFF_QC = OFF_US + S5_WIDTH
OFF_KC = OFF_QC + C_WIDTH
OFF_VC = OFF_KC + C_WIDTH
IN_PACKED = OFF_VC + C_WIDTH
IN_SRC_SPLIT = 3 * A_WIDTH + IDX_HEADS * IDX_DIM + IDX_DIM + IDX_HEADS

LOG2E = math.log2(math.e)
INT_MIN = -2 ** 31
INT_MAX = 2 ** 31 - 1


def _cparams(sem):
    return pltpu.CompilerParams(dimension_semantics=sem, vmem_limit_bytes=VMEM_LIMIT)


def _resident(shape, index_map):
    return pl.BlockSpec(shape, index_map, pipeline_mode=pl.Buffered(1))


def _rms(x, g):
    inv = lax.rsqrt(jnp.mean(x * x, axis=-1, keepdims=True) + EPS)
    return x * inv * g


def _dot(a, b):
    return jnp.dot(a, b, preferred_element_type=F32)


def _dot_nt(a, b):
    return lax.dot_general(a, b, (((1,), (1,)), ((), ())), preferred_element_type=F32)


def _inproj_body(x_ref, g_ref, w_ref, c128_ref, s128_ref, c64_ref, sa64_ref, sb64_ref,
                 c64k_ref, sa64k_ref, sb64k_ref,
                 qa_ref, ka_ref, va_ref, qi_ref, kiw_ref, us_ref, qc_ref, kc_ref, vc_ref):
    hn = _rms(x_ref[...], g_ref[...]).astype(BF16)
    c128, s128 = c128_ref[...], s128_ref[...]
    c64, sa64, sb64 = c64_ref[...], sa64_ref[...], sb64_ref[...]

    def rope128(t):
        return t * c128 + pltpu.roll(t, 64, 1) * s128

    def rope64(t):
        return t * c64 + pltpu.roll(t, 96, 1) * sa64 + pltpu.roll(t, 32, 1) * sb64

    def rope64_kiw(t):
        return t * c64k_ref[...] + pltpu.roll(t, 96, 1) * sa64k_ref[...] + pltpu.roll(t, 32, 1) * sb64k_ref[...]

    def emit(off, width, out_ref, fn, scale=None):
        step = 512 if width >= 512 else width
        for c0 in range(0, width, step):
            t = _dot(hn, w_ref[:, off + c0:off + c0 + step])
            for s0 in range(0, step, LANES):
                v = t[:, s0:s0 + LANES]
                if fn is not None:
                    v = fn(v)
                if scale is not None:
                    v = v * scale
                out_ref[:, c0 + s0:c0 + s0 + LANES] = v.astype(out_ref.dtype)

    emit(OFF_QA, A_WIDTH, qa_ref, rope128, A_HEAD_DIM ** -0.5 * LOG2E)
    emit(OFF_KA, A_WIDTH, ka_ref, rope128)
    emit(OFF_VA, A_WIDTH, va_ref, None)
    emit(OFF_QI, IDX_HEADS * IDX_DIM, qi_ref, rope64)
    emit(OFF_KIW, KIW_W, kiw_ref, rope64_kiw)
    emit(OFF_US, S5_WIDTH, us_ref, None)
    emit(OFF_QC, C_WIDTH, qc_ref, rope64, C_QK_DIM ** -0.5 * LOG2E)
    emit(OFF_KC, C_WIDTH, kc_ref, rope64)
    emit(OFF_VC, C_WIDTH, vc_ref, None)


def _in_proj(x, g, w_packed, layer, tabs, tm):
    L = x.shape[0]
    row = lambda i: (i, 0)
    tab_spec = pl.BlockSpec((tm, LANES), row)
    outs = [
        (A_WIDTH, BF16), (A_WIDTH, BF16), (A_WIDTH, BF16), (IDX_HEADS * IDX_DIM, BF16),
        (KIW_W, F32), (S5_WIDTH, F32), (C_WIDTH, BF16), (C_WIDTH, BF16), (C_WIDTH, BF16),
    ]
    return pl.pallas_call(
        _inproj_body,
        grid=(L // tm,),
        in_specs=[
            pl.BlockSpec((tm, D_MODEL), row),
            _resident((None, 1, D_MODEL), lambda i: (layer, 0, 0)),
            _resident((None, D_MODEL, IN_PACKED), lambda i: (layer, 0, 0)),
        ] + [tab_spec] * 8 + [
        ],
        out_specs=[pl.BlockSpec((tm, w), row) for w, _ in outs],
        out_shape=[jax.ShapeDtypeStruct((L, w), dt) for w, dt in outs],
        compiler_params=_cparams(("parallel",)),
        name="in_proj",
    )(x, g, w_packed, *tabs)


def _rope_tables(L):
    pos = jnp.arange(L, dtype=F32)[:, None]

    def cs(dim):
        inv = ROPE_THETA ** (-jnp.arange(0, dim, 2, dtype=F32) / dim)
        ang = pos * inv[None, :]
        return jnp.cos(ang), jnp.sin(ang)

    c, s = cs(A_HEAD_DIM)
    c128 = jnp.concatenate([c, c], axis=1)
    s128 = jnp.concatenate([-s, s], axis=1)
    c, s = cs(IDX_DIM)
    z = jnp.zeros_like(s)
    c64 = jnp.tile(jnp.concatenate([c, c], axis=1), (1, 2))
    sa64 = jnp.tile(jnp.concatenate([-s, z], axis=1), (1, 2))
    sb64 = jnp.tile(jnp.concatenate([z, s], axis=1), (1, 2))
    lane = jnp.arange(LANES)[None, :]
    c64k = jnp.where(lane < IDX_DIM, c64, 1.0)
    sa64k = jnp.where(lane < IDX_DIM, sa64, 0.0)
    sb64k = jnp.where(lane < IDX_DIM, sb64, 0.0)
    return (c128, s128, c64, sa64, sb64), (c64k, sa64k, sb64k)


def _sortable(v):
    k = lax.bitcast_convert_type(v, jnp.int32)
    return k ^ ((k >> 31) & jnp.int32(INT_MAX))


COUNT_ROWS = 32


def _dsa_index_body(qit_ref, wt_ref, ke_ref, ko_ref, bias_ref, skey_ref, gmax_ref, *, tq, top_k, n_tiles):
    i = pl.program_id(0)
    score_tiles_per_step = 2 if n_tiles % 2 == 0 else 1
    n_live = i + 1
    q_pos = i * tq + lax.broadcasted_iota(jnp.int32, (tq, tq), 1)
    k_off = lax.broadcasted_iota(jnp.int32, (tq, tq), 0)
    wv = wt_ref[...] * ((IDX_HEADS ** -0.5) * (IDX_DIM ** -0.5))
    gmax_ref[...] = jnp.full((tq, tq), INT_MIN, jnp.int32)

    def score_tiles(step, _):
        for jt in [step * score_tiles_per_step + u for u in range(score_tiles_per_step)]:
            rows = pl.ds(pl.multiple_of(jt * tq, tq), tq)
            ke = ke_ref[rows, :]
            ko = ko_ref[rows, :]
            acc = jnp.zeros((tq, tq), F32)
            for hp in range(IDX_HEADS // 2):
                q_pair = qit_ref[hp * LANES:(hp + 1) * LANES, :]
                d0 = _dot(ke, q_pair)
                d1 = _dot(ko, q_pair)
                acc = (acc + wv[2 * hp:2 * hp + 1, :] * jnp.maximum(d0, 0.0)
                       + wv[2 * hp + 1:2 * hp + 2, :] * jnp.maximum(d1, 0.0))
            allowed = ((jt * tq + k_off) >> CHUNK_SHIFT) <= (q_pos >> CHUNK_SHIFT)
            key = _sortable(jnp.where(allowed, acc, NEG))
            skey_ref[jt] = key
            gmax_ref[...] = jnp.maximum(gmax_ref[...], key)
        return 0

    lax.fori_loop(0, (n_live + score_tiles_per_step - 1) // score_tiles_per_step, score_tiles, 0)

    def count(pred):
        def body(jt, acc):
            for r in range(0, tq, COUNT_ROWS):
                acc = acc + jnp.where(pred(skey_ref[jt, r:r + COUNT_ROWS, :]), 1.0, 0.0)
            return acc
        acc = lax.fori_loop(0, n_live, body, jnp.zeros((COUNT_ROWS, tq), F32))
        return jnp.sum(acc, axis=0, keepdims=True)

    kf = float(top_k)

    gmax = gmax_ref[...]
    lo0 = jnp.min(gmax, axis=0, keepdims=True)
    hi0 = jnp.max(gmax, axis=0, keepdims=True)

    def narrowing(carry):
        lo, hi = carry
        return jnp.max(jnp.where(hi != lo, 1.0, 0.0)) > 0.0

    def bisect(carry):
        lo, hi = carry
        mid = (lo >> 1) + (hi >> 1) + (((lo & 1) + (hi & 1) + 1) >> 1)
        cnt = count(lambda t: t >= mid)
        ok = cnt >= kf
        hi_new = jnp.where(cnt == kf, mid, jnp.where(ok, hi, mid - 1))
        return jnp.where(ok, mid, lo), hi_new

    at_zero = count(lambda t: t >= 0) >= kf
    above_zero = count(lambda t: t >= 1) >= kf
    lo1 = jnp.where(above_zero, jnp.maximum(lo0, 1), jnp.where(at_zero, 0, lo0))
    hi1 = jnp.where(above_zero, hi0, jnp.where(at_zero, 0, jnp.minimum(hi0, -1)))
    lo, _ = lax.while_loop(narrowing, bisect, (lo1, hi1))

    need = kf - count(lambda t: t > lo)
    tri = jnp.where(k_off >= lax.broadcasted_iota(jnp.int32, (tq, tq), 1), 1.0, 0.0).astype(BF16)

    def write_tile(jt, seen):
        t = skey_ref[jt]
        k_pos = jt * tq + k_off
        allowed = (k_pos >> CHUNK_SHIFT) <= (q_pos >> CHUNK_SHIFT)
        tied = t == lo
        rank = seen + _dot(tri, jnp.where(tied, 1.0, 0.0).astype(BF16))
        sel = ((t > lo) | (tied & (rank <= need))) & allowed
        bias_ref[jt] = jnp.where(sel, 0.0, NEG).astype(BF16)
        return rank[tq - 1:tq, :]

    lax.fori_loop(0, n_live, write_tile, jnp.zeros((1, tq), F32))

    def blank_tile(jt, _):
        bias_ref[jt] = jnp.full((tq, tq), NEG, BF16)
        return 0

    lax.fori_loop(n_live, n_tiles, blank_tile, 0)


def _dsa_index(qi_t, w_t, ke, ko, tq):
    L = qi_t.shape[1]
    n_tiles = L // tq
    top_k = min(TOPK_MAX, L // 4)
    assert top_k <= tq
    body = functools.partial(_dsa_index_body, tq=tq, top_k=top_k, n_tiles=n_tiles)
    return pl.pallas_call(
        body,
        grid=(n_tiles,),
        in_specs=[
            pl.BlockSpec((IDX_HEADS * IDX_DIM, tq), lambda i: (0, i)),
            pl.BlockSpec((IDX_HEADS, tq), lambda i: (0, i)),
            _resident((L, LANES), lambda i: (0, 0)),
            _resident((L, LANES), lambda i: (0, 0)),
        ],
        out_specs=pl.BlockSpec((None, n_tiles, tq, tq), lambda i: (i, 0, 0, 0)),
        out_shape=jax.ShapeDtypeStruct((n_tiles, n_tiles, tq, tq), BF16),
        scratch_shapes=[pltpu.VMEM((n_tiles, tq, tq), jnp.int32), pltpu.VMEM((tq, tq), jnp.int32)],
        compiler_params=_cparams(("parallel",)),
        name="dsa_index",
    )(qi_t, w_t, ke, ko)


FLASH_TILE = 256
ONES_ROWS = 16


def _values_t(v, heads):
    L = v.shape[0]
    vt = jnp.transpose(v).reshape(heads, -1, L)
    return jnp.concatenate([vt, jnp.ones((heads, ONES_ROWS, L), v.dtype)], axis=1)


class _Flash:
    def __init__(self, s_ref, m_ref, acc_ref):
        self.s_ref, self.m_ref, self.acc_ref = s_ref, m_ref, acc_ref
        m_ref[...] = jnp.full(m_ref.shape, NEG, F32)
        acc_ref[...] = jnp.zeros(acc_ref.shape, F32)

    def step(self, cur, nxt):
        for c, n in zip(cur, nxt):
            if c is not None:
                buf, slot, vt = c
                s = self.s_ref[buf]
                m_prev = self.m_ref[slot]
                m_new = jnp.maximum(m_prev, jnp.max(s, axis=0, keepdims=True))
                alpha = jnp.exp2(m_prev - m_new)
                p = jnp.exp2(s - m_new).astype(BF16)
                self.m_ref[slot] = m_new
            if n is not None:
                buf_n, kt, qt, add, mask = n
                s_n = _dot(kt, qt)
                if add is not None:
                    s_n = s_n + add
                if mask is not None:
                    s_n = jnp.where(mask, s_n, NEG)
                self.s_ref[buf_n] = s_n
            if c is not None:
                self.acc_ref[slot] = alpha * self.acc_ref[slot] + _dot(vt, p)


def _dsa_attn_body(qt_ref, k_ref, vt_ref, bias_ref, o_ref, s_ref, m_ref, acc_ref, *, tq, tiles_per_step):
    i = pl.program_id(0)
    flash = _Flash(s_ref, m_ref, acc_ref)
    tk = FLASH_TILE
    dv = A_HEAD_DIM
    heads = [slice(h * dv, (h + 1) * dv) for h in range(A_HEADS)]

    def key_rows(step, u):
        return pl.ds(pl.multiple_of((step * tiles_per_step + u) * tk, tk), tk)

    def cur_side(step):
        return [(u * A_HEADS + h, h, vt_ref[h, :, key_rows(step, u)])
                for u in range(tiles_per_step) for h in range(A_HEADS)]

    def nxt_side(step):
        out = []
        for u in range(tiles_per_step):
            rows = key_rows(step, u)
            b = bias_ref[rows, :].astype(F32)
            out += [(u * A_HEADS + h, k_ref[rows, hs], qt_ref[hs, :], b, None) for h, hs in enumerate(heads)]
        return out

    idle = [None] * (tiles_per_step * A_HEADS)
    n_live = ((i + 1) * tq + tk - 1) // tk
    n_steps = (n_live + tiles_per_step - 1) // tiles_per_step
    flash.step(idle, nxt_side(0))

    def body(step, _):
        flash.step(cur_side(step), nxt_side(step + 1))
        return 0

    lax.fori_loop(0, n_steps - 1, body, 0)
    flash.step(cur_side(n_steps - 1), idle)
    for h, hs in enumerate(heads):
        o_ref[:, hs] = jnp.transpose(acc_ref[h, :dv, :] / acc_ref[h, dv:dv + 1, :]).astype(o_ref.dtype)


def _dsa_attn(q_t, k, v_t, bias, tq):
    L = k.shape[0]
    n_tiles = L // tq
    tiles_per_step = 2 if (L // FLASH_TILE) % 2 == 0 else 1
    return pl.pallas_call(
        functools.partial(_dsa_attn_body, tq=tq, tiles_per_step=tiles_per_step),
        grid=(n_tiles,),
        in_specs=[
            pl.BlockSpec((A_WIDTH, tq), lambda i: (0, i)),
            _resident((L, A_WIDTH), lambda i: (0, 0)),
            _resident((A_HEADS, A_HEAD_DIM + ONES_ROWS, L), lambda i: (0, 0, 0)),
            pl.BlockSpec((None, L, tq), lambda i: (i, 0, 0)),
        ],
        out_specs=pl.BlockSpec((tq, A_WIDTH), lambda i: (i, 0)),
        out_shape=jax.ShapeDtypeStruct((L, A_WIDTH), BF16),
        scratch_shapes=[
            pltpu.VMEM((tiles_per_step * A_HEADS, FLASH_TILE, tq), F32),
            pltpu.VMEM((A_HEADS, 1, tq), F32),
            pltpu.VMEM((A_HEADS, A_HEAD_DIM + ONES_ROWS, tq), F32),
        ],
        compiler_params=_cparams(("parallel",)),
        name="dsa_attn",
    )(q_t, k, v_t, bias)


def _diff_attn_body(lq1_ref, lk1_ref, lq2_ref, lk2_ref, g_ref, qt_ref, k_ref, vt_ref, o_ref,
                    s_ref, m_ref, acc_ref, *, tq, lam_init):
    i = pl.program_id(1)
    flash = _Flash(s_ref, m_ref, acc_ref)
    dv = C_V_DIM
    tk = FLASH_TILE
    n_chunks = tq // FLASH_TILE
    n_full = i * n_chunks
    qt = qt_ref[...]
    row = lax.broadcasted_iota(jnp.int32, qt.shape, 0)
    zero = jnp.zeros_like(qt)
    qts = (jnp.where(row < C_QK_DIM, qt, zero), jnp.where(row >= C_QK_DIM, qt, zero))
    kk = lax.broadcasted_iota(jnp.int32, (tk, FLASH_TILE), 0)
    qq = lax.broadcasted_iota(jnp.int32, (tk, FLASH_TILE), 1)
    streams = [(c * n_chunks + qc, qc, qts[c][:, qc * FLASH_TILE:(qc + 1) * FLASH_TILE])
               for qc in range(n_chunks) for c in range(2)]

    def tile_start(jt):
        return pl.multiple_of(jt * tk, tk)

    def cur_side(jt, first_chunk=0):
        vt = vt_ref[:, pl.ds(tile_start(jt), tk)]
        return [(slot, slot, vt) if qc >= first_chunk else None for slot, qc, _ in streams]

    def nxt_side(jt, first_chunk=0, mask_of=lambda qc: None):
        kt = k_ref[pl.ds(tile_start(jt), tk), :]
        return [(slot, kt, q, None, mask_of(qc)) if qc >= first_chunk else None for slot, qc, q in streams]

    idle = [None] * len(streams)

    def first_mask(qc):
        q_pos = (i * n_chunks + qc) * FLASH_TILE + qq
        return (kk >> CHUNK_SHIFT) <= (q_pos >> CHUNK_SHIFT)

    flash.step(idle, nxt_side(0, mask_of=first_mask))

    def body(jt, _):
        flash.step(cur_side(jt), nxt_side(jt + 1))
        return 0

    lax.fori_loop(0, jnp.maximum(n_full - 1, 0), body, 0)

    diag = lambda d: (lambda qc: ((kk >> CHUNK_SHIFT) <= (qq >> CHUNK_SHIFT)) if qc == d else None)

    @pl.when(i > 0)
    def _():
        flash.step(cur_side(n_full - 1), nxt_side(n_full, mask_of=diag(0)))

    for d in range(1, n_chunks):
        flash.step(cur_side(n_full + d - 1, d - 1), nxt_side(n_full + d, d, diag(d)))
    flash.step(cur_side(n_full + n_chunks - 1, n_chunks - 1), idle)

    lam = (jnp.exp(jnp.sum(lq1_ref[...] * lk1_ref[...], axis=1, keepdims=True))
           - jnp.exp(jnp.sum(lq2_ref[...] * lk2_ref[...], axis=1, keepdims=True)) + lam_init)
    for qc in range(n_chunks):
        s0, s1 = qc, n_chunks + qc
        o = (acc_ref[s0, :dv, :] / acc_ref[s0, dv:dv + 1, :]
             - lam * (acc_ref[s1, :dv, :] / acc_ref[s1, dv:dv + 1, :]))
        inv = lax.rsqrt(jnp.mean(o * o, axis=0, keepdims=True) + EPS)
        o_ref[qc * FLASH_TILE:(qc + 1) * FLASH_TILE, :] = (
            jnp.transpose(o * inv) * g_ref[...] * (1.0 - lam_init)).astype(o_ref.dtype)


def _diff_attn(q_t, k, v_t, lam_vecs, sub_gain, layer, lam_init, tq):
    L = k.shape[0]
    vec = lambda w: _resident((None, 1, w), lambda h, i: (layer, 0, 0))
    return pl.pallas_call(
        functools.partial(_diff_attn_body, tq=tq, lam_init=lam_init),
        grid=(C_HEADS, L // tq),
        in_specs=[
            vec(C_QK_DIM), vec(C_QK_DIM), vec(C_QK_DIM), vec(C_QK_DIM), vec(C_V_DIM),
            pl.BlockSpec((C_V_DIM, tq), lambda h, i: (h, i)),
            pl.BlockSpec((L, C_V_DIM), lambda h, i: (0, h)),
            pl.BlockSpec((None, C_V_DIM + ONES_ROWS, L), lambda h, i: (h, 0, 0)),
        ],
        out_specs=pl.BlockSpec((tq, C_V_DIM), lambda h, i: (i, h)),
        out_shape=jax.ShapeDtypeStruct((L, C_WIDTH), BF16),
        scratch_shapes=[
            pltpu.VMEM((2 * (tq // FLASH_TILE), FLASH_TILE, FLASH_TILE), F32),
            pltpu.VMEM((2 * (tq // FLASH_TILE), 1, FLASH_TILE), F32),
            pltpu.VMEM((2 * (tq // FLASH_TILE), C_V_DIM + ONES_ROWS, FLASH_TILE), F32),
        ],
        compiler_params=_cparams(("parallel", "parallel")),
        name="diff_attn",
    )(*lam_vecs, sub_gain, q_t, k, v_t)


def _s5_prep_body(are_ref, aim_ref, ldt_ref, bre_ref, bim_ref, bb_ref, lvl_ref, pw_ref):
    a_re, a_im = are_ref[...], aim_ref[...]
    dt = jnp.exp(ldt_ref[...])
    mag = jnp.exp(a_re * dt)
    lb_re = mag * jnp.cos(a_im * dt)
    lb_im = mag * jnp.sin(a_im * dt)
    den = a_re * a_re + a_im * a_im
    f_re = ((lb_re - 1.0) * a_re + lb_im * a_im) / den
    f_im = (lb_im * a_re - (lb_re - 1.0) * a_im) / den
    b_re, b_im = bre_ref[...], bim_ref[...]
    bb_ref[:, :S5_NSTATE] = (f_re * b_re - f_im * b_im).astype(BF16)
    bb_ref[:, S5_NSTATE:] = (f_re * b_im + f_im * b_re).astype(BF16)

    pows = [(lb_re, lb_im)]
    for _ in range(SUBLANES - 1):
        pr, pi = pows[-1]
        pows.append((pr * lb_re - pi * lb_im, pr * lb_im + pi * lb_re))
    row = lax.broadcasted_iota(jnp.int32, (SUBLANES, S5_NSTATE), 0)
    zero = jnp.zeros((SUBLANES, S5_NSTATE), F32)
    for lvl in range(3):
        sh = 1 << lvl
        pr, pi = pows[sh - 1]
        lvl_ref[lvl, 0] = jnp.where(row >= sh, pr, zero)
        lvl_ref[lvl, 1] = jnp.where(row >= sh, pi, zero)
    cr, ci = zero, zero
    for r in range(SUBLANES):
        cr = jnp.where(row == r, pows[r][0], cr)
        ci = jnp.where(row == r, pows[r][1], ci)
    pw_ref[0] = cr
    pw_ref[1] = ci


def _s5_prep(a_re, a_im, log_dt, b_re_bd, b_im_bd):
    depth = a_re.shape[0]
    vec = pl.BlockSpec((None, 1, S5_NSTATE), lambda l: (l, 0, 0))
    mat = pl.BlockSpec((None, S5_WIDTH, S5_NSTATE), lambda l: (l, 0, 0))
    return pl.pallas_call(
        _s5_prep_body,
        grid=(depth,),
        in_specs=[vec, vec, vec, mat, mat],
        out_specs=[
            pl.BlockSpec((None, S5_WIDTH, 2 * S5_NSTATE), lambda l: (l, 0, 0)),
            pl.BlockSpec((None, 3, 2, SUBLANES, S5_NSTATE), lambda l: (l, 0, 0, 0, 0)),
            pl.BlockSpec((None, 2, SUBLANES, S5_NSTATE), lambda l: (l, 0, 0, 0)),
        ],
        out_shape=[
            jax.ShapeDtypeStruct((depth, S5_WIDTH, 2 * S5_NSTATE), BF16),
            jax.ShapeDtypeStruct((depth, 3, 2, SUBLANES, S5_NSTATE), F32),
            jax.ShapeDtypeStruct((depth, 2, SUBLANES, S5_NSTATE), F32),
        ],
        compiler_params=_cparams(("parallel",)),
        name="s5_prep",
    )(a_re, a_im, log_dt, b_re_bd, b_im_bd)


S5_LANE_CHUNK = 512


def _s5_body(u_ref, bb_ref, lvl_ref, pw_ref, cre_ref, cim_ref, d_ref, wg_ref, bg_ref, o_ref,
             x_ref, carry_ref, *, tl):
    @pl.when(pl.program_id(0) == 0)
    def _():
        carry_ref[...] = jnp.zeros(carry_ref.shape, F32)

    u = u_ref[...]
    x_ref[...] = _dot(u.astype(BF16), bb_ref[...])

    for c0 in range(0, S5_NSTATE, S5_LANE_CHUNK):
        re_sl = slice(c0, c0 + S5_LANE_CHUNK)
        im_sl = slice(S5_NSTATE + c0, S5_NSTATE + c0 + S5_LANE_CHUNK)

        def block(t, carry, re_sl=re_sl, im_sl=im_sl):
            c_re, c_im = carry
            rows = pl.ds(pl.multiple_of(t * SUBLANES, SUBLANES), SUBLANES)
            re = x_ref[rows, re_sl]
            im = x_ref[rows, im_sl]
            for lvl in range(3):
                sh = 1 << lvl
                s_re = pltpu.roll(re, sh, 0)
                s_im = pltpu.roll(im, sh, 0)
                a_re = lvl_ref[lvl, 0, :, re_sl]
                a_im = lvl_ref[lvl, 1, :, re_sl]
                re, im = re + (a_re * s_re - a_im * s_im), im + (a_re * s_im + a_im * s_re)
            p_re = pw_ref[0, :, re_sl]
            p_im = pw_ref[1, :, re_sl]
            re, im = re + (p_re * c_re - p_im * c_im), im + (p_re * c_im + p_im * c_re)
            x_ref[rows, re_sl] = re
            x_ref[rows, im_sl] = im
            last = SUBLANES - 1
            return (jnp.broadcast_to(re[last:, :], re.shape), jnp.broadcast_to(im[last:, :], im.shape))

        carry = lax.fori_loop(0, tl // SUBLANES, block,
                              (carry_ref[0, :, re_sl], carry_ref[1, :, re_sl]))
        carry_ref[0, :, re_sl] = carry[0]
        carry_ref[1, :, re_sl] = carry[1]

    y = (_dot(x_ref[:, :S5_NSTATE].astype(BF16), cre_ref[...])
         - _dot(x_ref[:, S5_NSTATE:].astype(BF16), cim_ref[...])
         + d_ref[...] * u)
    y = 0.5 * y * (1.0 + jnp.tanh(math.sqrt(2.0 / math.pi) * (y + 0.044715 * (y * y * y))))
    z = _dot(y.astype(BF16), wg_ref[...]) + bg_ref[...]
    o_ref[...] = (y * (1.0 / (1.0 + jnp.exp(-z)))).astype(o_ref.dtype)


def _s5_mix(u, bb, lvl, pw, c_re_bd, c_im_bd, d_row, w_glu, b_glu, layer, tl):
    L = u.shape[0]
    lay = lambda *rest: (lambda t: (layer,) + rest)
    return pl.pallas_call(
        functools.partial(_s5_body, tl=tl),
        grid=(L // tl,),
        in_specs=[
            pl.BlockSpec((tl, S5_WIDTH), lambda t: (t, 0)),
            _resident((None, S5_WIDTH, 2 * S5_NSTATE), lay(0, 0)),
            _resident((None, 3, 2, SUBLANES, S5_NSTATE), lay(0, 0, 0, 0)),
            _resident((None, 2, SUBLANES, S5_NSTATE), lay(0, 0, 0)),
            _resident((None, S5_NSTATE, S5_WIDTH), lay(0, 0)),
            _resident((None, S5_NSTATE, S5_WIDTH), lay(0, 0)),
            _resident((None, 1, S5_WIDTH), lay(0, 0)),
            _resident((None, S5_WIDTH, S5_WIDTH), lay(0, 0)),
            _resident((None, 1, S5_WIDTH), lay(0, 0)),
        ],
        out_specs=pl.BlockSpec((tl, S5_WIDTH), lambda t: (t, 0)),
        out_shape=jax.ShapeDtypeStruct((L, S5_WIDTH), BF16),
        scratch_shapes=[
            pltpu.VMEM((tl, 2 * S5_NSTATE), F32),
            pltpu.VMEM((2, SUBLANES, S5_NSTATE), F32),
        ],
        compiler_params=_cparams(("arbitrary",)),
        name="s5_mix",
    )(u, bb, lvl, pw, c_re_bd, c_im_bd, d_row, w_glu, b_glu)


def _outproj_body(x_ref, oa_ref, os_ref, oc_ref, w_ref, o_ref):
    a0, a1, a2 = A_WIDTH, A_WIDTH + S5_WIDTH, A_WIDTH + S5_WIDTH + C_WIDTH
    o_ref[...] = (x_ref[...] + _dot(oa_ref[...], w_ref[:a0, :]) + _dot(os_ref[...], w_ref[a0:a1, :])
                  + _dot(oc_ref[...], w_ref[a1:a2, :]))


def _out_proj(x, o_a, o_s, o_c, w_out, layer, tm):
    L = x.shape[0]
    row = lambda i: (i, 0)
    return pl.pallas_call(
        _outproj_body,
        grid=(L // tm,),
        in_specs=[
            pl.BlockSpec((tm, D_MODEL), row),
            pl.BlockSpec((tm, A_WIDTH), row),
            pl.BlockSpec((tm, S5_WIDTH), row),
            pl.BlockSpec((tm, C_WIDTH), row),
            _resident((None, D_MODEL, D_MODEL), lambda i: (layer, 0, 0)),
        ],
        out_specs=pl.BlockSpec((tm, D_MODEL), row),
        out_shape=jax.ShapeDtypeStruct((L, D_MODEL), F32),
        compiler_params=_cparams(("parallel",)),
        name="out_proj",
    )(x, o_a, o_s, o_c, w_out)


def _norm_matmul_body(x_ref, g_ref, w_ref, o_ref):
    o_ref[...] = _dot(_rms(x_ref[...], g_ref[...]).astype(BF16), w_ref[...]).astype(o_ref.dtype)


def _norm_matmul(x, g, w, layer, tn):
    M, K = x.shape
    N = w.shape[-1]
    return pl.pallas_call(
        _norm_matmul_body,
        grid=(N // tn,),
        in_specs=[
            _resident((M, K), lambda j: (0, 0)),
            _resident((None, 1, K), lambda j: (layer, 0, 0)),
            pl.BlockSpec((None, K, tn), lambda j: (layer, 0, j)),
        ],
        out_specs=pl.BlockSpec((M, tn), lambda j: (0, j)),
        out_shape=jax.ShapeDtypeStruct((M, N), BF16),
        compiler_params=_cparams(("parallel",)),
        name="mem_proj",
    )(x, g, w)


def _xattn_body(x_ref, g_ref, wq_ref, k_ref, v_ref, wo_ref, o_ref):
    x = x_ref[...]
    q = _dot(_rms(x, g_ref[...]).astype(BF16), wq_ref[...]).astype(BF16)
    scale = X_HEAD_DIM ** -0.5
    out = x
    for h in range(X_HEADS):
        hs = slice(h * X_HEAD_DIM, (h + 1) * X_HEAD_DIM)
        s = _dot_nt(q[:, hs], k_ref[:, hs]) * scale
        p = jnp.exp(s - jnp.max(s, axis=1, keepdims=True))
        p = p / jnp.sum(p, axis=1, keepdims=True)
        o_h = _dot(p.astype(BF16), v_ref[:, hs]).astype(BF16)
        out = out + _dot(o_h, wo_ref[hs, :])
    o_ref[...] = out


def _cross_attn(x, g, wq, k_mem, v_mem, wo, layer, tm):
    L = x.shape[0]
    M = k_mem.shape[0]
    row = lambda i: (i, 0)
    return pl.pallas_call(
        _xattn_body,
        grid=(L // tm,),
        in_specs=[
            pl.BlockSpec((tm, D_MODEL), row),
            _resident((None, 1, D_MODEL), lambda i: (layer, 0, 0)),
            _resident((None, D_MODEL, D_MODEL), lambda i: (layer, 0, 0)),
            _resident((M, D_MODEL), lambda i: (0, 0)),
            _resident((M, D_MODEL), lambda i: (0, 0)),
            _resident((None, D_MODEL, D_MODEL), lambda i: (layer, 0, 0)),
        ],
        out_specs=pl.BlockSpec((tm, D_MODEL), row),
        out_shape=jax.ShapeDtypeStruct((L, D_MODEL), F32),
        compiler_params=_cparams(("parallel",)),
        name="cross_attn",
    )(x, g, wq, k_mem, v_mem, wo)


def _mlp_body(x_ref, g_ref, w1_ref, w2_ref, o_ref, hn_ref):
    @pl.when(pl.program_id(1) == 0)
    def _():
        x = x_ref[...]
        hn_ref[...] = _rms(x, g_ref[...]).astype(BF16)
        o_ref[...] = x

    a = jnp.maximum(_dot(hn_ref[...], w1_ref[...]), 0.0)
    o_ref[...] += _dot((a * a).astype(BF16), w2_ref[...])


def _mlp(x, g, w1, w2, layer, tm, tf):
    L = x.shape[0]
    return pl.pallas_call(
        _mlp_body,
        grid=(L // tm, D_FF // tf),
        in_specs=[
            pl.BlockSpec((tm, D_MODEL), lambda i, f: (i, 0)),
            _resident((None, 1, D_MODEL), lambda i, f: (layer, 0, 0)),
            pl.BlockSpec((None, D_MODEL, tf), lambda i, f: (layer, 0, f)),
            pl.BlockSpec((None, tf, D_MODEL), lambda i, f: (layer, f, 0)),
        ],
        out_specs=pl.BlockSpec((tm, D_MODEL), lambda i, f: (i, 0)),
        out_shape=jax.ShapeDtypeStruct((L, D_MODEL), F32),
        scratch_shapes=[pltpu.VMEM((tm, D_MODEL), BF16)],
        compiler_params=_cparams(("parallel", "arbitrary")),
        name="mlp",
    )(x, g, w1, w2)


def _final_norm_body(x_ref, g_ref, o_ref):
    o_ref[...] = _rms(x_ref[...], g_ref[...])


def _final_norm(x, g, tm):
    L = x.shape[0]
    return pl.pallas_call(
        _final_norm_body,
        grid=(L // tm,),
        in_specs=[pl.BlockSpec((tm, D_MODEL), lambda i: (i, 0)),
                  _resident((1, D_MODEL), lambda i: (0, 0))],
        out_specs=pl.BlockSpec((tm, D_MODEL), lambda i: (i, 0)),
        out_shape=jax.ShapeDtypeStruct((L, D_MODEL), F32),
        compiler_params=_cparams(("parallel",)),
        name="final_norm",
    )(x, g)


def _block_diag_in(b):
    eye = jnp.eye(S5_GROUPS, dtype=b.dtype)
    t = jnp.transpose(b, (0, 1, 3, 2))
    bd = t[:, :, :, None, :] * eye[None, :, None, :, None]
    return bd.reshape(b.shape[0], S5_WIDTH, S5_NSTATE)


def _block_diag_out(c):
    eye = jnp.eye(S5_GROUPS, dtype=c.dtype)
    t = jnp.transpose(c, (0, 1, 3, 2))
    bd = t[:, :, :, None, :] * eye[None, :, None, :, None]
    return bd.reshape(c.shape[0], S5_NSTATE, S5_WIDTH)


def _pick_tile(n, want):
    t = min(n, want)
    assert n % t == 0, (n, t)
    return t


def kernel(x, mem, norm_mix, w_in, s5_a_re, s5_a_im, s5_log_dt, s5_b_re, s5_b_im, s5_c_re, s5_c_im, s5_d, s5_w_glu, s5_b_glu, diff_lam_q1, diff_lam_k1, diff_lam_q2, diff_lam_k2, diff_subln, w_out, norm_xattn, norm_mem, xattn_q, xattn_k, xattn_v, xattn_o, norm_mlp, w_ff1, w_ff2, norm_final):
    B, L, _ = x.shape
    assert B == 1
    depth = w_in.shape[0]
    tq = _pick_tile(L, 256)

    pad = jnp.zeros((depth, D_MODEL, KIW_W - IDX_DIM - IDX_HEADS), w_in.dtype)
    w_in_p = jnp.concatenate([w_in[..., :IN_SRC_SPLIT], pad, w_in[..., IN_SRC_SPLIT:]], axis=-1).astype(BF16)
    w_out_b = w_out.astype(BF16)
    wq_b, wk_b, wv_b, wo_b = (w.astype(BF16) for w in (xattn_q, xattn_k, xattn_v, xattn_o))
    w1_b, w2_b = w_ff1.astype(BF16), w_ff2.astype(BF16)
    wg_b = s5_w_glu.astype(BF16)
    row3 = lambda a: a.reshape(depth, 1, -1)

    tabs, tabs_k = _rope_tables(L)

    rep = lambda a: row3(a.astype(F32))
    a_re, a_im = rep(s5_a_re), rep(s5_a_im)
    log_dt = row3(jnp.broadcast_to(s5_log_dt.astype(F32)[:, :, None], (depth, S5_GROUPS, S5_STATE)))
    bb, lvl, pw = _s5_prep(a_re, a_im, log_dt, _block_diag_in(s5_b_re.astype(F32)),
                           _block_diag_in(s5_b_im.astype(F32)))
    c_re_bd = _block_diag_out(s5_c_re).astype(BF16)
    c_im_bd = _block_diag_out(s5_c_im).astype(BF16)
    d_row = row3(s5_d.astype(F32))
    lam_vecs = tuple(row3(v.astype(F32)) for v in (diff_lam_q1, diff_lam_k1, diff_lam_q2, diff_lam_k2))

    xs = x[0]
    mem2 = mem[0]
    for l in range(depth):
        qa, ka, va, qi, kiw, us, qc, kc, vc = _in_proj(xs, row3(norm_mix), w_in_p, l, tabs + tabs_k, _pick_tile(L, 256))

        k_idx = kiw[:, :IDX_DIM].astype(BF16)
        zeros = jnp.zeros_like(k_idx)
        ke = jnp.concatenate([k_idx, zeros], axis=1)
        ko = jnp.concatenate([zeros, k_idx], axis=1)
        w_t = jnp.transpose(kiw[:, IDX_DIM:IDX_DIM + IDX_HEADS])
        bias = _dsa_index(jnp.transpose(qi), w_t, ke, ko, tq).reshape(L // tq, L, tq)
        o_a = _dsa_attn(jnp.transpose(qa), ka, _values_t(va, A_HEADS), bias, tq)

        o_s = _s5_mix(us, bb, lvl, pw, c_re_bd, c_im_bd, d_row, wg_b, row3(s5_b_glu.astype(F32)), l,
                      _pick_tile(L, 512))

        lam_init = 0.8 - 0.6 * math.exp(-0.3 * l)
        o_c = _diff_attn(jnp.transpose(qc), kc, _values_t(vc, C_HEADS), lam_vecs, row3(diff_subln.astype(F32)), l,
                         lam_init, _pick_tile(L, 2048))

        xs = _out_proj(xs, o_a, o_s, o_c, w_out_b, l, _pick_tile(L, 512))

        k_mem = _norm_matmul(mem2, row3(norm_mem), wk_b, l, 512)
        v_mem = _norm_matmul(mem2, row3(norm_mem), wv_b, l, 512)
        xs = _cross_attn(xs, row3(norm_xattn), wq_b, k_mem, v_mem, wo_b, l, _pick_tile(L, 512))

        xs = _mlp(xs, row3(norm_mlp), w1_b, w2_b, l, _pick_tile(L, 1024), 512)

    return _final_norm(xs, norm_final.reshape(1, -1), _pick_tile(L, 512))[None]
```

```python
import functools
import math

import jax
import jax.numpy as jnp
from jax import lax
from jax.experimental import pallas as pl
from jax.experimental.pallas import tpu as pltpu

F32 = jnp.float32
BF16 = jnp.bfloat16

D_MODEL = 2048
CHUNK = 64
CHUNK_SHIFT = CHUNK.bit_length() - 1
assert 1 << CHUNK_SHIFT == CHUNK
ROPE_THETA = 10000.0
EPS = 1e-6
NEG = -1e30

A_HEAD_DIM = 128
A_WIDTH = D_MODEL // 4
A_HEADS = A_WIDTH // A_HEAD_DIM
IDX_HEADS = 8
IDX_DIM = 64
TOPK_MAX = 256

S5_GROUP = 16
S5_WIDTH = D_MODEL // 4
S5_GROUPS = S5_WIDTH // S5_GROUP
S5_STATE = 64
S5_NSTATE = S5_GROUPS * S5_STATE

C_QK_DIM = 64
C_V_DIM = 2 * C_QK_DIM
C_WIDTH = D_MODEL // 2
C_HEADS = C_WIDTH // C_V_DIM

X_HEADS = 4
X_HEAD_DIM = D_MODEL // X_HEADS
D_FF = 4 * D_MODEL

LANES = 128
SUBLANES = 8
VMEM_LIMIT = 56 * 1024 * 1024

KIW_W = LANES
OFF_QA = 0
OFF_KA = OFF_QA + A_WIDTH
OFF_VA = OFF_KA + A_WIDTH
OFF_QI = OFF_VA + A_WIDTH
OFF_KIW = OFF_QI + IDX_HEADS * IDX_DIM
OFF_US = OFF_KIW + KIW_W
OFF_QC = OFF_US + S5_WIDTH
OFF_KC = OFF_QC + C_WIDTH
OFF_VC = OFF_KC + C_WIDTH
IN_PACKED = OFF_VC + C_WIDTH
IN_SRC_SPLIT = 3 * A_WIDTH + IDX_HEADS * IDX_DIM + IDX_DIM + IDX_HEADS

LOG2E = math.log2(math.e)
INT_MIN = -2 ** 31
INT_MAX = 2 ** 31 - 1


def _cparams(sem):
    return pltpu.CompilerParams(dimension_semantics=sem, vmem_limit_bytes=VMEM_LIMIT)


def _resident(shape, index_map):
    return pl.BlockSpec(shape, index_map, pipeline_mode=pl.Buffered(1))


def _rms(x, g):
    inv = lax.rsqrt(jnp.mean(x * x, axis=-1, keepdims=True) + EPS)
    return x * inv * g


def _dot(a, b):
    return jnp.dot(a, b, preferred_element_type=F32)


def _dot_nt(a, b):
    return lax.dot_general(a, b, (((1,), (1,)), ((), ())), preferred_element_type=F32)


def _inproj_body(x_ref, g_ref, w_ref, c128_ref, s128_ref, c64_ref, sa64_ref, sb64_ref,
                 c64k_ref, sa64k_ref, sb64k_ref,
                 qa_ref, ka_ref, va_ref, qi_ref, kiw_ref, us_ref, qc_ref, kc_ref, vc_ref):
    hn = _rms(x_ref[...], g_ref[...]).astype(BF16)
    c128, s128 = c128_ref[...], s128_ref[...]
    c64, sa64, sb64 = c64_ref[...], sa64_ref[...], sb64_ref[...]

    def rope128(t):
        return t * c128 + pltpu.roll(t, 64, 1) * s128

    def rope64(t):
        return t * c64 + pltpu.roll(t, 96, 1) * sa64 + pltpu.roll(t, 32, 1) * sb64

    def rope64_kiw(t):
        return t * c64k_ref[...] + pltpu.roll(t, 96, 1) * sa64k_ref[...] + pltpu.roll(t, 32, 1) * sb64k_ref[...]

    def emit(off, width, out_ref, fn, scale=None):
        step = 512 if width >= 512 else width
        for c0 in range(0, width, step):
            t = _dot(hn, w_ref[:, off + c0:off + c0 + step])
            for s0 in range(0, step, LANES):
                v = t[:, s0:s0 + LANES]
                if fn is not None:
                    v = fn(v)
                if scale is not None:
                    v = v * scale
                out_ref[:, c0 + s0:c0 + s0 + LANES] = v.astype(out_ref.dtype)

    emit(OFF_QA, A_WIDTH, qa_ref, rope128, A_HEAD_DIM ** -0.5 * LOG2E)
    emit(OFF_KA, A_WIDTH, ka_ref, rope128)
    emit(OFF_VA, A_WIDTH, va_ref, None)
    emit(OFF_QI, IDX_HEADS * IDX_DIM, qi_ref, rope64)
    emit(OFF_KIW, KIW_W, kiw_ref, rope64_kiw)
    emit(OFF_US, S5_WIDTH, us_ref, None)
    emit(OFF_QC, C_WIDTH, qc_ref, rope64, C_QK_DIM ** -0.5 * LOG2E)
    emit(OFF_KC, C_WIDTH, kc_ref, rope64)
    emit(OFF_VC, C_WIDTH, vc_ref, None)


def _in_proj(x, g, w_packed, layer, tabs, tm):
    L = x.shape[0]
    row = lambda i: (i, 0)
    tab_spec = pl.BlockSpec((tm, LANES), row)
    outs = [
        (A_WIDTH, BF16), (A_WIDTH, BF16), (A_WIDTH, BF16), (IDX_HEADS * IDX_DIM, BF16),
        (KIW_W, F32), (S5_WIDTH, F32), (C_WIDTH, BF16), (C_WIDTH, BF16), (C_WIDTH, BF16),
    ]
    return pl.pallas_call(
        _inproj_body,
        grid=(L // tm,),
        in_specs=[
            pl.BlockSpec((tm, D_MODEL), row),
            _resident((None, 1, D_MODEL), lambda i: (layer, 0, 0)),
            _resident((None, D_MODEL, IN_PACKED), lambda i: (layer, 0, 0)),
        ] + [tab_spec] * 8 + [
        ],
        out_specs=[pl.BlockSpec((tm, w), row) for w, _ in outs],
        out_shape=[jax.ShapeDtypeStruct((L, w), dt) for w, dt in outs],
        compiler_params=_cparams(("parallel",)),
        name="in_proj",
    )(x, g, w_packed, *tabs)


def _rope_tables(L):
    pos = jnp.arange(L, dtype=F32)[:, None]

    def cs(dim):
        inv = ROPE_THETA ** (-jnp.arange(0, dim, 2, dtype=F32) / dim)
        ang = pos * inv[None, :]
        return jnp.cos(ang), jnp.sin(ang)

    c, s = cs(A_HEAD_DIM)
    c128 = jnp.concatenate([c, c], axis=1)
    s128 = jnp.concatenate([-s, s], axis=1)
    c, s = cs(IDX_DIM)
    z = jnp.zeros_like(s)
    c64 = jnp.tile(jnp.concatenate([c, c], axis=1), (1, 2))
    sa64 = jnp.tile(jnp.concatenate([-s, z], axis=1), (1, 2))
    sb64 = jnp.tile(jnp.concatenate([z, s], axis=1), (1, 2))
    lane = jnp.arange(LANES)[None, :]
    c64k = jnp.where(lane < IDX_DIM, c64, 1.0)
    sa64k = jnp.where(lane < IDX_DIM, sa64, 0.0)
    sb64k = jnp.where(lane < IDX_DIM, sb64, 0.0)
    return (c128, s128, c64, sa64, sb64), (c64k, sa64k, sb64k)


def _sortable(v):
    k = lax.bitcast_convert_type(v, jnp.int32)
    return k ^ ((k >> 31) & jnp.int32(INT_MAX))


COUNT_ROWS = 32
WRITE_TILES_PER_STEP = 4
COARSE_BITS = 16
COARSE_SPAN = 1 << COARSE_BITS


def _dsa_index_body(qit_ref, wt_ref, ke_ref, ko_ref, bias_ref, skey_ref, coarse_ref, gmax_ref,
                    *, tq, top_k, n_tiles):
    i = pl.program_id(0)
    score_tiles_per_step = 2 if n_tiles % 2 == 0 else 1
    n_live = i + 1
    q_pos = i * tq + lax.broadcasted_iota(jnp.int32, (tq, tq), 1)
    k_off = lax.broadcasted_iota(jnp.int32, (tq, tq), 0)
    wv = wt_ref[...] * ((IDX_HEADS ** -0.5) * (IDX_DIM ** -0.5))
    gmax_ref[...] = jnp.full((tq, tq), INT_MIN, jnp.int32)

    def score_tiles(step, _):
        for jt in [step * score_tiles_per_step + u for u in range(score_tiles_per_step)]:
            rows = pl.ds(pl.multiple_of(jt * tq, tq), tq)
            ke = ke_ref[rows, :]
            ko = ko_ref[rows, :]
            acc = jnp.zeros((tq, tq), F32)
            for hp in range(IDX_HEADS // 2):
                q_pair = qit_ref[hp * LANES:(hp + 1) * LANES, :]
                d0 = _dot(ke, q_pair)
                d1 = _dot(ko, q_pair)
                acc = (acc + wv[2 * hp:2 * hp + 1, :] * jnp.maximum(d0, 0.0)
                       + wv[2 * hp + 1:2 * hp + 2, :] * jnp.maximum(d1, 0.0))
            allowed = ((jt * tq + k_off) >> CHUNK_SHIFT) <= (q_pos >> CHUNK_SHIFT)
            bits = lax.bitcast_convert_type(jnp.where(allowed, acc, NEG), jnp.int32)
            key = bits ^ ((bits >> 31) & jnp.int32(INT_MAX))
            skey_ref[jt] = key
            coarse_ref[jt] = lax.bitcast_convert_type(bits & jnp.int32(-COARSE_SPAN), F32).astype(BF16)
            gmax_ref[...] = jnp.maximum(gmax_ref[...], key)
        return 0

    lax.fori_loop(0, (n_live + score_tiles_per_step - 1) // score_tiles_per_step, score_tiles, 0)

    def count(pred):
        def body(jt, acc):
            for r in range(0, tq, COUNT_ROWS):
                acc = acc + jnp.where(pred(skey_ref[jt, r:r + COUNT_ROWS, :]), 1.0, 0.0)
            return acc
        acc = lax.fori_loop(0, n_live, body, jnp.zeros((COUNT_ROWS, tq), F32))
        return jnp.sum(acc, axis=0, keepdims=True)

    kf = float(top_k)

    gmax = gmax_ref[...]
    lo0 = jnp.min(gmax, axis=0, keepdims=True)
    hi0 = jnp.max(gmax, axis=0, keepdims=True)

    def narrowing(carry):
        lo, hi = carry
        return jnp.max(jnp.where(hi != lo, 1.0, 0.0)) > 0.0

    def bisect(carry):
        lo, hi = carry
        mid = (lo >> 1) + (hi >> 1) + (((lo & 1) + (hi & 1) + 1) >> 1)
        cnt = count(lambda t: t >= mid)
        ok = cnt >= kf
        hi_new = jnp.where(cnt == kf, mid, jnp.where(ok, hi, mid - 1))
        return jnp.where(ok, mid, lo), hi_new

    at_zero = count(lambda t: t >= 0) >= kf
    above_zero = count(lambda t: t >= 1) >= kf
    lo1 = jnp.where(above_zero, jnp.maximum(lo0, 1), jnp.where(at_zero, 0, lo0))
    hi1 = jnp.where(above_zero, hi0, jnp.where(at_zero, 0, jnp.minimum(hi0, -1)))

    def coarse_count(cand):
        cb = jnp.broadcast_to(cand, (COUNT_ROWS, tq))
        one, zero = jnp.ones((COUNT_ROWS, tq), cb.dtype), jnp.zeros((COUNT_ROWS, tq), cb.dtype)

        def body(jt, acc):
            part = zero
            for r in range(0, tq, COUNT_ROWS):
                part = part + jnp.where(coarse_ref[jt, r:r + COUNT_ROWS, :] >= cb, one, zero)
            return acc + part.astype(F32)
        acc = lax.fori_loop(0, n_live, body, jnp.zeros((COUNT_ROWS, tq), F32))
        return jnp.sum(acc, axis=0, keepdims=True)

    def coarse_value(p):
        k = (p << COARSE_BITS) | jnp.where(p < 0, COARSE_SPAN - 1, 0)
        return lax.bitcast_convert_type(k ^ ((k >> 31) & jnp.int32(INT_MAX)), F32).astype(BF16)

    def coarse_bisect(carry):
        lo, hi, done = carry
        mid = (lo >> 1) + (hi >> 1) + (((lo & 1) + (hi & 1) + 1) >> 1)
        cnt = coarse_count(coarse_value(mid))
        ok = cnt >= kf
        stop = (cnt == kf) & (mid != 0)
        return (jnp.where(ok, mid, lo), jnp.where(stop, mid, jnp.where(ok, hi, mid - 1)),
                jnp.where(stop, 1, done))

    c16, _, done = lax.while_loop(lambda c: narrowing(c[:2]), coarse_bisect,
                                  (lo1 >> COARSE_BITS, hi1 >> COARSE_BITS, jnp.zeros((1, tq), jnp.int32)))
    base = c16 << COARSE_BITS
    lo2 = jnp.maximum(jnp.where(c16 == 0, -COARSE_SPAN, base), lo1)
    hi2 = jnp.where(done == 1, lo2, jnp.minimum(base + (COARSE_SPAN - 1), hi1))
    lo, _ = lax.while_loop(narrowing, bisect, (lo2, hi2))

    need = kf - count(lambda t: t > lo)
    tri = jnp.where(k_off >= lax.broadcasted_iota(jnp.int32, (tq, tq), 1), 1.0, 0.0).astype(BF16)

    def write_tile(jt, seen, on_diagonal):
        t = skey_ref[jt]
        tied = t == lo
        rank = seen + _dot(tri, jnp.where(tied, 1.0, 0.0).astype(BF16))
        sel = (t > lo) | (tied & (rank <= need))
        if on_diagonal:
            sel = sel & (((jt * tq + k_off) >> CHUNK_SHIFT) <= (q_pos >> CHUNK_SHIFT))
        bias_ref[jt] = jnp.where(sel, 0.0, NEG).astype(BF16)
        return rank[tq - 1:tq, :]

    def write_group(step, seen):
        for u in range(WRITE_TILES_PER_STEP):
            seen = write_tile(step * WRITE_TILES_PER_STEP + u, seen, False)
        return seen

    n_groups = i // WRITE_TILES_PER_STEP
    seen = lax.fori_loop(0, n_groups, write_group, jnp.zeros((1, tq), F32))
    seen = lax.fori_loop(n_groups * WRITE_TILES_PER_STEP, i, lambda jt, s: write_tile(jt, s, False), seen)
    write_tile(i, seen, True)

    def blank_tile(jt, _):
        bias_ref[jt] = jnp.full((tq, tq), NEG, BF16)
        return 0

    lax.fori_loop(n_live, n_tiles, blank_tile, 0)


def _dsa_index(qi_t, w_t, ke, ko, tq):
    L = qi_t.shape[1]
    n_tiles = L // tq
    top_k = min(TOPK_MAX, L // 4)
    assert top_k <= tq
    body = functools.partial(_dsa_index_body, tq=tq, top_k=top_k, n_tiles=n_tiles)
    return pl.pallas_call(
        body,
        grid=(n_tiles,),
        in_specs=[
            pl.BlockSpec((IDX_HEADS * IDX_DIM, tq), lambda i: (0, i)),
            pl.BlockSpec((IDX_HEADS, tq), lambda i: (0, i)),
            _resident((L, LANES), lambda i: (0, 0)),
            _resident((L, LANES), lambda i: (0, 0)),
        ],
        out_specs=pl.BlockSpec((None, n_tiles, tq, tq), lambda i: (i, 0, 0, 0)),
        out_shape=jax.ShapeDtypeStruct((n_tiles, n_tiles, tq, tq), BF16),
        scratch_shapes=[pltpu.VMEM((n_tiles, tq, tq), jnp.int32), pltpu.VMEM((n_tiles, tq, tq), BF16),
                        pltpu.VMEM((tq, tq), jnp.int32)],
        compiler_params=_cparams(("parallel",)),
        name="dsa_index",
    )(qi_t, w_t, ke, ko)


FLASH_TILE = 256
ONES_ROWS = 16


def _values_t(v, heads):
    L = v.shape[0]
    vt = jnp.transpose(v).reshape(heads, -1, L)
    return jnp.concatenate([vt, jnp.ones((heads, ONES_ROWS, L), v.dtype)], axis=1)


class _Flash:
    def __init__(self, s_ref, m_ref, acc_ref):
        self.s_ref, self.m_ref, self.acc_ref = s_ref, m_ref, acc_ref
        m_ref[...] = jnp.full(m_ref.shape, NEG, F32)
        acc_ref[...] = jnp.zeros(acc_ref.shape, F32)

    def step(self, cur, nxt):
        for c, n in zip(cur, nxt):
            if c is not None:
                buf, slot, vt = c
                s = self.s_ref[buf]
                m_prev = self.m_ref[slot]
                m_new = jnp.maximum(m_prev, jnp.max(s, axis=0, keepdims=True))
                alpha = jnp.exp2(m_prev - m_new)
                p = jnp.exp2(s - m_new).astype(BF16)
                self.m_ref[slot] = m_new
            if n is not None:
                buf_n, kt, qt, add, mask = n
                s_n = _dot(kt, qt)
                if add is not None:
                    s_n = s_n + add
                if mask is not None:
                    s_n = jnp.where(mask, s_n, NEG)
                self.s_ref[buf_n] = s_n
            if c is not None:
                self.acc_ref[slot] = alpha * self.acc_ref[slot] + _dot(vt, p)


def _dsa_attn_body(qt_ref, k_ref, vt_ref, bias_ref, o_ref, s_ref, m_ref, acc_ref, *, tq, tiles_per_step):
    i = pl.program_id(0)
    flash = _Flash(s_ref, m_ref, acc_ref)
    tk = FLASH_TILE
    dv = A_HEAD_DIM
    heads = [slice(h * dv, (h + 1) * dv) for h in range(A_HEADS)]

    def key_rows(step, u):
        return pl.ds(pl.multiple_of((step * tiles_per_step + u) * tk, tk), tk)

    def cur_side(step):
        return [(u * A_HEADS + h, h, vt_ref[h, :, key_rows(step, u)])
                for u in range(tiles_per_step) for h in range(A_HEADS)]

    def nxt_side(step):
        out = []
        for u in range(tiles_per_step):
            rows = key_rows(step, u)
            b = bias_ref[rows, :].astype(F32)
            out += [(u * A_HEADS + h, k_ref[rows, hs], qt_ref[hs, :], b, None) for h, hs in enumerate(heads)]
        return out

    idle = [None] * (tiles_per_step * A_HEADS)
    n_live = ((i + 1) * tq + tk - 1) // tk
    n_steps = (n_live + tiles_per_step - 1) // tiles_per_step
    flash.step(idle, nxt_side(0))

    def body(step, _):
        flash.step(cur_side(step), nxt_side(step + 1))
        return 0

    lax.fori_loop(0, n_steps - 1, body, 0)
    flash.step(cur_side(n_steps - 1), idle)
    for h, hs in enumerate(heads):
        o_ref[:, hs] = jnp.transpose(acc_ref[h, :dv, :] / acc_ref[h, dv:dv + 1, :]).astype(o_ref.dtype)


def _dsa_attn(q_t, k, v_t, bias, tq):
    L = k.shape[0]
    n_tiles = L // tq
    tiles_per_step = 2 if (L // FLASH_TILE) % 2 == 0 else 1
    return pl.pallas_call(
        functools.partial(_dsa_attn_body, tq=tq, tiles_per_step=tiles_per_step),
        grid=(n_tiles,),
        in_specs=[
            pl.BlockSpec((A_WIDTH, tq), lambda i: (0, i)),
            _resident((L, A_WIDTH), lambda i: (0, 0)),
            _resident((A_HEADS, A_HEAD_DIM + ONES_ROWS, L), lambda i: (0, 0, 0)),
            pl.BlockSpec((None, L, tq), lambda i: (i, 0, 0)),
        ],
        out_specs=pl.BlockSpec((tq, A_WIDTH), lambda i: (i, 0)),
        out_shape=jax.ShapeDtypeStruct((L, A_WIDTH), BF16),
        scratch_shapes=[
            pltpu.VMEM((tiles_per_step * A_HEADS, FLASH_TILE, tq), F32),
            pltpu.VMEM((A_HEADS, 1, tq), F32),
            pltpu.VMEM((A_HEADS, A_HEAD_DIM + ONES_ROWS, tq), F32),
        ],
        compiler_params=_cparams(("parallel",)),
        name="dsa_attn",
    )(q_t, k, v_t, bias)


def _diff_attn_body(lq1_ref, lk1_ref, lq2_ref, lk2_ref, g_ref, qt_ref, k_ref, vt_ref, o_ref,
                    s_ref, m_ref, acc_ref, *, tq, lam_init):
    i = pl.program_id(1)
    flash = _Flash(s_ref, m_ref, acc_ref)
    dv = C_V_DIM
    tk = FLASH_TILE
    n_chunks = tq // FLASH_TILE
    n_full = i * n_chunks
    qt = qt_ref[...]
    row = lax.broadcasted_iota(jnp.int32, qt.shape, 0)
    zero = jnp.zeros_like(qt)
    qts = (jnp.where(row < C_QK_DIM, qt, zero), jnp.where(row >= C_QK_DIM, qt, zero))
    kk = lax.broadcasted_iota(jnp.int32, (tk, FLASH_TILE), 0)
    qq = lax.broadcasted_iota(jnp.int32, (tk, FLASH_TILE), 1)
    streams = [(c * n_chunks + qc, qc, qts[c][:, qc * FLASH_TILE:(qc + 1) * FLASH_TILE])
               for qc in range(n_chunks) for c in range(2)]

    def tile_start(jt):
        return pl.multiple_of(jt * tk, tk)

    def cur_side(jt, first_chunk=0):
        vt = vt_ref[:, pl.ds(tile_start(jt), tk)]
        return [(slot, slot, vt) if qc >= first_chunk else None for slot, qc, _ in streams]

    def nxt_side(jt, first_chunk=0, mask_of=lambda qc: None):
        kt = k_ref[pl.ds(tile_start(jt), tk), :]
        return [(slot, kt, q, None, mask_of(qc)) if qc >= first_chunk else None for slot, qc, q in streams]

    idle = [None] * len(streams)

    def first_mask(qc):
        q_pos = (i * n_chunks + qc) * FLASH_TILE + qq
        return (kk >> CHUNK_SHIFT) <= (q_pos >> CHUNK_SHIFT)

    flash.step(idle, nxt_side(0, mask_of=first_mask))

    def body(jt, _):
        flash.step(cur_side(jt), nxt_side(jt + 1))
        return 0

    lax.fori_loop(0, jnp.maximum(n_full - 1, 0), body, 0)

    diag = lambda d: (lambda qc: ((kk >> CHUNK_SHIFT) <= (qq >> CHUNK_SHIFT)) if qc == d else None)

    @pl.when(i > 0)
    def _():
        flash.step(cur_side(n_full - 1), nxt_side(n_full, mask_of=diag(0)))

    for d in range(1, n_chunks):
        flash.step(cur_side(n_full + d - 1, d - 1), nxt_side(n_full + d, d, diag(d)))
    flash.step(cur_side(n_full + n_chunks - 1, n_chunks - 1), idle)

    lam = (jnp.exp(jnp.sum(lq1_ref[...] * lk1_ref[...], axis=1, keepdims=True))
           - jnp.exp(jnp.sum(lq2_ref[...] * lk2_ref[...], axis=1, keepdims=True)) + lam_init)
    for qc in range(n_chunks):
        s0, s1 = qc, n_chunks + qc
        o = (acc_ref[s0, :dv, :] / acc_ref[s0, dv:dv + 1, :]
             - lam * (acc_ref[s1, :dv, :] / acc_ref[s1, dv:dv + 1, :]))
        inv = lax.rsqrt(jnp.mean(o * o, axis=0, keepdims=True) + EPS)
        o_ref[qc * FLASH_TILE:(qc + 1) * FLASH_TILE, :] = (
            jnp.transpose(o * inv) * g_ref[...] * (1.0 - lam_init)).astype(o_ref.dtype)


def _diff_attn(q_t, k, v_t, lam_vecs, sub_gain, layer, lam_init, tq):
    L = k.shape[0]
    vec = lambda w: _resident((None, 1, w), lambda h, i: (layer, 0, 0))
    return pl.pallas_call(
        functools.partial(_diff_attn_body, tq=tq, lam_init=lam_init),
        grid=(C_HEADS, L // tq),
        in_specs=[
            vec(C_QK_DIM), vec(C_QK_DIM), vec(C_QK_DIM), vec(C_QK_DIM), vec(C_V_DIM),
            pl.BlockSpec((C_V_DIM, tq), lambda h, i: (h, i)),
            pl.BlockSpec((L, C_V_DIM), lambda h, i: (0, h)),
            pl.BlockSpec((None, C_V_DIM + ONES_ROWS, L), lambda h, i: (h, 0, 0)),
        ],
        out_specs=pl.BlockSpec((tq, C_V_DIM), lambda h, i: (i, h)),
        out_shape=jax.ShapeDtypeStruct((L, C_WIDTH), BF16),
        scratch_shapes=[
            pltpu.VMEM((2 * (tq // FLASH_TILE), FLASH_TILE, FLASH_TILE), F32),
            pltpu.VMEM((2 * (tq // FLASH_TILE), 1, FLASH_TILE), F32),
            pltpu.VMEM((2 * (tq // FLASH_TILE), C_V_DIM + ONES_ROWS, FLASH_TILE), F32),
        ],
        compiler_params=_cparams(("parallel", "parallel")),
        name="diff_attn",
    )(*lam_vecs, sub_gain, q_t, k, v_t)


def _s5_prep_body(are_ref, aim_ref, ldt_ref, bre_ref, bim_ref, bb_ref, lvl_ref, pw_ref):
    a_re, a_im = are_ref[...], aim_ref[...]
    dt = jnp.exp(ldt_ref[...])
    mag = jnp.exp(a_re * dt)
    lb_re = mag * jnp.cos(a_im * dt)
    lb_im = mag * jnp.sin(a_im * dt)
    den = a_re * a_re + a_im * a_im
    f_re = ((lb_re - 1.0) * a_re + lb_im * a_im) / den
    f_im = (lb_im * a_re - (lb_re - 1.0) * a_im) / den
    b_re, b_im = bre_ref[...], bim_ref[...]
    bb_ref[:, :S5_NSTATE] = (f_re * b_re - f_im * b_im).astype(BF16)
    bb_ref[:, S5_NSTATE:] = (f_re * b_im + f_im * b_re).astype(BF16)

    pows = [(lb_re, lb_im)]
    for _ in range(SUBLANES - 1):
        pr, pi = pows[-1]
        pows.append((pr * lb_re - pi * lb_im, pr * lb_im + pi * lb_re))
    row = lax.broadcasted_iota(jnp.int32, (SUBLANES, S5_NSTATE), 0)
    zero = jnp.zeros((SUBLANES, S5_NSTATE), F32)
    for lvl in range(3):
        sh = 1 << lvl
        pr, pi = pows[sh - 1]
        lvl_ref[lvl, 0] = jnp.where(row >= sh, pr, zero)
        lvl_ref[lvl, 1] = jnp.where(row >= sh, pi, zero)
    cr, ci = zero, zero
    for r in range(SUBLANES):
        cr = jnp.where(row == r, pows[r][0], cr)
        ci = jnp.where(row == r, pows[r][1], ci)
    pw_ref[0] = cr
    pw_ref[1] = ci


def _s5_prep(a_re, a_im, log_dt, b_re_bd, b_im_bd):
    depth = a_re.shape[0]
    vec = pl.BlockSpec((None, 1, S5_NSTATE), lambda l: (l, 0, 0))
    mat = pl.BlockSpec((None, S5_WIDTH, S5_NSTATE), lambda l: (l, 0, 0))
    return pl.pallas_call(
        _s5_prep_body,
        grid=(depth,),
        in_specs=[vec, vec, vec, mat, mat],
        out_specs=[
            pl.BlockSpec((None, S5_WIDTH, 2 * S5_NSTATE), lambda l: (l, 0, 0)),
            pl.BlockSpec((None, 3, 2, SUBLANES, S5_NSTATE), lambda l: (l, 0, 0, 0, 0)),
            pl.BlockSpec((None, 2, SUBLANES, S5_NSTATE), lambda l: (l, 0, 0, 0)),
        ],
        out_shape=[
            jax.ShapeDtypeStruct((depth, S5_WIDTH, 2 * S5_NSTATE), BF16),
            jax.ShapeDtypeStruct((depth, 3, 2, SUBLANES, S5_NSTATE), F32),
            jax.ShapeDtypeStruct((depth, 2, SUBLANES, S5_NSTATE), F32),
        ],
        compiler_params=_cparams(("parallel",)),
        name="s5_prep",
    )(a_re, a_im, log_dt, b_re_bd, b_im_bd)


S5_LANE_CHUNK = 512


def _s5_body(u_ref, bb_ref, lvl_ref, pw_ref, cre_ref, cim_ref, d_ref, wg_ref, bg_ref, o_ref,
             x_ref, carry_ref, *, tl):
    @pl.when(pl.program_id(0) == 0)
    def _():
        carry_ref[...] = jnp.zeros(carry_ref.shape, F32)

    u = u_ref[...]
    x_ref[...] = _dot(u.astype(BF16), bb_ref[...])

    for c0 in range(0, S5_NSTATE, S5_LANE_CHUNK):
        re_sl = slice(c0, c0 + S5_LANE_CHUNK)
        im_sl = slice(S5_NSTATE + c0, S5_NSTATE + c0 + S5_LANE_CHUNK)

        def block(t, carry, re_sl=re_sl, im_sl=im_sl):
            c_re, c_im = carry
            rows = pl.ds(pl.multiple_of(t * SUBLANES, SUBLANES), SUBLANES)
            re = x_ref[rows, re_sl]
            im = x_ref[rows, im_sl]
            for lvl in range(3):
                sh = 1 << lvl
                s_re = pltpu.roll(re, sh, 0)
                s_im = pltpu.roll(im, sh, 0)
                a_re = lvl_ref[lvl, 0, :, re_sl]
                a_im = lvl_ref[lvl, 1, :, re_sl]
                re, im = re + (a_re * s_re - a_im * s_im), im + (a_re * s_im + a_im * s_re)
            p_re = pw_ref[0, :, re_sl]
            p_im = pw_ref[1, :, re_sl]
            re, im = re + (p_re * c_re - p_im * c_im), im + (p_re * c_im + p_im * c_re)
            x_ref[rows, re_sl] = re
            x_ref[rows, im_sl] = im
            last = SUBLANES - 1
            return (jnp.broadcast_to(re[last:, :], re.shape), jnp.broadcast_to(im[last:, :], im.shape))

        carry = lax.fori_loop(0, tl // SUBLANES, block,
                              (carry_ref[0, :, re_sl], carry_ref[1, :, re_sl]))
        carry_ref[0, :, re_sl] = carry[0]
        carry_ref[1, :, re_sl] = carry[1]

    y = (_dot(x_ref[:, :S5_NSTATE].astype(BF16), cre_ref[...])
         - _dot(x_ref[:, S5_NSTATE:].astype(BF16), cim_ref[...])
         + d_ref[...] * u)
    y = 0.5 * y * (1.0 + jnp.tanh(math.sqrt(2.0 / math.pi) * (y + 0.044715 * (y * y * y))))
    z = _dot(y.astype(BF16), wg_ref[...]) + bg_ref[...]
    o_ref[...] = (y * (1.0 / (1.0 + jnp.exp(-z)))).astype(o_ref.dtype)


def _s5_mix(u, bb, lvl, pw, c_re_bd, c_im_bd, d_row, w_glu, b_glu, layer, tl):
    L = u.shape[0]
    lay = lambda *rest: (lambda t: (layer,) + rest)
    return pl.pallas_call(
        functools.partial(_s5_body, tl=tl),
        grid=(L // tl,),
        in_specs=[
            pl.BlockSpec((tl, S5_WIDTH), lambda t: (t, 0)),
            _resident((None, S5_WIDTH, 2 * S5_NSTATE), lay(0, 0)),
            _resident((None, 3, 2, SUBLANES, S5_NSTATE), lay(0, 0, 0, 0)),
            _resident((None, 2, SUBLANES, S5_NSTATE), lay(0, 0, 0)),
            _resident((None, S5_NSTATE, S5_WIDTH), lay(0, 0)),
            _resident((None, S5_NSTATE, S5_WIDTH), lay(0, 0)),
            _resident((None, 1, S5_WIDTH), lay(0, 0)),
            _resident((None, S5_WIDTH, S5_WIDTH), lay(0, 0)),
            _resident((None, 1, S5_WIDTH), lay(0, 0)),
        ],
        out_specs=pl.BlockSpec((tl, S5_WIDTH), lambda t: (t, 0)),
        out_shape=jax.ShapeDtypeStruct((L, S5_WIDTH), BF16),
        scratch_shapes=[
            pltpu.VMEM((tl, 2 * S5_NSTATE), F32),
            pltpu.VMEM((2, SUBLANES, S5_NSTATE), F32),
        ],
        compiler_params=_cparams(("arbitrary",)),
        name="s5_mix",
    )(u, bb, lvl, pw, c_re_bd, c_im_bd, d_row, w_glu, b_glu)


def _outproj_body(x_ref, oa_ref, os_ref, oc_ref, w_ref, o_ref):
    a0, a1, a2 = A_WIDTH, A_WIDTH + S5_WIDTH, A_WIDTH + S5_WIDTH + C_WIDTH
    o_ref[...] = (x_ref[...] + _dot(oa_ref[...], w_ref[:a0, :]) + _dot(os_ref[...], w_ref[a0:a1, :])
                  + _dot(oc_ref[...], w_ref[a1:a2, :]))


def _out_proj(x, o_a, o_s, o_c, w_out, layer, tm):
    L = x.shape[0]
    row = lambda i: (i, 0)
    return pl.pallas_call(
        _outproj_body,
        grid=(L // tm,),
        in_specs=[
            pl.BlockSpec((tm, D_MODEL), row),
            pl.BlockSpec((tm, A_WIDTH), row),
            pl.BlockSpec((tm, S5_WIDTH), row),
            pl.BlockSpec((tm, C_WIDTH), row),
            _resident((None, D_MODEL, D_MODEL), lambda i: (layer, 0, 0)),
        ],
        out_specs=pl.BlockSpec((tm, D_MODEL), row),
        out_shape=jax.ShapeDtypeStruct((L, D_MODEL), F32),
        compiler_params=_cparams(("parallel",)),
        name="out_proj",
    )(x, o_a, o_s, o_c, w_out)


def _norm_matmul_body(x_ref, g_ref, w_ref, o_ref):
    o_ref[...] = _dot(_rms(x_ref[...], g_ref[...]).astype(BF16), w_ref[...]).astype(o_ref.dtype)


def _norm_matmul(x, g, w, layer, tn):
    M, K = x.shape
    N = w.shape[-1]
    return pl.pallas_call(
        _norm_matmul_body,
        grid=(N // tn,),
        in_specs=[
            _resident((M, K), lambda j: (0, 0)),
            _resident((None, 1, K), lambda j: (layer, 0, 0)),
            pl.BlockSpec((None, K, tn), lambda j: (layer, 0, j)),
        ],
        out_specs=pl.BlockSpec((M, tn), lambda j: (0, j)),
        out_shape=jax.ShapeDtypeStruct((M, N), BF16),
        compiler_params=_cparams(("parallel",)),
        name="mem_proj",
    )(x, g, w)


def _xattn_body(x_ref, g_ref, wq_ref, k_ref, v_ref, wo_ref, o_ref):
    x = x_ref[...]
    q = _dot(_rms(x, g_ref[...]).astype(BF16), wq_ref[...]).astype(BF16)
    scale = X_HEAD_DIM ** -0.5
    out = x
    for h in range(X_HEADS):
        hs = slice(h * X_HEAD_DIM, (h + 1) * X_HEAD_DIM)
        s = _dot_nt(q[:, hs], k_ref[:, hs]) * scale
        p = jnp.exp(s - jnp.max(s, axis=1, keepdims=True))
        p = p / jnp.sum(p, axis=1, keepdims=True)
        o_h = _dot(p.astype(BF16), v_ref[:, hs]).astype(BF16)
        out = out + _dot(o_h, wo_ref[hs, :])
    o_ref[...] = out


def _cross_attn(x, g, wq, k_mem, v_mem, wo, layer, tm):
    L = x.shape[0]
    M = k_mem.shape[0]
    row = lambda i: (i, 0)
    return pl.pallas_call(
        _xattn_body,
        grid=(L // tm,),
        in_specs=[
            pl.BlockSpec((tm, D_MODEL), row),
            _resident((None, 1, D_MODEL), lambda i: (layer, 0, 0)),
            _resident((None, D_MODEL, D_MODEL), lambda i: (layer, 0, 0)),
            _resident((M, D_MODEL), lambda i: (0, 0)),
            _resident((M, D_MODEL), lambda i: (0, 0)),
            _resident((None, D_MODEL, D_MODEL), lambda i: (layer, 0, 0)),
        ],
        out_specs=pl.BlockSpec((tm, D_MODEL), row),
        out_shape=jax.ShapeDtypeStruct((L, D_MODEL), F32),
        compiler_params=_cparams(("parallel",)),
        name="cross_attn",
    )(x, g, wq, k_mem, v_mem, wo)


def _mlp_body(x_ref, g_ref, w1_ref, w2_ref, o_ref, hn_ref):
    @pl.when(pl.program_id(1) == 0)
    def _():
        x = x_ref[...]
        hn_ref[...] = _rms(x, g_ref[...]).astype(BF16)
        o_ref[...] = x

    a = jnp.maximum(_dot(hn_ref[...], w1_ref[...]), 0.0)
    o_ref[...] += _dot((a * a).astype(BF16), w2_ref[...])


def _mlp(x, g, w1, w2, layer, tm, tf):
    L = x.shape[0]
    return pl.pallas_call(
        _mlp_body,
        grid=(L // tm, D_FF // tf),
        in_specs=[
            pl.BlockSpec((tm, D_MODEL), lambda i, f: (i, 0)),
            _resident((None, 1, D_MODEL), lambda i, f: (layer, 0, 0)),
            pl.BlockSpec((None, D_MODEL, tf), lambda i, f: (layer, 0, f)),
            pl.BlockSpec((None, tf, D_MODEL), lambda i, f: (layer, f, 0)),
        ],
        out_specs=pl.BlockSpec((tm, D_MODEL), lambda i, f: (i, 0)),
        out_shape=jax.ShapeDtypeStruct((L, D_MODEL), F32),
        scratch_shapes=[pltpu.VMEM((tm, D_MODEL), BF16)],
        compiler_params=_cparams(("parallel", "arbitrary")),
        name="mlp",
    )(x, g, w1, w2)


def _final_norm_body(x_ref, g_ref, o_ref):
    o_ref[...] = _rms(x_ref[...], g_ref[...])


def _final_norm(x, g, tm):
    L = x.shape[0]
    return pl.pallas_call(
        _final_norm_body,
        grid=(L // tm,),
        in_specs=[pl.BlockSpec((tm, D_MODEL), lambda i: (i, 0)),
                  _resident((1, D_MODEL), lambda i: (0, 0))],
        out_specs=pl.BlockSpec((tm, D_MODEL), lambda i: (i, 0)),
        out_shape=jax.ShapeDtypeStruct((L, D_MODEL), F32),
        compiler_params=_cparams(("parallel",)),
        name="final_norm",
    )(x, g)


def _block_diag_in(b):
    eye = jnp.eye(S5_GROUPS, dtype=b.dtype)
    t = jnp.transpose(b, (0, 1, 3, 2))
    bd = t[:, :, :, None, :] * eye[None, :, None, :, None]
    return bd.reshape(b.shape[0], S5_WIDTH, S5_NSTATE)


def _block_diag_out(c):
    eye = jnp.eye(S5_GROUPS, dtype=c.dtype)
    t = jnp.transpose(c, (0, 1, 3, 2))
    bd = t[:, :, :, None, :] * eye[None, :, None, :, None]
    return bd.reshape(c.shape[0], S5_NSTATE, S5_WIDTH)


def _pick_tile(n, want):
    t = min(n, want)
    assert n % t == 0, (n, t)
    return t


def kernel(x, mem, norm_mix, w_in, s5_a_re, s5_a_im, s5_log_dt, s5_b_re, s5_b_im, s5_c_re, s5_c_im, s5_d, s5_w_glu, s5_b_glu, diff_lam_q1, diff_lam_k1, diff_lam_q2, diff_lam_k2, diff_subln, w_out, norm_xattn, norm_mem, xattn_q, xattn_k, xattn_v, xattn_o, norm_mlp, w_ff1, w_ff2, norm_final):
    B, L, _ = x.shape
    assert B == 1
    depth = w_in.shape[0]
    tq = _pick_tile(L, 256)

    pad = jnp.zeros((depth, D_MODEL, KIW_W - IDX_DIM - IDX_HEADS), BF16)
    w_in_p = jnp.concatenate([w_in[..., :IN_SRC_SPLIT].astype(BF16), pad, w_in[..., IN_SRC_SPLIT:].astype(BF16)],
                             axis=-1)
    w_out_b = w_out.astype(BF16)
    wq_b, wk_b, wv_b, wo_b = (w.astype(BF16) for w in (xattn_q, xattn_k, xattn_v, xattn_o))
    w1_b, w2_b = w_ff1.astype(BF16), w_ff2.astype(BF16)
    wg_b = s5_w_glu.astype(BF16)
    row3 = lambda a: a.reshape(depth, 1, -1)

    tabs, tabs_k = _rope_tables(L)

    rep = lambda a: row3(a.astype(F32))
    a_re, a_im = rep(s5_a_re), rep(s5_a_im)
    log_dt = row3(jnp.broadcast_to(s5_log_dt.astype(F32)[:, :, None], (depth, S5_GROUPS, S5_STATE)))
    bb, lvl, pw = _s5_prep(a_re, a_im, log_dt, _block_diag_in(s5_b_re.astype(F32)),
                           _block_diag_in(s5_b_im.astype(F32)))
    c_re_bd = _block_diag_out(s5_c_re).astype(BF16)
    c_im_bd = _block_diag_out(s5_c_im).astype(BF16)
    d_row = row3(s5_d.astype(F32))
    lam_vecs = tuple(row3(v.astype(F32)) for v in (diff_lam_q1, diff_lam_k1, diff_lam_q2, diff_lam_k2))

    xs = x[0]
    mem2 = mem[0]
    for l in range(depth):
        qa, ka, va, qi, kiw, us, qc, kc, vc = _in_proj(xs, row3(norm_mix), w_in_p, l, tabs + tabs_k, _pick_tile(L, 256))

        k_idx = kiw[:, :IDX_DIM].astype(BF16)
        zeros = jnp.zeros_like(k_idx)
        ke = jnp.concatenate([k_idx, zeros], axis=1)
        ko = jnp.concatenate([zeros, k_idx], axis=1)
        w_t = jnp.transpose(kiw[:, IDX_DIM:IDX_DIM + IDX_HEADS])
        bias = _dsa_index(jnp.transpose(qi), w_t, ke, ko, tq).reshape(L // tq, L, tq)
        o_a = _dsa_attn(jnp.transpose(qa), ka, _values_t(va, A_HEADS), bias, tq)

        o_s = _s5_mix(us, bb, lvl, pw, c_re_bd, c_im_bd, d_row, wg_b, row3(s5_b_glu.astype(F32)), l,
                      _pick_tile(L, 512))

        lam_init = 0.8 - 0.6 * math.exp(-0.3 * l)
        o_c = _diff_attn(jnp.transpose(qc), kc, _values_t(vc, C_HEADS), lam_vecs, row3(diff_subln.astype(F32)), l,
                         lam_init, _pick_tile(L, 2048))

        xs = _out_proj(xs, o_a, o_s, o_c, w_out_b, l, _pick_tile(L, 512))

        k_mem = _norm_matmul(mem2, row3(norm_mem), wk_b, l, 512)
        v_mem = _norm_matmul(mem2, row3(norm_mem), wv_b, l, 512)
        xs = _cross_attn(xs, row3(norm_xattn), wq_b, k_mem, v_mem, wo_b, l, _pick_tile(L, 512))

        xs = _mlp(xs, row3(norm_mlp), w1_b, w2_b, l, _pick_tile(L, 1024), 512)

    return _final_norm(xs, norm_final.reshape(1, -1), _pick_tile(L, 512))[None]
```

```python
import functools
import math

import jax
import jax.numpy as jnp
from jax import lax
from jax.experimental import pallas as pl
from jax.experimental.pallas import tpu as pltpu

F32 = jnp.float32
BF16 = jnp.bfloat16

D_MODEL = 2048
CHUNK = 64
CHUNK_SHIFT = CHUNK.bit_length() - 1
assert 1 << CHUNK_SHIFT == CHUNK
ROPE_THETA = 10000.0
EPS = 1e-6
NEG = -1e30

A_HEAD_DIM = 128
A_WIDTH = D_MODEL // 4
A_HEADS = A_WIDTH // A_HEAD_DIM
IDX_HEADS = 8
IDX_DIM = 64
TOPK_MAX = 256

S5_GROUP = 16
S5_WIDTH = D_MODEL // 4
S5_GROUPS = S5_WIDTH // S5_GROUP
S5_STATE = 64
S5_NSTATE = S5_GROUPS * S5_STATE

C_QK_DIM = 64
C_V_DIM = 2 * C_QK_DIM
C_WIDTH = D_MODEL // 2
C_HEADS = C_WIDTH // C_V_DIM

X_HEADS = 4
X_HEAD_DIM = D_MODEL // X_HEADS
D_FF = 4 * D_MODEL

LANES = 128
SUBLANES = 8
VMEM_LIMIT = 58 * 1024 * 1024

KIW_W = LANES
OFF_QA = 0
OFF_KA = OFF_QA + A_WIDTH
OFF_VA = OFF_KA + A_WIDTH
OFF_QI = OFF_VA + A_WIDTH
OFF_KIW = OFF_QI + IDX_HEADS * IDX_DIM
OFF_US = OFF_KIW + KIW_W
OFF_QC = OFF_US + S5_WIDTH
OFF_KC = OFF_QC + C_WIDTH
OFF_VC = OFF_KC + C_WIDTH
IN_PACKED = OFF_VC + C_WIDTH
IN_SRC_SPLIT = 3 * A_WIDTH + IDX_HEADS * IDX_DIM + IDX_DIM + IDX_HEADS

LOG2E = math.log2(math.e)
INT_MIN = -2 ** 31
INT_MAX = 2 ** 31 - 1


def _cparams(sem):
    return pltpu.CompilerParams(dimension_semantics=sem, vmem_limit_bytes=VMEM_LIMIT)


def _resident(shape, index_map):
    return pl.BlockSpec(shape, index_map, pipeline_mode=pl.Buffered(1))


def _rms(x, g):
    inv = lax.rsqrt(jnp.mean(x * x, axis=-1, keepdims=True) + EPS)
    return x * inv * g


def _dot(a, b):
    return jnp.dot(a, b, preferred_element_type=F32)


def _dot_nt(a, b):
    return lax.dot_general(a, b, (((1,), (1,)), ((), ())), preferred_element_type=F32)


def _inproj_body(x_ref, g_ref, w_ref, c128_ref, s128_ref, c64_ref, sa64_ref, sb64_ref,
                 c64k_ref, sa64k_ref, sb64k_ref,
                 qa_ref, ka_ref, va_ref, qi_ref, ke_ref, ko_ref, wi_ref, us_ref, qc_ref, kc_ref, vc_ref):
    hn = _rms(x_ref[...], g_ref[...]).astype(BF16)
    c128, s128 = c128_ref[...], s128_ref[...]
    c64, sa64, sb64 = c64_ref[...], sa64_ref[...], sb64_ref[...]

    def rope128(t):
        return t * c128 + pltpu.roll(t, 64, 1) * s128

    def rope64(t):
        return t * c64 + pltpu.roll(t, 96, 1) * sa64 + pltpu.roll(t, 32, 1) * sb64

    def rope64_kiw(t):
        return t * c64k_ref[...] + pltpu.roll(t, 96, 1) * sa64k_ref[...] + pltpu.roll(t, 32, 1) * sb64k_ref[...]

    def emit(off, width, out_ref, fn, scale=None):
        step = 512 if width >= 512 else width
        for c0 in range(0, width, step):
            t = _dot(hn, w_ref[:, off + c0:off + c0 + step])
            for s0 in range(0, step, LANES):
                v = t[:, s0:s0 + LANES]
                if fn is not None:
                    v = fn(v)
                if scale is not None:
                    v = v * scale
                out_ref[:, c0 + s0:c0 + s0 + LANES] = v.astype(out_ref.dtype)

    emit(OFF_QA, A_WIDTH, qa_ref, rope128, A_HEAD_DIM ** -0.5 * LOG2E)
    emit(OFF_KA, A_WIDTH, ka_ref, rope128)
    emit(OFF_VA, A_WIDTH, va_ref, None)
    emit(OFF_QI, IDX_HEADS * IDX_DIM, qi_ref, rope64)
    kiw = rope64_kiw(_dot(hn, w_ref[:, OFF_KIW:OFF_KIW + KIW_W]))
    lane = lax.broadcasted_iota(jnp.int32, kiw.shape, 1)
    k_low = jnp.where(lane < IDX_DIM, kiw, 0.0)
    ke_ref[...] = k_low.astype(BF16)
    ko_ref[...] = pltpu.roll(k_low, IDX_DIM, 1).astype(BF16)
    wi_ref[...] = kiw
    emit(OFF_US, S5_WIDTH, us_ref, None)
    emit(OFF_QC, C_WIDTH, qc_ref, rope64, C_QK_DIM ** -0.5 * LOG2E)
    emit(OFF_KC, C_WIDTH, kc_ref, rope64)
    emit(OFF_VC, C_WIDTH, vc_ref, None)


def _in_proj(x, g, w_packed, layer, tabs, tm):
    L = x.shape[0]
    row = lambda i: (i, 0)
    tab_spec = pl.BlockSpec((tm, LANES), row)
    outs = [
        (A_WIDTH, BF16), (A_WIDTH, BF16), (A_WIDTH, BF16), (IDX_HEADS * IDX_DIM, BF16),
        (KIW_W, BF16), (KIW_W, BF16), (KIW_W, F32),
        (S5_WIDTH, F32), (C_WIDTH, BF16), (C_WIDTH, BF16), (C_WIDTH, BF16),
    ]
    return pl.pallas_call(
        _inproj_body,
        grid=(L // tm,),
        in_specs=[
            pl.BlockSpec((tm, D_MODEL), row),
            _resident((None, 1, D_MODEL), lambda i: (layer, 0, 0)),
            _resident((None, D_MODEL, IN_PACKED), lambda i: (layer, 0, 0)),
        ] + [tab_spec] * 8,
        out_specs=[pl.BlockSpec((tm, w), row) for w, _ in outs],
        out_shape=[jax.ShapeDtypeStruct((L, w), dt) for w, dt in outs],
        compiler_params=_cparams(("parallel",)),
        name="in_proj",
    )(x, g, w_packed, *tabs)


def _rope_tables(L):
    pos = jnp.arange(L, dtype=F32)[:, None]

    def cs(dim):
        inv = ROPE_THETA ** (-jnp.arange(0, dim, 2, dtype=F32) / dim)
        ang = pos * inv[None, :]
        return jnp.cos(ang), jnp.sin(ang)

    c, s = cs(A_HEAD_DIM)
    c128 = jnp.concatenate([c, c], axis=1)
    s128 = jnp.concatenate([-s, s], axis=1)
    c, s = cs(IDX_DIM)
    z = jnp.zeros_like(s)
    c64 = jnp.tile(jnp.concatenate([c, c], axis=1), (1, 2))
    sa64 = jnp.tile(jnp.concatenate([-s, z], axis=1), (1, 2))
    sb64 = jnp.tile(jnp.concatenate([z, s], axis=1), (1, 2))
    lane = jnp.arange(LANES)[None, :]
    c64k = jnp.where(lane < IDX_DIM, c64, 1.0)
    sa64k = jnp.where(lane < IDX_DIM, sa64, 0.0)
    sb64k = jnp.where(lane < IDX_DIM, sb64, 0.0)
    return (c128, s128, c64, sa64, sb64), (c64k, sa64k, sb64k)


def _sortable(v):
    k = lax.bitcast_convert_type(v, jnp.int32)
    return k ^ ((k >> 31) & jnp.int32(INT_MAX))


COUNT_ROWS = 32
WRITE_TILES_PER_STEP = 4
COARSE_BITS = 16
COARSE_SPAN = 1 << COARSE_BITS


def _dsa_index_body(qit_ref, wt_ref, ke_ref, ko_ref, bias_ref, skey_ref, coarse_ref, gmax_ref,
                    *, tq, top_k, n_tiles):
    i = pl.program_id(0)
    score_tiles_per_step = 2 if n_tiles % 2 == 0 else 1
    n_live = i + 1
    q_pos = i * tq + lax.broadcasted_iota(jnp.int32, (tq, tq), 1)
    k_off = lax.broadcasted_iota(jnp.int32, (tq, tq), 0)
    wv = wt_ref[...] * ((IDX_HEADS ** -0.5) * (IDX_DIM ** -0.5))
    gmax_ref[...] = jnp.full((tq, tq), INT_MIN, jnp.int32)

    def score_tiles(step, _):
        for jt in [step * score_tiles_per_step + u for u in range(score_tiles_per_step)]:
            rows = pl.ds(pl.multiple_of(jt * tq, tq), tq)
            ke = ke_ref[rows, :]
            ko = ko_ref[rows, :]
            acc = jnp.zeros((tq, tq), F32)
            for hp in range(IDX_HEADS // 2):
                q_pair = qit_ref[hp * LANES:(hp + 1) * LANES, :]
                d0 = _dot(ke, q_pair)
                d1 = _dot(ko, q_pair)
                acc = (acc + wv[2 * hp:2 * hp + 1, :] * jnp.maximum(d0, 0.0)
                       + wv[2 * hp + 1:2 * hp + 2, :] * jnp.maximum(d1, 0.0))
            allowed = ((jt * tq + k_off) >> CHUNK_SHIFT) <= (q_pos >> CHUNK_SHIFT)
            bits = lax.bitcast_convert_type(jnp.where(allowed, acc, NEG), jnp.int32)
            key = bits ^ ((bits >> 31) & jnp.int32(INT_MAX))
            skey_ref[jt] = key
            coarse_ref[jt] = lax.bitcast_convert_type(bits & jnp.int32(-COARSE_SPAN), F32).astype(BF16)
            gmax_ref[...] = jnp.maximum(gmax_ref[...], key)
        return 0

    lax.fori_loop(0, (n_live + score_tiles_per_step - 1) // score_tiles_per_step, score_tiles, 0)

    def count(pred):
        def body(jt, acc):
            for r in range(0, tq, COUNT_ROWS):
                acc = acc + jnp.where(pred(skey_ref[jt, r:r + COUNT_ROWS, :]), 1.0, 0.0)
            return acc
        acc = lax.fori_loop(0, n_live, body, jnp.zeros((COUNT_ROWS, tq), F32))
        return jnp.sum(acc, axis=0, keepdims=True)

    kf = float(top_k)

    gmax = gmax_ref[...]
    lo0 = jnp.min(gmax, axis=0, keepdims=True)
    hi0 = jnp.max(gmax, axis=0, keepdims=True)

    def narrowing(carry):
        lo, hi = carry
        return jnp.max(jnp.where(hi != lo, 1.0, 0.0)) > 0.0

    def bisect(carry):
        lo, hi = carry
        mid = (lo >> 1) + (hi >> 1) + (((lo & 1) + (hi & 1) + 1) >> 1)
        cnt = count(lambda t: t >= mid)
        ok = cnt >= kf
        hi_new = jnp.where(cnt == kf, mid, jnp.where(ok, hi, mid - 1))
        return jnp.where(ok, mid, lo), hi_new

    at_zero = count(lambda t: t >= 0) >= kf
    above_zero = count(lambda t: t >= 1) >= kf
    lo1 = jnp.where(above_zero, jnp.maximum(lo0, 1), jnp.where(at_zero, 0, lo0))
    hi1 = jnp.where(above_zero, hi0, jnp.where(at_zero, 0, jnp.minimum(hi0, -1)))

    def coarse_count(cand):
        cb = jnp.broadcast_to(cand, (COUNT_ROWS, tq))
        one, zero = jnp.ones((COUNT_ROWS, tq), cb.dtype), jnp.zeros((COUNT_ROWS, tq), cb.dtype)

        def body(jt, acc):
            part = zero
            for r in range(0, tq, COUNT_ROWS):
                part = part + jnp.where(coarse_ref[jt, r:r + COUNT_ROWS, :] >= cb, one, zero)
            return acc + part.astype(F32)
        acc = lax.fori_loop(0, n_live, body, jnp.zeros((COUNT_ROWS, tq), F32))
        return jnp.sum(acc, axis=0, keepdims=True)

    def coarse_value(p):
        k = (p << COARSE_BITS) | jnp.where(p < 0, COARSE_SPAN - 1, 0)
        return lax.bitcast_convert_type(k ^ ((k >> 31) & jnp.int32(INT_MAX)), F32).astype(BF16)

    def coarse_bisect(carry):
        lo, hi, done = carry
        mid = (lo >> 1) + (hi >> 1) + (((lo & 1) + (hi & 1) + 1) >> 1)
        cnt = coarse_count(coarse_value(mid))
        ok = cnt >= kf
        stop = (cnt == kf) & (mid != 0)
        return (jnp.where(ok, mid, lo), jnp.where(stop, mid, jnp.where(ok, hi, mid - 1)),
                jnp.where(stop, 1, done))

    c16, _, done = lax.while_loop(lambda c: narrowing(c[:2]), coarse_bisect,
                                  (lo1 >> COARSE_BITS, hi1 >> COARSE_BITS, jnp.zeros((1, tq), jnp.int32)))
    base = c16 << COARSE_BITS
    lo2 = jnp.maximum(jnp.where(c16 == 0, -COARSE_SPAN, base), lo1)
    hi2 = jnp.where(done == 1, lo2, jnp.minimum(base + (COARSE_SPAN - 1), hi1))
    lo, _ = lax.while_loop(narrowing, bisect, (lo2, hi2))

    need = kf - count(lambda t: t > lo)
    tri = jnp.where(k_off >= lax.broadcasted_iota(jnp.int32, (tq, tq), 1), 1.0, 0.0).astype(BF16)

    def write_tile(jt, seen, on_diagonal):
        t = skey_ref[jt]
        tied = t == lo
        rank = seen + _dot(tri, jnp.where(tied, 1.0, 0.0).astype(BF16))
        sel = (t > lo) | (tied & (rank <= need))
        if on_diagonal:
            sel = sel & (((jt * tq + k_off) >> CHUNK_SHIFT) <= (q_pos >> CHUNK_SHIFT))
        bias_ref[jt] = jnp.where(sel, 0.0, NEG).astype(BF16)
        return rank[tq - 1:tq, :]

    def write_group(step, seen):
        for u in range(WRITE_TILES_PER_STEP):
            seen = write_tile(step * WRITE_TILES_PER_STEP + u, seen, False)
        return seen

    n_groups = i // WRITE_TILES_PER_STEP
    seen = lax.fori_loop(0, n_groups, write_group, jnp.zeros((1, tq), F32))
    seen = lax.fori_loop(n_groups * WRITE_TILES_PER_STEP, i, lambda jt, s: write_tile(jt, s, False), seen)
    write_tile(i, seen, True)

    def blank_tile(jt, _):
        bias_ref[jt] = jnp.full((tq, tq), NEG, BF16)
        return 0

    lax.fori_loop(n_live, n_tiles, blank_tile, 0)


def _dsa_index(qi_t, w_t, ke, ko, tq):
    L = qi_t.shape[1]
    n_tiles = L // tq
    top_k = min(TOPK_MAX, L // 4)
    assert top_k <= tq
    body = functools.partial(_dsa_index_body, tq=tq, top_k=top_k, n_tiles=n_tiles)
    return pl.pallas_call(
        body,
        grid=(n_tiles,),
        in_specs=[
            pl.BlockSpec((IDX_HEADS * IDX_DIM, tq), lambda i: (0, i)),
            pl.BlockSpec((IDX_HEADS, tq), lambda i: (0, i)),
            _resident((L, LANES), lambda i: (0, 0)),
            _resident((L, LANES), lambda i: (0, 0)),
        ],
        out_specs=pl.BlockSpec((None, n_tiles, tq, tq), lambda i: (i, 0, 0, 0)),
        out_shape=jax.ShapeDtypeStruct((n_tiles, n_tiles, tq, tq), BF16),
        scratch_shapes=[pltpu.VMEM((n_tiles, tq, tq), jnp.int32), pltpu.VMEM((n_tiles, tq, tq), BF16),
                        pltpu.VMEM((tq, tq), jnp.int32)],
        compiler_params=_cparams(("parallel",)),
        name="dsa_index",
    )(qi_t, w_t, ke, ko)


FLASH_TILE = 256
ONES_ROWS = 16


def _values_t(v, heads):
    L = v.shape[0]
    vt = jnp.transpose(v).reshape(heads, -1, L)
    return jnp.concatenate([vt, jnp.ones((heads, ONES_ROWS, L), v.dtype)], axis=1)


class _Flash:
    def __init__(self, s_ref, m_ref, acc_ref):
        self.s_ref, self.m_ref, self.acc_ref = s_ref, m_ref, acc_ref
        m_ref[...] = jnp.full(m_ref.shape, NEG, F32)
        acc_ref[...] = jnp.zeros(acc_ref.shape, F32)

    def step(self, cur, nxt):
        for c, n in zip(cur, nxt):
            if c is not None:
                buf, slot, vt = c
                s = self.s_ref[buf]
                m_prev = self.m_ref[slot]
                m_new = jnp.maximum(m_prev, jnp.max(s, axis=0, keepdims=True))
                alpha = jnp.exp2(m_prev - m_new)
                p = jnp.exp2(s - m_new).astype(BF16)
                self.m_ref[slot] = m_new
            if n is not None:
                buf_n, kt, qt, add, mask = n
                s_n = _dot(kt, qt)
                if add is not None:
                    s_n = s_n + add
                if mask is not None:
                    s_n = jnp.where(mask, s_n, NEG)
                self.s_ref[buf_n] = s_n
            if c is not None:
                self.acc_ref[slot] = alpha * self.acc_ref[slot] + _dot(vt, p)


def _dsa_attn_body(qt_ref, k_ref, vt_ref, bias_ref, o_ref, s_ref, m_ref, acc_ref, *, tq, tiles_per_step):
    i = pl.program_id(0)
    flash = _Flash(s_ref, m_ref, acc_ref)
    tk = FLASH_TILE
    dv = A_HEAD_DIM
    heads = [slice(h * dv, (h + 1) * dv) for h in range(A_HEADS)]

    def key_rows(step, u):
        return pl.ds(pl.multiple_of((step * tiles_per_step + u) * tk, tk), tk)

    def cur_side(step):
        return [(u * A_HEADS + h, h, vt_ref[h, :, key_rows(step, u)])
                for u in range(tiles_per_step) for h in range(A_HEADS)]

    def nxt_side(step):
        out = []
        for u in range(tiles_per_step):
            rows = key_rows(step, u)
            b = bias_ref[rows, :].astype(F32)
            out += [(u * A_HEADS + h, k_ref[rows, hs], qt_ref[hs, :], b, None) for h, hs in enumerate(heads)]
        return out

    idle = [None] * (tiles_per_step * A_HEADS)
    n_live = ((i + 1) * tq + tk - 1) // tk
    n_steps = (n_live + tiles_per_step - 1) // tiles_per_step
    flash.step(idle, nxt_side(0))

    def body(step, _):
        flash.step(cur_side(step), nxt_side(step + 1))
        return 0

    lax.fori_loop(0, n_steps - 1, body, 0)
    flash.step(cur_side(n_steps - 1), idle)
    for h, hs in enumerate(heads):
        o_ref[:, hs] = jnp.transpose(acc_ref[h, :dv, :] / acc_ref[h, dv:dv + 1, :]).astype(o_ref.dtype)


def _dsa_attn(q_t, k, v_t, bias, tq):
    L = k.shape[0]
    n_tiles = L // tq
    key_tiles = L // FLASH_TILE
    tiles_per_step = 4 if key_tiles % 4 == 0 else (2 if key_tiles % 2 == 0 else 1)
    return pl.pallas_call(
        functools.partial(_dsa_attn_body, tq=tq, tiles_per_step=tiles_per_step),
        grid=(n_tiles,),
        in_specs=[
            pl.BlockSpec((A_WIDTH, tq), lambda i: (0, i)),
            _resident((L, A_WIDTH), lambda i: (0, 0)),
            _resident((A_HEADS, A_HEAD_DIM + ONES_ROWS, L), lambda i: (0, 0, 0)),
            pl.BlockSpec((None, L, tq), lambda i: (i, 0, 0)),
        ],
        out_specs=pl.BlockSpec((tq, A_WIDTH), lambda i: (i, 0)),
        out_shape=jax.ShapeDtypeStruct((L, A_WIDTH), BF16),
        scratch_shapes=[
            pltpu.VMEM((tiles_per_step * A_HEADS, FLASH_TILE, tq), F32),
            pltpu.VMEM((A_HEADS, 1, tq), F32),
            pltpu.VMEM((A_HEADS, A_HEAD_DIM + ONES_ROWS, tq), F32),
        ],
        compiler_params=_cparams(("parallel",)),
        name="dsa_attn",
    )(q_t, k, v_t, bias)


def _diff_attn_body(lq1_ref, lk1_ref, lq2_ref, lk2_ref, g_ref, qt_ref, k_ref, vt_ref, o_ref,
                    s_ref, m_ref, acc_ref, *, tq, lam_init):
    i = pl.program_id(1)
    flash = _Flash(s_ref, m_ref, acc_ref)
    dv = C_V_DIM
    tk = FLASH_TILE
    n_chunks = tq // FLASH_TILE
    n_full = i * n_chunks
    qt = qt_ref[...]
    row = lax.broadcasted_iota(jnp.int32, qt.shape, 0)
    zero = jnp.zeros_like(qt)
    qts = (jnp.where(row < C_QK_DIM, qt, zero), jnp.where(row >= C_QK_DIM, qt, zero))
    kk = lax.broadcasted_iota(jnp.int32, (tk, FLASH_TILE), 0)
    qq = lax.broadcasted_iota(jnp.int32, (tk, FLASH_TILE), 1)
    streams = [(c * n_chunks + qc, qc, qts[c][:, qc * FLASH_TILE:(qc + 1) * FLASH_TILE])
               for qc in range(n_chunks) for c in range(2)]

    def tile_start(jt):
        return pl.multiple_of(jt * tk, tk)

    def cur_side(jt, first_chunk=0):
        vt = vt_ref[:, pl.ds(tile_start(jt), tk)]
        return [(slot, slot, vt) if qc >= first_chunk else None for slot, qc, _ in streams]

    def nxt_side(jt, first_chunk=0, mask_of=lambda qc: None):
        kt = k_ref[pl.ds(tile_start(jt), tk), :]
        return [(slot, kt, q, None, mask_of(qc)) if qc >= first_chunk else None for slot, qc, q in streams]

    idle = [None] * len(streams)

    def first_mask(qc):
        q_pos = (i * n_chunks + qc) * FLASH_TILE + qq
        return (kk >> CHUNK_SHIFT) <= (q_pos >> CHUNK_SHIFT)

    flash.step(idle, nxt_side(0, mask_of=first_mask))

    def body(jt, _):
        flash.step(cur_side(jt), nxt_side(jt + 1))
        return 0

    lax.fori_loop(0, jnp.maximum(n_full - 1, 0), body, 0)

    diag = lambda d: (lambda qc: ((kk >> CHUNK_SHIFT) <= (qq >> CHUNK_SHIFT)) if qc == d else None)

    @pl.when(i > 0)
    def _():
        flash.step(cur_side(n_full - 1), nxt_side(n_full, mask_of=diag(0)))

    for d in range(1, n_chunks):
        flash.step(cur_side(n_full + d - 1, d - 1), nxt_side(n_full + d, d, diag(d)))
    flash.step(cur_side(n_full + n_chunks - 1, n_chunks - 1), idle)

    lam = (jnp.exp(jnp.sum(lq1_ref[...] * lk1_ref[...], axis=1, keepdims=True))
           - jnp.exp(jnp.sum(lq2_ref[...] * lk2_ref[...], axis=1, keepdims=True)) + lam_init)
    for qc in range(n_chunks):
        s0, s1 = qc, n_chunks + qc
        o = (acc_ref[s0, :dv, :] / acc_ref[s0, dv:dv + 1, :]
             - lam * (acc_ref[s1, :dv, :] / acc_ref[s1, dv:dv + 1, :]))
        inv = lax.rsqrt(jnp.mean(o * o, axis=0, keepdims=True) + EPS)
        o_ref[qc * FLASH_TILE:(qc + 1) * FLASH_TILE, :] = (
            jnp.transpose(o * inv) * g_ref[...] * (1.0 - lam_init)).astype(o_ref.dtype)


def _diff_attn(q_t, k, v_t, lam_vecs, sub_gain, layer, lam_init, tq):
    L = k.shape[0]
    vec = lambda w: _resident((None, 1, w), lambda h, i: (layer, 0, 0))
    return pl.pallas_call(
        functools.partial(_diff_attn_body, tq=tq, lam_init=lam_init),
        grid=(C_HEADS, L // tq),
        in_specs=[
            vec(C_QK_DIM), vec(C_QK_DIM), vec(C_QK_DIM), vec(C_QK_DIM), vec(C_V_DIM),
            pl.BlockSpec((C_V_DIM, tq), lambda h, i: (h, i)),
            pl.BlockSpec((L, C_V_DIM), lambda h, i: (0, h)),
            pl.BlockSpec((None, C_V_DIM + ONES_ROWS, L), lambda h, i: (h, 0, 0)),
        ],
        out_specs=pl.BlockSpec((tq, C_V_DIM), lambda h, i: (i, h)),
        out_shape=jax.ShapeDtypeStruct((L, C_WIDTH), BF16),
        scratch_shapes=[
            pltpu.VMEM((2 * (tq // FLASH_TILE), FLASH_TILE, FLASH_TILE), F32),
            pltpu.VMEM((2 * (tq // FLASH_TILE), 1, FLASH_TILE), F32),
            pltpu.VMEM((2 * (tq // FLASH_TILE), C_V_DIM + ONES_ROWS, FLASH_TILE), F32),
        ],
        compiler_params=_cparams(("parallel", "parallel")),
        name="diff_attn",
    )(*lam_vecs, sub_gain, q_t, k, v_t)


def _s5_prep_body(are_ref, aim_ref, ldt_ref, bre_ref, bim_ref, bb_ref, lvl_ref, pw_ref):
    a_re, a_im = are_ref[...], aim_ref[...]
    dt = jnp.exp(ldt_ref[...])
    mag = jnp.exp(a_re * dt)
    lb_re = mag * jnp.cos(a_im * dt)
    lb_im = mag * jnp.sin(a_im * dt)
    den = a_re * a_re + a_im * a_im
    f_re = ((lb_re - 1.0) * a_re + lb_im * a_im) / den
    f_im = (lb_im * a_re - (lb_re - 1.0) * a_im) / den
    b_re, b_im = bre_ref[...], bim_ref[...]
    bb_ref[:, :S5_NSTATE] = (f_re * b_re - f_im * b_im).astype(BF16)
    bb_ref[:, S5_NSTATE:] = (f_re * b_im + f_im * b_re).astype(BF16)

    pows = [(lb_re, lb_im)]
    for _ in range(SUBLANES - 1):
        pr, pi = pows[-1]
        pows.append((pr * lb_re - pi * lb_im, pr * lb_im + pi * lb_re))
    row = lax.broadcasted_iota(jnp.int32, (SUBLANES, S5_NSTATE), 0)
    zero = jnp.zeros((SUBLANES, S5_NSTATE), F32)
    for lvl in range(3):
        sh = 1 << lvl
        pr, pi = pows[sh - 1]
        lvl_ref[lvl, 0] = jnp.where(row >= sh, pr, zero)
        lvl_ref[lvl, 1] = jnp.where(row >= sh, pi, zero)
    cr, ci = zero, zero
    for r in range(SUBLANES):
        cr = jnp.where(row == r, pows[r][0], cr)
        ci = jnp.where(row == r, pows[r][1], ci)
    pw_ref[0] = cr
    pw_ref[1] = ci


def _s5_prep(a_re, a_im, log_dt, b_re_bd, b_im_bd):
    depth = a_re.shape[0]
    vec = pl.BlockSpec((None, 1, S5_NSTATE), lambda l: (l, 0, 0))
    mat = pl.BlockSpec((None, S5_WIDTH, S5_NSTATE), lambda l: (l, 0, 0))
    return pl.pallas_call(
        _s5_prep_body,
        grid=(depth,),
        in_specs=[vec, vec, vec, mat, mat],
        out_specs=[
            pl.BlockSpec((None, S5_WIDTH, 2 * S5_NSTATE), lambda l: (l, 0, 0)),
            pl.BlockSpec((None, 3, 2, SUBLANES, S5_NSTATE), lambda l: (l, 0, 0, 0, 0)),
            pl.BlockSpec((None, 2, SUBLANES, S5_NSTATE), lambda l: (l, 0, 0, 0)),
        ],
        out_shape=[
            jax.ShapeDtypeStruct((depth, S5_WIDTH, 2 * S5_NSTATE), BF16),
            jax.ShapeDtypeStruct((depth, 3, 2, SUBLANES, S5_NSTATE), F32),
            jax.ShapeDtypeStruct((depth, 2, SUBLANES, S5_NSTATE), F32),
        ],
        compiler_params=_cparams(("parallel",)),
        name="s5_prep",
    )(a_re, a_im, log_dt, b_re_bd, b_im_bd)


S5_LANE_CHUNK = 512


def _s5_body(u_ref, bb_ref, lvl_ref, pw_ref, cre_ref, cim_ref, d_ref, wg_ref, bg_ref, o_ref,
             x_ref, carry_ref, *, tl):
    @pl.when(pl.program_id(0) == 0)
    def _():
        carry_ref[...] = jnp.zeros(carry_ref.shape, F32)

    u = u_ref[...]
    x_ref[...] = _dot(u.astype(BF16), bb_ref[...])

    for c0 in range(0, S5_NSTATE, S5_LANE_CHUNK):
        re_sl = slice(c0, c0 + S5_LANE_CHUNK)
        im_sl = slice(S5_NSTATE + c0, S5_NSTATE + c0 + S5_LANE_CHUNK)

        def block(t, carry, re_sl=re_sl, im_sl=im_sl):
            c_re, c_im = carry
            rows = pl.ds(pl.multiple_of(t * SUBLANES, SUBLANES), SUBLANES)
            re = x_ref[rows, re_sl]
            im = x_ref[rows, im_sl]
            for lvl in range(3):
                sh = 1 << lvl
                s_re = pltpu.roll(re, sh, 0)
                s_im = pltpu.roll(im, sh, 0)
                a_re = lvl_ref[lvl, 0, :, re_sl]
                a_im = lvl_ref[lvl, 1, :, re_sl]
                re, im = re + (a_re * s_re - a_im * s_im), im + (a_re * s_im + a_im * s_re)
            p_re = pw_ref[0, :, re_sl]
            p_im = pw_ref[1, :, re_sl]
            re, im = re + (p_re * c_re - p_im * c_im), im + (p_re * c_im + p_im * c_re)
            x_ref[rows, re_sl] = re
            x_ref[rows, im_sl] = im
            last = SUBLANES - 1
            return (jnp.broadcast_to(re[last:, :], re.shape), jnp.broadcast_to(im[last:, :], im.shape))

        carry = lax.fori_loop(0, tl // SUBLANES, block,
                              (carry_ref[0, :, re_sl], carry_ref[1, :, re_sl]))
        carry_ref[0, :, re_sl] = carry[0]
        carry_ref[1, :, re_sl] = carry[1]

    y = (_dot(x_ref[:, :S5_NSTATE].astype(BF16), cre_ref[...])
         - _dot(x_ref[:, S5_NSTATE:].astype(BF16), cim_ref[...])
         + d_ref[...] * u)
    y = 0.5 * y * (1.0 + jnp.tanh(math.sqrt(2.0 / math.pi) * (y + 0.044715 * (y * y * y))))
    z = _dot(y.astype(BF16), wg_ref[...]) + bg_ref[...]
    o_ref[...] = (y * (1.0 / (1.0 + jnp.exp(-z)))).astype(o_ref.dtype)


def _s5_mix(u, bb, lvl, pw, c_re_bd, c_im_bd, d_row, w_glu, b_glu, layer, tl):
    L = u.shape[0]
    lay = lambda *rest: (lambda t: (layer,) + rest)
    return pl.pallas_call(
        functools.partial(_s5_body, tl=tl),
        grid=(L // tl,),
        in_specs=[
            pl.BlockSpec((tl, S5_WIDTH), lambda t: (t, 0)),
            _resident((None, S5_WIDTH, 2 * S5_NSTATE), lay(0, 0)),
            _resident((None, 3, 2, SUBLANES, S5_NSTATE), lay(0, 0, 0, 0)),
            _resident((None, 2, SUBLANES, S5_NSTATE), lay(0, 0, 0)),
            _resident((None, S5_NSTATE, S5_WIDTH), lay(0, 0)),
            _resident((None, S5_NSTATE, S5_WIDTH), lay(0, 0)),
            _resident((None, 1, S5_WIDTH), lay(0, 0)),
            _resident((None, S5_WIDTH, S5_WIDTH), lay(0, 0)),
            _resident((None, 1, S5_WIDTH), lay(0, 0)),
        ],
        out_specs=pl.BlockSpec((tl, S5_WIDTH), lambda t: (t, 0)),
        out_shape=jax.ShapeDtypeStruct((L, S5_WIDTH), BF16),
        scratch_shapes=[
            pltpu.VMEM((tl, 2 * S5_NSTATE), F32),
            pltpu.VMEM((2, SUBLANES, S5_NSTATE), F32),
        ],
        compiler_params=_cparams(("arbitrary",)),
        name="s5_mix",
    )(u, bb, lvl, pw, c_re_bd, c_im_bd, d_row, w_glu, b_glu)


def _outproj_body(x_ref, oa_ref, os_ref, oc_ref, w_ref, o_ref):
    a0, a1, a2 = A_WIDTH, A_WIDTH + S5_WIDTH, A_WIDTH + S5_WIDTH + C_WIDTH
    o_ref[...] = (x_ref[...] + _dot(oa_ref[...], w_ref[:a0, :]) + _dot(os_ref[...], w_ref[a0:a1, :])
                  + _dot(oc_ref[...], w_ref[a1:a2, :]))


def _out_proj(x, o_a, o_s, o_c, w_out, layer, tm):
    L = x.shape[0]
    row = lambda i: (i, 0)
    return pl.pallas_call(
        _outproj_body,
        grid=(L // tm,),
        in_specs=[
            pl.BlockSpec((tm, D_MODEL), row),
            pl.BlockSpec((tm, A_WIDTH), row),
            pl.BlockSpec((tm, S5_WIDTH), row),
            pl.BlockSpec((tm, C_WIDTH), row),
            _resident((None, D_MODEL, D_MODEL), lambda i: (layer, 0, 0)),
        ],
        out_specs=pl.BlockSpec((tm, D_MODEL), row),
        out_shape=jax.ShapeDtypeStruct((L, D_MODEL), F32),
        compiler_params=_cparams(("parallel",)),
        name="out_proj",
    )(x, o_a, o_s, o_c, w_out)


def _norm_matmul_body(x_ref, g_ref, w_ref, o_ref):
    o_ref[...] = _dot(_rms(x_ref[...], g_ref[...]).astype(BF16), w_ref[...]).astype(o_ref.dtype)


def _norm_matmul(x, g, w, layer, tn):
    M, K = x.shape
    N = w.shape[-1]
    return pl.pallas_call(
        _norm_matmul_body,
        grid=(N // tn,),
        in_specs=[
            _resident((M, K), lambda j: (0, 0)),
            _resident((None, 1, K), lambda j: (layer, 0, 0)),
            pl.BlockSpec((None, K, tn), lambda j: (layer, 0, j)),
        ],
        out_specs=pl.BlockSpec((M, tn), lambda j: (0, j)),
        out_shape=jax.ShapeDtypeStruct((M, N), BF16),
        compiler_params=_cparams(("parallel",)),
        name="mem_proj",
    )(x, g, w)


def _xattn_body(x_ref, g_ref, wq_ref, k_ref, v_ref, wo_ref, o_ref):
    x = x_ref[...]
    q = _dot(_rms(x, g_ref[...]).astype(BF16), wq_ref[...]).astype(BF16)
    scale = X_HEAD_DIM ** -0.5
    out = x
    for h in range(X_HEADS):
        hs = slice(h * X_HEAD_DIM, (h + 1) * X_HEAD_DIM)
        s = _dot_nt(q[:, hs], k_ref[:, hs]) * scale
        p = jnp.exp(s - jnp.max(s, axis=1, keepdims=True))
        p = p / jnp.sum(p, axis=1, keepdims=True)
        o_h = _dot(p.astype(BF16), v_ref[:, hs]).astype(BF16)
        out = out + _dot(o_h, wo_ref[hs, :])
    o_ref[...] = out


def _cross_attn(x, g, wq, k_mem, v_mem, wo, layer, tm):
    L = x.shape[0]
    M = k_mem.shape[0]
    row = lambda i: (i, 0)
    return pl.pallas_call(
        _xattn_body,
        grid=(L // tm,),
        in_specs=[
            pl.BlockSpec((tm, D_MODEL), row),
            _resident((None, 1, D_MODEL), lambda i: (layer, 0, 0)),
            _resident((None, D_MODEL, D_MODEL), lambda i: (layer, 0, 0)),
            _resident((M, D_MODEL), lambda i: (0, 0)),
            _resident((M, D_MODEL), lambda i: (0, 0)),
            _resident((None, D_MODEL, D_MODEL), lambda i: (layer, 0, 0)),
        ],
        out_specs=pl.BlockSpec((tm, D_MODEL), row),
        out_shape=jax.ShapeDtypeStruct((L, D_MODEL), F32),
        compiler_params=_cparams(("parallel",)),
        name="cross_attn",
    )(x, g, wq, k_mem, v_mem, wo)


def _mlp_body(x_ref, g_ref, w1_ref, w2_ref, o_ref, hn_ref):
    @pl.when(pl.program_id(1) == 0)
    def _():
        x = x_ref[...]
        hn_ref[...] = _rms(x, g_ref[...]).astype(BF16)
        o_ref[...] = x

    a = jnp.maximum(_dot(hn_ref[...], w1_ref[...]), 0.0)
    o_ref[...] += _dot((a * a).astype(BF16), w2_ref[...])


def _mlp(x, g, w1, w2, layer, tm, tf):
    L = x.shape[0]
    return pl.pallas_call(
        _mlp_body,
        grid=(L // tm, D_FF // tf),
        in_specs=[
            pl.BlockSpec((tm, D_MODEL), lambda i, f: (i, 0), pipeline_mode=pl.Buffered(1)),
            _resident((None, 1, D_MODEL), lambda i, f: (layer, 0, 0)),
            pl.BlockSpec((None, D_MODEL, tf), lambda i, f: (layer, 0, f)),
            pl.BlockSpec((None, tf, D_MODEL), lambda i, f: (layer, f, 0)),
        ],
        out_specs=pl.BlockSpec((tm, D_MODEL), lambda i, f: (i, 0)),
        out_shape=jax.ShapeDtypeStruct((L, D_MODEL), F32),
        scratch_shapes=[pltpu.VMEM((tm, D_MODEL), BF16)],
        compiler_params=_cparams(("parallel", "arbitrary")),
        name="mlp",
    )(x, g, w1, w2)


def _final_norm_body(x_ref, g_ref, o_ref):
    o_ref[...] = _rms(x_ref[...], g_ref[...])


def _final_norm(x, g, tm):
    L = x.shape[0]
    return pl.pallas_call(
        _final_norm_body,
        grid=(L // tm,),
        in_specs=[pl.BlockSpec((tm, D_MODEL), lambda i: (i, 0)),
                  _resident((1, D_MODEL), lambda i: (0, 0))],
        out_specs=pl.BlockSpec((tm, D_MODEL), lambda i: (i, 0)),
        out_shape=jax.ShapeDtypeStruct((L, D_MODEL), F32),
        compiler_params=_cparams(("parallel",)),
        name="final_norm",
    )(x, g)


def _block_diag_in(b):
    eye = jnp.eye(S5_GROUPS, dtype=b.dtype)
    t = jnp.transpose(b, (0, 1, 3, 2))
    bd = t[:, :, :, None, :] * eye[None, :, None, :, None]
    return bd.reshape(b.shape[0], S5_WIDTH, S5_NSTATE)


def _block_diag_out(c):
    eye = jnp.eye(S5_GROUPS, dtype=c.dtype)
    t = jnp.transpose(c, (0, 1, 3, 2))
    bd = t[:, :, :, None, :] * eye[None, :, None, :, None]
    return bd.reshape(c.shape[0], S5_NSTATE, S5_WIDTH)


def _pick_tile(n, want):
    t = min(n, want)
    assert n % t == 0, (n, t)
    return t


def kernel(x, mem, norm_mix, w_in, s5_a_re, s5_a_im, s5_log_dt, s5_b_re, s5_b_im, s5_c_re, s5_c_im, s5_d, s5_w_glu, s5_b_glu, diff_lam_q1, diff_lam_k1, diff_lam_q2, diff_lam_k2, diff_subln, w_out, norm_xattn, norm_mem, xattn_q, xattn_k, xattn_v, xattn_o, norm_mlp, w_ff1, w_ff2, norm_final):
    B, L, _ = x.shape
    assert B == 1
    depth = w_in.shape[0]
    tq = _pick_tile(L, 256)

    pad = jnp.zeros((depth, D_MODEL, KIW_W - IDX_DIM - IDX_HEADS), BF16)
    w_in_p = jnp.concatenate([w_in[..., :IN_SRC_SPLIT].astype(BF16), pad, w_in[..., IN_SRC_SPLIT:].astype(BF16)],
                             axis=-1)
    w_out_b = w_out.astype(BF16)
    wq_b, wk_b, wv_b, wo_b = (w.astype(BF16) for w in (xattn_q, xattn_k, xattn_v, xattn_o))
    w1_b, w2_b = w_ff1.astype(BF16), w_ff2.astype(BF16)
    wg_b = s5_w_glu.astype(BF16)
    row3 = lambda a: a.reshape(depth, 1, -1)

    tabs, tabs_k = _rope_tables(L)

    rep = lambda a: row3(a.astype(F32))
    a_re, a_im = rep(s5_a_re), rep(s5_a_im)
    log_dt = row3(jnp.broadcast_to(s5_log_dt.astype(F32)[:, :, None], (depth, S5_GROUPS, S5_STATE)))
    bb, lvl, pw = _s5_prep(a_re, a_im, log_dt, _block_diag_in(s5_b_re.astype(F32)),
                           _block_diag_in(s5_b_im.astype(F32)))
    c_re_bd = _block_diag_out(s5_c_re).astype(BF16)
    c_im_bd = _block_diag_out(s5_c_im).astype(BF16)
    d_row = row3(s5_d.astype(F32))
    lam_vecs = tuple(row3(v.astype(F32)) for v in (diff_lam_q1, diff_lam_k1, diff_lam_q2, diff_lam_k2))

    xs = x[0]
    mem2 = mem[0]
    for l in range(depth):
        qa, ka, va, qi, ke, ko, wi, us, qc, kc, vc = _in_proj(xs, row3(norm_mix), w_in_p, l, tabs + tabs_k,
                                                               _pick_tile(L, 256))

        w_t = jnp.transpose(wi[:, IDX_DIM:IDX_DIM + IDX_HEADS])
        bias = _dsa_index(jnp.transpose(qi), w_t, ke, ko, tq).reshape(L // tq, L, tq)
        o_a = _dsa_attn(jnp.transpose(qa), ka, _values_t(va, A_HEADS), bias, tq)

        o_s = _s5_mix(us, bb, lvl, pw, c_re_bd, c_im_bd, d_row, wg_b, row3(s5_b_glu.astype(F32)), l,
                      _pick_tile(L, 512))

        lam_init = 0.8 - 0.6 * math.exp(-0.3 * l)
        o_c = _diff_attn(jnp.transpose(qc), kc, _values_t(vc, C_HEADS), lam_vecs, row3(diff_subln.astype(F32)), l,
                         lam_init, _pick_tile(L, 2048))

        xs = _out_proj(xs, o_a, o_s, o_c, w_out_b, l, _pick_tile(L, 512))

        k_mem = _norm_matmul(mem2, row3(norm_mem), wk_b, l, 512)
        v_mem = _norm_matmul(mem2, row3(norm_mem), wv_b, l, 512)
        xs = _cross_attn(xs, row3(norm_xattn), wq_b, k_mem, v_mem, wo_b, l, _pick_tile(L, 512))

        xs = _mlp(xs, row3(norm_mlp), w1_b, w2_b, l, _pick_tile(L, 1024), 1024)

    return _final_norm(xs, norm_final.reshape(1, -1), _pick_tile(L, 512))[None]
```

```python
import functools
import math

import jax
import jax.numpy as jnp
from jax import lax
from jax.experimental import pallas as pl
from jax.experimental.pallas import tpu as pltpu

F32 = jnp.float32
BF16 = jnp.bfloat16

D_MODEL = 2048
CHUNK = 64
CHUNK_SHIFT = CHUNK.bit_length() - 1
assert 1 << CHUNK_SHIFT == CHUNK
ROPE_THETA = 10000.0
EPS = 1e-6
NEG = -1e30

A_HEAD_DIM = 128
A_WIDTH = D_MODEL // 4
A_HEADS = A_WIDTH // A_HEAD_DIM
IDX_HEADS = 8
IDX_DIM = 64
TOPK_MAX = 256

S5_GROUP = 16
S5_WIDTH = D_MODEL // 4
S5_GROUPS = S5_WIDTH // S5_GROUP
S5_STATE = 64
S5_NSTATE = S5_GROUPS * S5_STATE

C_QK_DIM = 64
C_V_DIM = 2 * C_QK_DIM
C_WIDTH = D_MODEL // 2
C_HEADS = C_WIDTH // C_V_DIM

X_HEADS = 4
X_HEAD_DIM = D_MODEL // X_HEADS
D_FF = 4 * D_MODEL

LANES = 128
SUBLANES = 8
VMEM_LIMIT = 58 * 1024 * 1024

KIW_W = LANES
OFF_QA = 0
OFF_KA = OFF_QA + A_WIDTH
OFF_VA = OFF_KA + A_WIDTH
OFF_QI = OFF_VA + A_WIDTH
OFF_KIW = OFF_QI + IDX_HEADS * IDX_DIM
OFF_US = OFF_KIW + KIW_W
OFF_QC = OFF_US + S5_WIDTH
OFF_KC = OFF_QC + C_WIDTH
OFF_VC = OFF_KC + C_WIDTH
IN_PACKED = OFF_VC + C_WIDTH
IN_SRC_SPLIT = 3 * A_WIDTH + IDX_HEADS * IDX_DIM + IDX_DIM + IDX_HEADS

LOG2E = math.log2(math.e)
INT_MIN = -2 ** 31
INT_MAX = 2 ** 31 - 1


def _cparams(sem):
    return pltpu.CompilerParams(dimension_semantics=sem, vmem_limit_bytes=VMEM_LIMIT)


def _resident(shape, index_map):
    return pl.BlockSpec(shape, index_map, pipeline_mode=pl.Buffered(1))


def _rms(x, g):
    inv = lax.rsqrt(jnp.mean(x * x, axis=-1, keepdims=True) + EPS)
    return x * inv * g


def _dot(a, b):
    return jnp.dot(a, b, preferred_element_type=F32)


def _dot_nt(a, b):
    return lax.dot_general(a, b, (((1,), (1,)), ((), ())), preferred_element_type=F32)


def _inproj_body(x_ref, g_ref, w_ref, c128_ref, s128_ref, c64_ref, sa64_ref, sb64_ref,
                 c64k_ref, sa64k_ref, sb64k_ref,
                 qa_ref, ka_ref, va_ref, qi_ref, ke_ref, ko_ref, wi_ref, us_ref, qc_ref, kc_ref, vc_ref):
    hn = _rms(x_ref[...], g_ref[...]).astype(BF16)
    c128, s128 = c128_ref[...], s128_ref[...]
    c64, sa64, sb64 = c64_ref[...], sa64_ref[...], sb64_ref[...]

    def rope128(t):
        return t * c128 + pltpu.roll(t, 64, 1) * s128

    def rope64(t):
        return t * c64 + pltpu.roll(t, 96, 1) * sa64 + pltpu.roll(t, 32, 1) * sb64

    def rope64_kiw(t):
        return t * c64k_ref[...] + pltpu.roll(t, 96, 1) * sa64k_ref[...] + pltpu.roll(t, 32, 1) * sb64k_ref[...]

    def emit(off, width, out_ref, fn, scale=None):
        step = 512 if width >= 512 else width
        for c0 in range(0, width, step):
            t = _dot(hn, w_ref[:, off + c0:off + c0 + step])
            for s0 in range(0, step, LANES):
                v = t[:, s0:s0 + LANES]
                if fn is not None:
                    v = fn(v)
                if scale is not None:
                    v = v * scale
                out_ref[:, c0 + s0:c0 + s0 + LANES] = v.astype(out_ref.dtype)

    emit(OFF_QA, A_WIDTH, qa_ref, rope128, A_HEAD_DIM ** -0.5 * LOG2E)
    emit(OFF_KA, A_WIDTH, ka_ref, rope128)
    emit(OFF_VA, A_WIDTH, va_ref, None)
    emit(OFF_QI, IDX_HEADS * IDX_DIM, qi_ref, rope64)
    kiw = rope64_kiw(_dot(hn, w_ref[:, OFF_KIW:OFF_KIW + KIW_W]))
    lane = lax.broadcasted_iota(jnp.int32, kiw.shape, 1)
    k_low = jnp.where(lane < IDX_DIM, kiw, 0.0)
    ke_ref[...] = k_low.astype(BF16)
    ko_ref[...] = pltpu.roll(k_low, IDX_DIM, 1).astype(BF16)
    wi_ref[...] = kiw
    emit(OFF_US, S5_WIDTH, us_ref, None)
    emit(OFF_QC, C_WIDTH, qc_ref, rope64, C_QK_DIM ** -0.5 * LOG2E)
    emit(OFF_KC, C_WIDTH, kc_ref, rope64)
    emit(OFF_VC, C_WIDTH, vc_ref, None)


def _in_proj(x, g, w_packed, layer, tabs, tm):
    L = x.shape[0]
    row = lambda i: (i, 0)
    tab_spec = pl.BlockSpec((tm, LANES), row)
    outs = [
        (A_WIDTH, BF16), (A_WIDTH, BF16), (A_WIDTH, BF16), (IDX_HEADS * IDX_DIM, BF16),
        (KIW_W, BF16), (KIW_W, BF16), (KIW_W, F32),
        (S5_WIDTH, F32), (C_WIDTH, BF16), (C_WIDTH, BF16), (C_WIDTH, BF16),
    ]
    return pl.pallas_call(
        _inproj_body,
        grid=(L // tm,),
        in_specs=[
            pl.BlockSpec((tm, D_MODEL), row),
            _resident((None, 1, D_MODEL), lambda i: (layer, 0, 0)),
            _resident((None, D_MODEL, IN_PACKED), lambda i: (layer, 0, 0)),
        ] + [tab_spec] * 8,
        out_specs=[pl.BlockSpec((tm, w), row) for w, _ in outs],
        out_shape=[jax.ShapeDtypeStruct((L, w), dt) for w, dt in outs],
        compiler_params=_cparams(("parallel",)),
        name="in_proj",
    )(x, g, w_packed, *tabs)


def _rope_tables(L):
    pos = jnp.arange(L, dtype=F32)[:, None]

    def cs(dim):
        inv = ROPE_THETA ** (-jnp.arange(0, dim, 2, dtype=F32) / dim)
        ang = pos * inv[None, :]
        return jnp.cos(ang), jnp.sin(ang)

    c, s = cs(A_HEAD_DIM)
    c128 = jnp.concatenate([c, c], axis=1)
    s128 = jnp.concatenate([-s, s], axis=1)
    c, s = cs(IDX_DIM)
    z = jnp.zeros_like(s)
    c64 = jnp.tile(jnp.concatenate([c, c], axis=1), (1, 2))
    sa64 = jnp.tile(jnp.concatenate([-s, z], axis=1), (1, 2))
    sb64 = jnp.tile(jnp.concatenate([z, s], axis=1), (1, 2))
    lane = jnp.arange(LANES)[None, :]
    c64k = jnp.where(lane < IDX_DIM, c64, 1.0)
    sa64k = jnp.where(lane < IDX_DIM, sa64, 0.0)
    sb64k = jnp.where(lane < IDX_DIM, sb64, 0.0)
    return (c128, s128, c64, sa64, sb64), (c64k, sa64k, sb64k)


def _sortable(v):
    k = lax.bitcast_convert_type(v, jnp.int32)
    return k ^ ((k >> 31) & jnp.int32(INT_MAX))


COUNT_ROWS = 32
WRITE_TILES_PER_STEP = 4
COARSE_BITS = 16
COARSE_SPAN = 1 << COARSE_BITS
COARSE_TILES_PER_STEP = 4
assert COARSE_TILES_PER_STEP * 256 // COUNT_ROWS <= 256


def _dsa_index_body(qit_ref, wt_ref, ke_ref, ko_ref, bias_ref, skey_ref, coarse_ref, gmax_ref,
                    *, tq, top_k, n_tiles):
    i = pl.program_id(0)
    score_tiles_per_step = 2 if n_tiles % 2 == 0 else 1
    n_live = i + 1
    q_pos = i * tq + lax.broadcasted_iota(jnp.int32, (tq, tq), 1)
    k_off = lax.broadcasted_iota(jnp.int32, (tq, tq), 0)
    wv = wt_ref[...] * ((IDX_HEADS ** -0.5) * (IDX_DIM ** -0.5))
    gmax_ref[...] = jnp.full((tq, tq), INT_MIN, jnp.int32)

    def score_tiles(step, _):
        for jt in [step * score_tiles_per_step + u for u in range(score_tiles_per_step)]:
            rows = pl.ds(pl.multiple_of(jt * tq, tq), tq)
            ke = ke_ref[rows, :]
            ko = ko_ref[rows, :]
            acc = jnp.zeros((tq, tq), F32)
            for hp in range(IDX_HEADS // 2):
                q_pair = qit_ref[hp * LANES:(hp + 1) * LANES, :]
                d0 = _dot(ke, q_pair)
                d1 = _dot(ko, q_pair)
                acc = (acc + wv[2 * hp:2 * hp + 1, :] * jnp.maximum(d0, 0.0)
                       + wv[2 * hp + 1:2 * hp + 2, :] * jnp.maximum(d1, 0.0))
            allowed = ((jt * tq + k_off) >> CHUNK_SHIFT) <= (q_pos >> CHUNK_SHIFT)
            bits = lax.bitcast_convert_type(jnp.where(allowed, acc, NEG), jnp.int32)
            key = bits ^ ((bits >> 31) & jnp.int32(INT_MAX))
            skey_ref[jt] = key
            coarse_ref[jt] = lax.bitcast_convert_type(bits & jnp.int32(-COARSE_SPAN), F32).astype(BF16)
            gmax_ref[...] = jnp.maximum(gmax_ref[...], key)
        return 0

    lax.fori_loop(0, (n_live + score_tiles_per_step - 1) // score_tiles_per_step, score_tiles, 0)

    def count(pred):
        def tiles(first, n, acc):
            for jt in [first + u for u in range(n)]:
                for r in range(0, tq, COUNT_ROWS):
                    acc = acc + jnp.where(pred(skey_ref[jt, r:r + COUNT_ROWS, :]), 1.0, 0.0)
            return acc

        n_groups = n_live // COARSE_TILES_PER_STEP
        acc = lax.fori_loop(0, n_groups, lambda g, a: tiles(g * COARSE_TILES_PER_STEP, COARSE_TILES_PER_STEP, a),
                            jnp.zeros((COUNT_ROWS, tq), F32))
        acc = lax.fori_loop(n_groups * COARSE_TILES_PER_STEP, n_live, lambda jt, a: tiles(jt, 1, a), acc)
        return jnp.sum(acc, axis=0, keepdims=True)

    kf = float(top_k)

    gmax = gmax_ref[...]
    lo0 = jnp.min(gmax, axis=0, keepdims=True)
    hi0 = jnp.max(gmax, axis=0, keepdims=True)

    def narrowing(carry):
        lo, hi = carry
        return jnp.max(jnp.where(hi != lo, 1.0, 0.0)) > 0.0

    def bisect(carry):
        lo, hi = carry
        mid = (lo >> 1) + (hi >> 1) + (((lo & 1) + (hi & 1) + 1) >> 1)
        cnt = count(lambda t: t >= mid)
        ok = cnt >= kf
        hi_new = jnp.where(cnt == kf, mid, jnp.where(ok, hi, mid - 1))
        return jnp.where(ok, mid, lo), hi_new

    at_zero = count(lambda t: t >= 0) >= kf
    above_zero = count(lambda t: t >= 1) >= kf
    lo1 = jnp.where(above_zero, jnp.maximum(lo0, 1), jnp.where(at_zero, 0, lo0))
    hi1 = jnp.where(above_zero, hi0, jnp.where(at_zero, 0, jnp.minimum(hi0, -1)))

    def coarse_count(cand):
        cb = jnp.broadcast_to(cand, (COUNT_ROWS, tq))
        one, zero = jnp.ones((COUNT_ROWS, tq), cb.dtype), jnp.zeros((COUNT_ROWS, tq), cb.dtype)

        def tiles(first, n, acc):
            part = zero
            for jt in [first + u for u in range(n)]:
                for r in range(0, tq, COUNT_ROWS):
                    part = part + jnp.where(coarse_ref[jt, r:r + COUNT_ROWS, :] >= cb, one, zero)
            return acc + part.astype(F32)

        n_groups = n_live // COARSE_TILES_PER_STEP
        acc = lax.fori_loop(0, n_groups, lambda g, a: tiles(g * COARSE_TILES_PER_STEP, COARSE_TILES_PER_STEP, a),
                            jnp.zeros((COUNT_ROWS, tq), F32))
        acc = lax.fori_loop(n_groups * COARSE_TILES_PER_STEP, n_live, lambda jt, a: tiles(jt, 1, a), acc)
        return jnp.sum(acc, axis=0, keepdims=True)

    def coarse_value(p):
        k = (p << COARSE_BITS) | jnp.where(p < 0, COARSE_SPAN - 1, 0)
        return lax.bitcast_convert_type(k ^ ((k >> 31) & jnp.int32(INT_MAX)), F32).astype(BF16)

    def coarse_bisect(carry):
        lo, hi, done = carry
        mid = (lo >> 1) + (hi >> 1) + (((lo & 1) + (hi & 1) + 1) >> 1)
        cnt = coarse_count(coarse_value(mid))
        ok = cnt >= kf
        stop = (cnt == kf) & (mid != 0)
        return (jnp.where(ok, mid, lo), jnp.where(stop, mid, jnp.where(ok, hi, mid - 1)),
                jnp.where(stop, 1, done))

    c16, _, done = lax.while_loop(lambda c: narrowing(c[:2]), coarse_bisect,
                                  (lo1 >> COARSE_BITS, hi1 >> COARSE_BITS, jnp.zeros((1, tq), jnp.int32)))
    base = c16 << COARSE_BITS
    lo2 = jnp.maximum(jnp.where(c16 == 0, -COARSE_SPAN, base), lo1)
    hi2 = jnp.where(done == 1, lo2, jnp.minimum(base + (COARSE_SPAN - 1), hi1))
    lo, _ = lax.while_loop(narrowing, bisect, (lo2, hi2))

    need = kf - count(lambda t: t > lo)
    tri = jnp.where(k_off >= lax.broadcasted_iota(jnp.int32, (tq, tq), 1), 1.0, 0.0).astype(BF16)

    def write_tile(jt, seen, on_diagonal):
        t = skey_ref[jt]
        tied = t == lo
        rank = seen + _dot(tri, jnp.where(tied, 1.0, 0.0).astype(BF16))
        sel = (t > lo) | (tied & (rank <= need))
        if on_diagonal:
            sel = sel & (((jt * tq + k_off) >> CHUNK_SHIFT) <= (q_pos >> CHUNK_SHIFT))
        bias_ref[jt] = jnp.where(sel, 0.0, NEG).astype(BF16)
        return rank[tq - 1:tq, :]

    def write_group(step, seen):
        for u in range(WRITE_TILES_PER_STEP):
            seen = write_tile(step * WRITE_TILES_PER_STEP + u, seen, False)
        return seen

    n_groups = i // WRITE_TILES_PER_STEP
    seen = lax.fori_loop(0, n_groups, write_group, jnp.zeros((1, tq), F32))
    seen = lax.fori_loop(n_groups * WRITE_TILES_PER_STEP, i, lambda jt, s: write_tile(jt, s, False), seen)
    write_tile(i, seen, True)

    def blank_tile(jt, _):
        bias_ref[jt] = jnp.full((tq, tq), NEG, BF16)
        return 0

    lax.fori_loop(n_live, n_tiles, blank_tile, 0)


def _dsa_index(qi_t, w_t, ke, ko, tq):
    L = qi_t.shape[1]
    n_tiles = L // tq
    top_k = min(TOPK_MAX, L // 4)
    assert top_k <= tq
    body = functools.partial(_dsa_index_body, tq=tq, top_k=top_k, n_tiles=n_tiles)
    return pl.pallas_call(
        body,
        grid=(n_tiles,),
        in_specs=[
            pl.BlockSpec((IDX_HEADS * IDX_DIM, tq), lambda i: (0, i)),
            pl.BlockSpec((IDX_HEADS, tq), lambda i: (0, i)),
            _resident((L, LANES), lambda i: (0, 0)),
            _resident((L, LANES), lambda i: (0, 0)),
        ],
        out_specs=pl.BlockSpec((None, n_tiles, tq, tq), lambda i: (i, 0, 0, 0)),
        out_shape=jax.ShapeDtypeStruct((n_tiles, n_tiles, tq, tq), BF16),
        scratch_shapes=[pltpu.VMEM((n_tiles, tq, tq), jnp.int32), pltpu.VMEM((n_tiles, tq, tq), BF16),
                        pltpu.VMEM((tq, tq), jnp.int32)],
        compiler_params=_cparams(("parallel",)),
        name="dsa_index",
    )(qi_t, w_t, ke, ko)


FLASH_TILE = 256
ONES_ROWS = 16


def _values_t(v, heads):
    L = v.shape[0]
    vt = jnp.transpose(v).reshape(heads, -1, L)
    return jnp.concatenate([vt, jnp.ones((heads, ONES_ROWS, L), v.dtype)], axis=1)


class _Flash:
    def __init__(self, s_ref, m_ref, acc_ref):
        self.s_ref, self.m_ref, self.acc_ref = s_ref, m_ref, acc_ref
        m_ref[...] = jnp.full(m_ref.shape, NEG, F32)
        acc_ref[...] = jnp.zeros(acc_ref.shape, F32)

    def step(self, cur, nxt):
        for c, n in zip(cur, nxt):
            if c is not None:
                buf, slot, vt = c
                s = self.s_ref[buf]
                m_prev = self.m_ref[slot]
                m_new = jnp.maximum(m_prev, jnp.max(s, axis=0, keepdims=True))
                alpha = jnp.exp2(m_prev - m_new)
                p = jnp.exp2(s - m_new).astype(BF16)
                self.m_ref[slot] = m_new
            if n is not None:
                buf_n, kt, qt, add, mask = n
                s_n = _dot(kt, qt)
                if add is not None:
                    s_n = s_n + add
                if mask is not None:
                    s_n = jnp.where(mask, s_n, NEG)
                self.s_ref[buf_n] = s_n
            if c is not None:
                self.acc_ref[slot] = alpha * self.acc_ref[slot] + _dot(vt, p)


def _dsa_attn_body(qt_ref, k_ref, vt_ref, bias_ref, o_ref, s_ref, m_ref, acc_ref, *, tq, tiles_per_step):
    i = pl.program_id(0)
    flash = _Flash(s_ref, m_ref, acc_ref)
    tk = FLASH_TILE
    dv = A_HEAD_DIM
    heads = [slice(h * dv, (h + 1) * dv) for h in range(A_HEADS)]

    def key_rows(step, u):
        return pl.ds(pl.multiple_of((step * tiles_per_step + u) * tk, tk), tk)

    def cur_side(step):
        return [(u * A_HEADS + h, h, vt_ref[h, :, key_rows(step, u)])
                for u in range(tiles_per_step) for h in range(A_HEADS)]

    def nxt_side(step):
        out = []
        for u in range(tiles_per_step):
            rows = key_rows(step, u)
            b = bias_ref[rows, :].astype(F32)
            out += [(u * A_HEADS + h, k_ref[rows, hs], qt_ref[hs, :], b, None) for h, hs in enumerate(heads)]
        return out

    idle = [None] * (tiles_per_step * A_HEADS)
    n_live = ((i + 1) * tq + tk - 1) // tk
    n_steps = (n_live + tiles_per_step - 1) // tiles_per_step
    flash.step(idle, nxt_side(0))

    def body(step, _):
        flash.step(cur_side(step), nxt_side(step + 1))
        return 0

    lax.fori_loop(0, n_steps - 1, body, 0)
    flash.step(cur_side(n_steps - 1), idle)
    for h, hs in enumerate(heads):
        o_ref[:, hs] = jnp.transpose(acc_ref[h, :dv, :] / acc_ref[h, dv:dv + 1, :]).astype(o_ref.dtype)


def _dsa_attn(q_t, k, v_t, bias, tq):
    L = k.shape[0]
    n_tiles = L // tq
    key_tiles = L // FLASH_TILE
    tiles_per_step = 4 if key_tiles % 4 == 0 else (2 if key_tiles % 2 == 0 else 1)
    return pl.pallas_call(
        functools.partial(_dsa_attn_body, tq=tq, tiles_per_step=tiles_per_step),
        grid=(n_tiles,),
        in_specs=[
            pl.BlockSpec((A_WIDTH, tq), lambda i: (0, i)),
            _resident((L, A_WIDTH), lambda i: (0, 0)),
            _resident((A_HEADS, A_HEAD_DIM + ONES_ROWS, L), lambda i: (0, 0, 0)),
            pl.BlockSpec((None, L, tq), lambda i: (i, 0, 0)),
        ],
        out_specs=pl.BlockSpec((tq, A_WIDTH), lambda i: (i, 0)),
        out_shape=jax.ShapeDtypeStruct((L, A_WIDTH), BF16),
        scratch_shapes=[
            pltpu.VMEM((tiles_per_step * A_HEADS, FLASH_TILE, tq), F32),
            pltpu.VMEM((A_HEADS, 1, tq), F32),
            pltpu.VMEM((A_HEADS, A_HEAD_DIM + ONES_ROWS, tq), F32),
        ],
        compiler_params=_cparams(("parallel",)),
        name="dsa_attn",
    )(q_t, k, v_t, bias)


def _diff_attn_body(lq1_ref, lk1_ref, lq2_ref, lk2_ref, g_ref, qt_ref, k_ref, vt_ref, o_ref,
                    s_ref, m_ref, acc_ref, *, tq, lam_init):
    i = pl.program_id(1)
    flash = _Flash(s_ref, m_ref, acc_ref)
    dv = C_V_DIM
    tk = FLASH_TILE
    n_chunks = tq // FLASH_TILE
    n_full = i * n_chunks
    qt = qt_ref[...]
    row = lax.broadcasted_iota(jnp.int32, qt.shape, 0)
    zero = jnp.zeros_like(qt)
    qts = (jnp.where(row < C_QK_DIM, qt, zero), jnp.where(row >= C_QK_DIM, qt, zero))
    kk = lax.broadcasted_iota(jnp.int32, (tk, FLASH_TILE), 0)
    qq = lax.broadcasted_iota(jnp.int32, (tk, FLASH_TILE), 1)
    streams = [(c * n_chunks + qc, qc, qts[c][:, qc * FLASH_TILE:(qc + 1) * FLASH_TILE])
               for qc in range(n_chunks) for c in range(2)]

    def tile_start(jt):
        return pl.multiple_of(jt * tk, tk)

    def cur_side(jt, first_chunk=0):
        vt = vt_ref[:, pl.ds(tile_start(jt), tk)]
        return [(slot, slot, vt) if qc >= first_chunk else None for slot, qc, _ in streams]

    def nxt_side(jt, first_chunk=0, mask_of=lambda qc: None):
        kt = k_ref[pl.ds(tile_start(jt), tk), :]
        return [(slot, kt, q, None, mask_of(qc)) if qc >= first_chunk else None for slot, qc, q in streams]

    idle = [None] * len(streams)

    def first_mask(qc):
        q_pos = (i * n_chunks + qc) * FLASH_TILE + qq
        return (kk >> CHUNK_SHIFT) <= (q_pos >> CHUNK_SHIFT)

    flash.step(idle, nxt_side(0, mask_of=first_mask))

    def body(jt, _):
        flash.step(cur_side(jt), nxt_side(jt + 1))
        return 0

    lax.fori_loop(0, jnp.maximum(n_full - 1, 0), body, 0)

    diag = lambda d: (lambda qc: ((kk >> CHUNK_SHIFT) <= (qq >> CHUNK_SHIFT)) if qc == d else None)

    @pl.when(i > 0)
    def _():
        flash.step(cur_side(n_full - 1), nxt_side(n_full, mask_of=diag(0)))

    for d in range(1, n_chunks):
        flash.step(cur_side(n_full + d - 1, d - 1), nxt_side(n_full + d, d, diag(d)))
    flash.step(cur_side(n_full + n_chunks - 1, n_chunks - 1), idle)

    lam = (jnp.exp(jnp.sum(lq1_ref[...] * lk1_ref[...], axis=1, keepdims=True))
           - jnp.exp(jnp.sum(lq2_ref[...] * lk2_ref[...], axis=1, keepdims=True)) + lam_init)
    for qc in range(n_chunks):
        s0, s1 = qc, n_chunks + qc
        o = (acc_ref[s0, :dv, :] / acc_ref[s0, dv:dv + 1, :]
             - lam * (acc_ref[s1, :dv, :] / acc_ref[s1, dv:dv + 1, :]))
        inv = lax.rsqrt(jnp.mean(o * o, axis=0, keepdims=True) + EPS)
        o_ref[qc * FLASH_TILE:(qc + 1) * FLASH_TILE, :] = (
            jnp.transpose(o * inv) * g_ref[...] * (1.0 - lam_init)).astype(o_ref.dtype)


def _diff_attn(q_t, k, v_t, lam_vecs, sub_gain, layer, lam_init, tq):
    L = k.shape[0]
    vec = lambda w: _resident((None, 1, w), lambda h, i: (layer, 0, 0))
    return pl.pallas_call(
        functools.partial(_diff_attn_body, tq=tq, lam_init=lam_init),
        grid=(C_HEADS, L // tq),
        in_specs=[
            vec(C_QK_DIM), vec(C_QK_DIM), vec(C_QK_DIM), vec(C_QK_DIM), vec(C_V_DIM),
            pl.BlockSpec((C_V_DIM, tq), lambda h, i: (h, i)),
            pl.BlockSpec((L, C_V_DIM), lambda h, i: (0, h)),
            pl.BlockSpec((None, C_V_DIM + ONES_ROWS, L), lambda h, i: (h, 0, 0)),
        ],
        out_specs=pl.BlockSpec((tq, C_V_DIM), lambda h, i: (i, h)),
        out_shape=jax.ShapeDtypeStruct((L, C_WIDTH), BF16),
        scratch_shapes=[
            pltpu.VMEM((2 * (tq // FLASH_TILE), FLASH_TILE, FLASH_TILE), F32),
            pltpu.VMEM((2 * (tq // FLASH_TILE), 1, FLASH_TILE), F32),
            pltpu.VMEM((2 * (tq // FLASH_TILE), C_V_DIM + ONES_ROWS, FLASH_TILE), F32),
        ],
        compiler_params=_cparams(("parallel", "parallel")),
        name="diff_attn",
    )(*lam_vecs, sub_gain, q_t, k, v_t)


def _s5_prep_body(are_ref, aim_ref, ldt_ref, bre_ref, bim_ref, bb_ref, lvl_ref, pw_ref):
    a_re, a_im = are_ref[...], aim_ref[...]
    dt = jnp.exp(ldt_ref[...])
    mag = jnp.exp(a_re * dt)
    lb_re = mag * jnp.cos(a_im * dt)
    lb_im = mag * jnp.sin(a_im * dt)
    den = a_re * a_re + a_im * a_im
    f_re = ((lb_re - 1.0) * a_re + lb_im * a_im) / den
    f_im = (lb_im * a_re - (lb_re - 1.0) * a_im) / den
    b_re, b_im = bre_ref[...], bim_ref[...]
    bb_ref[:, :S5_NSTATE] = (f_re * b_re - f_im * b_im).astype(BF16)
    bb_ref[:, S5_NSTATE:] = (f_re * b_im + f_im * b_re).astype(BF16)

    pows = [(lb_re, lb_im)]
    for _ in range(SUBLANES - 1):
        pr, pi = pows[-1]
        pows.append((pr * lb_re - pi * lb_im, pr * lb_im + pi * lb_re))
    row = lax.broadcasted_iota(jnp.int32, (SUBLANES, S5_NSTATE), 0)
    zero = jnp.zeros((SUBLANES, S5_NSTATE), F32)
    for lvl in range(3):
        sh = 1 << lvl
        pr, pi = pows[sh - 1]
        lvl_ref[lvl, 0] = jnp.where(row >= sh, pr, zero)
        lvl_ref[lvl, 1] = jnp.where(row >= sh, pi, zero)
    cr, ci = zero, zero
    for r in range(SUBLANES):
        cr = jnp.where(row == r, pows[r][0], cr)
        ci = jnp.where(row == r, pows[r][1], ci)
    pw_ref[0] = cr
    pw_ref[1] = ci


def _s5_prep(a_re, a_im, log_dt, b_re_bd, b_im_bd):
    depth = a_re.shape[0]
    vec = pl.BlockSpec((None, 1, S5_NSTATE), lambda l: (l, 0, 0))
    mat = pl.BlockSpec((None, S5_WIDTH, S5_NSTATE), lambda l: (l, 0, 0))
    return pl.pallas_call(
        _s5_prep_body,
        grid=(depth,),
        in_specs=[vec, vec, vec, mat, mat],
        out_specs=[
            pl.BlockSpec((None, S5_WIDTH, 2 * S5_NSTATE), lambda l: (l, 0, 0)),
            pl.BlockSpec((None, 3, 2, SUBLANES, S5_NSTATE), lambda l: (l, 0, 0, 0, 0)),
            pl.BlockSpec((None, 2, SUBLANES, S5_NSTATE), lambda l: (l, 0, 0, 0)),
        ],
        out_shape=[
            jax.ShapeDtypeStruct((depth, S5_WIDTH, 2 * S5_NSTATE), BF16),
            jax.ShapeDtypeStruct((depth, 3, 2, SUBLANES, S5_NSTATE), F32),
            jax.ShapeDtypeStruct((depth, 2, SUBLANES, S5_NSTATE), F32),
        ],
        compiler_params=_cparams(("parallel",)),
        name="s5_prep",
    )(a_re, a_im, log_dt, b_re_bd, b_im_bd)


S5_LANE_CHUNK = 512


def _s5_body(u_ref, bb_ref, lvl_ref, pw_ref, cre_ref, cim_ref, d_ref, wg_ref, bg_ref, o_ref,
             x_ref, carry_ref, *, tl):
    @pl.when(pl.program_id(0) == 0)
    def _():
        carry_ref[...] = jnp.zeros(carry_ref.shape, F32)

    u = u_ref[...]
    x_ref[...] = _dot(u.astype(BF16), bb_ref[...])

    for c0 in range(0, S5_NSTATE, S5_LANE_CHUNK):
        re_sl = slice(c0, c0 + S5_LANE_CHUNK)
        im_sl = slice(S5_NSTATE + c0, S5_NSTATE + c0 + S5_LANE_CHUNK)

        def block(t, carry, re_sl=re_sl, im_sl=im_sl):
            c_re, c_im = carry
            rows = pl.ds(pl.multiple_of(t * SUBLANES, SUBLANES), SUBLANES)
            re = x_ref[rows, re_sl]
            im = x_ref[rows, im_sl]
            for lvl in range(3):
                sh = 1 << lvl
                s_re = pltpu.roll(re, sh, 0)
                s_im = pltpu.roll(im, sh, 0)
                a_re = lvl_ref[lvl, 0, :, re_sl]
                a_im = lvl_ref[lvl, 1, :, re_sl]
                re, im = re + (a_re * s_re - a_im * s_im), im + (a_re * s_im + a_im * s_re)
            p_re = pw_ref[0, :, re_sl]
            p_im = pw_ref[1, :, re_sl]
            re, im = re + (p_re * c_re - p_im * c_im), im + (p_re * c_im + p_im * c_re)
            x_ref[rows, re_sl] = re
            x_ref[rows, im_sl] = im
            last = SUBLANES - 1
            return (jnp.broadcast_to(re[last:, :], re.shape), jnp.broadcast_to(im[last:, :], im.shape))

        carry = lax.fori_loop(0, tl // SUBLANES, block,
                              (carry_ref[0, :, re_sl], carry_ref[1, :, re_sl]))
        carry_ref[0, :, re_sl] = carry[0]
        carry_ref[1, :, re_sl] = carry[1]

    y = (_dot(x_ref[:, :S5_NSTATE].astype(BF16), cre_ref[...])
         - _dot(x_ref[:, S5_NSTATE:].astype(BF16), cim_ref[...])
         + d_ref[...] * u)
    y = 0.5 * y * (1.0 + jnp.tanh(math.sqrt(2.0 / math.pi) * (y + 0.044715 * (y * y * y))))
    z = _dot(y.astype(BF16), wg_ref[...]) + bg_ref[...]
    o_ref[...] = (y * (1.0 / (1.0 + jnp.exp(-z)))).astype(o_ref.dtype)


def _s5_mix(u, bb, lvl, pw, c_re_bd, c_im_bd, d_row, w_glu, b_glu, layer, tl):
    L = u.shape[0]
    lay = lambda *rest: (lambda t: (layer,) + rest)
    return pl.pallas_call(
        functools.partial(_s5_body, tl=tl),
        grid=(L // tl,),
        in_specs=[
            pl.BlockSpec((tl, S5_WIDTH), lambda t: (t, 0)),
            _resident((None, S5_WIDTH, 2 * S5_NSTATE), lay(0, 0)),
            _resident((None, 3, 2, SUBLANES, S5_NSTATE), lay(0, 0, 0, 0)),
            _resident((None, 2, SUBLANES, S5_NSTATE), lay(0, 0, 0)),
            _resident((None, S5_NSTATE, S5_WIDTH), lay(0, 0)),
            _resident((None, S5_NSTATE, S5_WIDTH), lay(0, 0)),
            _resident((None, 1, S5_WIDTH), lay(0, 0)),
            _resident((None, S5_WIDTH, S5_WIDTH), lay(0, 0)),
            _resident((None, 1, S5_WIDTH), lay(0, 0)),
        ],
        out_specs=pl.BlockSpec((tl, S5_WIDTH), lambda t: (t, 0)),
        out_shape=jax.ShapeDtypeStruct((L, S5_WIDTH), BF16),
        scratch_shapes=[
            pltpu.VMEM((tl, 2 * S5_NSTATE), F32),
            pltpu.VMEM((2, SUBLANES, S5_NSTATE), F32),
        ],
        compiler_params=_cparams(("arbitrary",)),
        name="s5_mix",
    )(u, bb, lvl, pw, c_re_bd, c_im_bd, d_row, w_glu, b_glu)


def _outproj_body(x_ref, oa_ref, os_ref, oc_ref, w_ref, o_ref):
    a0, a1, a2 = A_WIDTH, A_WIDTH + S5_WIDTH, A_WIDTH + S5_WIDTH + C_WIDTH
    o_ref[...] = (x_ref[...] + _dot(oa_ref[...], w_ref[:a0, :]) + _dot(os_ref[...], w_ref[a0:a1, :])
                  + _dot(oc_ref[...], w_ref[a1:a2, :]))


def _out_proj(x, o_a, o_s, o_c, w_out, layer, tm):
    L = x.shape[0]
    row = lambda i: (i, 0)
    return pl.pallas_call(
        _outproj_body,
        grid=(L // tm,),
        in_specs=[
            pl.BlockSpec((tm, D_MODEL), row),
            pl.BlockSpec((tm, A_WIDTH), row),
            pl.BlockSpec((tm, S5_WIDTH), row),
            pl.BlockSpec((tm, C_WIDTH), row),
            _resident((None, D_MODEL, D_MODEL), lambda i: (layer, 0, 0)),
        ],
        out_specs=pl.BlockSpec((tm, D_MODEL), row),
        out_shape=jax.ShapeDtypeStruct((L, D_MODEL), F32),
        compiler_params=_cparams(("parallel",)),
        name="out_proj",
    )(x, o_a, o_s, o_c, w_out)


def _norm_matmul_body(x_ref, g_ref, w_ref, o_ref):
    o_ref[...] = _dot(_rms(x_ref[...], g_ref[...]).astype(BF16), w_ref[...]).astype(o_ref.dtype)


def _norm_matmul(x, g, w, layer, tn):
    M, K = x.shape
    N = w.shape[-1]
    return pl.pallas_call(
        _norm_matmul_body,
        grid=(N // tn,),
        in_specs=[
            _resident((M, K), lambda j: (0, 0)),
            _resident((None, 1, K), lambda j: (layer, 0, 0)),
            pl.BlockSpec((None, K, tn), lambda j: (layer, 0, j)),
        ],
        out_specs=pl.BlockSpec((M, tn), lambda j: (0, j)),
        out_shape=jax.ShapeDtypeStruct((M, N), BF16),
        compiler_params=_cparams(("parallel",)),
        name="mem_proj",
    )(x, g, w)


def _xattn_body(x_ref, g_ref, wq_ref, k_ref, v_ref, wo_ref, o_ref):
    x = x_ref[...]
    q = _dot(_rms(x, g_ref[...]).astype(BF16), wq_ref[...]).astype(BF16)
    scale = X_HEAD_DIM ** -0.5
    out = x
    for h in range(X_HEADS):
        hs = slice(h * X_HEAD_DIM, (h + 1) * X_HEAD_DIM)
        s = _dot_nt(q[:, hs], k_ref[:, hs]) * scale
        p = jnp.exp(s - jnp.max(s, axis=1, keepdims=True))
        p = p / jnp.sum(p, axis=1, keepdims=True)
        o_h = _dot(p.astype(BF16), v_ref[:, hs]).astype(BF16)
        out = out + _dot(o_h, wo_ref[hs, :])
    o_ref[...] = out


def _cross_attn(x, g, wq, k_mem, v_mem, wo, layer, tm):
    L = x.shape[0]
    M = k_mem.shape[0]
    row = lambda i: (i, 0)
    return pl.pallas_call(
        _xattn_body,
        grid=(L // tm,),
        in_specs=[
            pl.BlockSpec((tm, D_MODEL), row),
            _resident((None, 1, D_MODEL), lambda i: (layer, 0, 0)),
            _resident((None, D_MODEL, D_MODEL), lambda i: (layer, 0, 0)),
            _resident((M, D_MODEL), lambda i: (0, 0)),
            _resident((M, D_MODEL), lambda i: (0, 0)),
            _resident((None, D_MODEL, D_MODEL), lambda i: (layer, 0, 0)),
        ],
        out_specs=pl.BlockSpec((tm, D_MODEL), row),
        out_shape=jax.ShapeDtypeStruct((L, D_MODEL), F32),
        compiler_params=_cparams(("parallel",)),
        name="cross_attn",
    )(x, g, wq, k_mem, v_mem, wo)


def _mlp_body(x_ref, g_ref, w1_ref, w2_ref, o_ref, hn_ref):
    @pl.when(pl.program_id(1) == 0)
    def _():
        x = x_ref[...]
        hn_ref[...] = _rms(x, g_ref[...]).astype(BF16)
        o_ref[...] = x

    a = jnp.maximum(_dot(hn_ref[...], w1_ref[...]), 0.0)
    o_ref[...] += _dot((a * a).astype(BF16), w2_ref[...])


def _mlp(x, g, w1, w2, layer, tm, tf):
    L = x.shape[0]
    return pl.pallas_call(
        _mlp_body,
        grid=(L // tm, D_FF // tf),
        in_specs=[
            pl.BlockSpec((tm, D_MODEL), lambda i, f: (i, 0)),
            _resident((None, 1, D_MODEL), lambda i, f: (layer, 0, 0)),
            pl.BlockSpec((None, D_MODEL, tf), lambda i, f: (layer, 0, f)),
            pl.BlockSpec((None, tf, D_MODEL), lambda i, f: (layer, f, 0)),
        ],
        out_specs=pl.BlockSpec((tm, D_MODEL), lambda i, f: (i, 0)),
        out_shape=jax.ShapeDtypeStruct((L, D_MODEL), F32),
        scratch_shapes=[pltpu.VMEM((tm, D_MODEL), BF16)],
        compiler_params=_cparams(("parallel", "arbitrary")),
        name="mlp",
    )(x, g, w1, w2)


def _final_norm_body(x_ref, g_ref, o_ref):
    o_ref[...] = _rms(x_ref[...], g_ref[...])


def _final_norm(x, g, tm):
    L = x.shape[0]
    return pl.pallas_call(
        _final_norm_body,
        grid=(L // tm,),
        in_specs=[pl.BlockSpec((tm, D_MODEL), lambda i: (i, 0)),
                  _resident((1, D_MODEL), lambda i: (0, 0))],
        out_specs=pl.BlockSpec((tm, D_MODEL), lambda i: (i, 0)),
        out_shape=jax.ShapeDtypeStruct((L, D_MODEL), F32),
        compiler_params=_cparams(("parallel",)),
        name="final_norm",
    )(x, g)


def _block_diag_in(b):
    eye = jnp.eye(S5_GROUPS, dtype=b.dtype)
    t = jnp.transpose(b, (0, 1, 3, 2))
    bd = t[:, :, :, None, :] * eye[None, :, None, :, None]
    return bd.reshape(b.shape[0], S5_WIDTH, S5_NSTATE)


def _block_diag_out(c):
    eye = jnp.eye(S5_GROUPS, dtype=c.dtype)
    t = jnp.transpose(c, (0, 1, 3, 2))
    bd = t[:, :, :, None, :] * eye[None, :, None, :, None]
    return bd.reshape(c.shape[0], S5_NSTATE, S5_WIDTH)


def _pick_tile(n, want):
    t = min(n, want)
    assert n % t == 0, (n, t)
    return t


def kernel(x, mem, norm_mix, w_in, s5_a_re, s5_a_im, s5_log_dt, s5_b_re, s5_b_im, s5_c_re, s5_c_im, s5_d, s5_w_glu, s5_b_glu, diff_lam_q1, diff_lam_k1, diff_lam_q2, diff_lam_k2, diff_subln, w_out, norm_xattn, norm_mem, xattn_q, xattn_k, xattn_v, xattn_o, norm_mlp, w_ff1, w_ff2, norm_final):
    B, L, _ = x.shape
    assert B == 1
    depth = w_in.shape[0]
    tq = _pick_tile(L, 256)

    pad = jnp.zeros((depth, D_MODEL, KIW_W - IDX_DIM - IDX_HEADS), BF16)
    w_in_p = jnp.concatenate([w_in[..., :IN_SRC_SPLIT].astype(BF16), pad, w_in[..., IN_SRC_SPLIT:].astype(BF16)],
                             axis=-1)
    w_out_b = w_out.astype(BF16)
    wq_b, wk_b, wv_b, wo_b = (w.astype(BF16) for w in (xattn_q, xattn_k, xattn_v, xattn_o))
    w1_b, w2_b = w_ff1.astype(BF16), w_ff2.astype(BF16)
    wg_b = s5_w_glu.astype(BF16)
    row3 = lambda a: a.reshape(depth, 1, -1)

    tabs, tabs_k = _rope_tables(L)

    rep = lambda a: row3(a.astype(F32))
    a_re, a_im = rep(s5_a_re), rep(s5_a_im)
    log_dt = row3(jnp.broadcast_to(s5_log_dt.astype(F32)[:, :, None], (depth, S5_GROUPS, S5_STATE)))
    bb, lvl, pw = _s5_prep(a_re, a_im, log_dt, _block_diag_in(s5_b_re.astype(F32)),
                           _block_diag_in(s5_b_im.astype(F32)))
    c_re_bd = _block_diag_out(s5_c_re).astype(BF16)
    c_im_bd = _block_diag_out(s5_c_im).astype(BF16)
    d_row = row3(s5_d.astype(F32))
    lam_vecs = tuple(row3(v.astype(F32)) for v in (diff_lam_q1, diff_lam_k1, diff_lam_q2, diff_lam_k2))

    xs = x[0]
    mem2 = mem[0]
    for l in range(depth):
        qa, ka, va, qi, ke, ko, wi, us, qc, kc, vc = _in_proj(xs, row3(norm_mix), w_in_p, l, tabs + tabs_k,
                                                               _pick_tile(L, 256))

        w_t = jnp.transpose(wi[:, IDX_DIM:IDX_DIM + IDX_HEADS])
        bias = _dsa_index(jnp.transpose(qi), w_t, ke, ko, tq).reshape(L // tq, L, tq)
        o_a = _dsa_attn(jnp.transpose(qa), ka, _values_t(va, A_HEADS), bias, tq)

        o_s = _s5_mix(us, bb, lvl, pw, c_re_bd, c_im_bd, d_row, wg_b, row3(s5_b_glu.astype(F32)), l,
                      _pick_tile(L, 512))

        lam_init = 0.8 - 0.6 * math.exp(-0.3 * l)
        o_c = _diff_attn(jnp.transpose(qc), kc, _values_t(vc, C_HEADS), lam_vecs, row3(diff_subln.astype(F32)), l,
                         lam_init, _pick_tile(L, 2048))

        xs = _out_proj(xs, o_a, o_s, o_c, w_out_b, l, _pick_tile(L, 512))

        k_mem = _norm_matmul(mem2, row3(norm_mem), wk_b, l, 512)
        v_mem = _norm_matmul(mem2, row3(norm_mem), wv_b, l, 512)
        xs = _cross_attn(xs, row3(norm_xattn), wq_b, k_mem, v_mem, wo_b, l, _pick_tile(L, 512))

        xs = _mlp(xs, row3(norm_mlp), w1_b, w2_b, l, _pick_tile(L, 1024), 512)

    return _final_norm(xs, norm_final.reshape(1, -1), _pick_tile(L, 512))[None]
```

```python
import functools
import math

import jax
import jax.numpy as jnp
from jax import lax
from jax.experimental import pallas as pl
from jax.experimental.pallas import tpu as pltpu

F32 = jnp.float32
BF16 = jnp.bfloat16

D_MODEL = 2048
CHUNK = 64
CHUNK_SHIFT = CHUNK.bit_length() - 1
assert 1 << CHUNK_SHIFT == CHUNK
ROPE_THETA = 10000.0
EPS = 1e-6
NEG = -1e30

A_HEAD_DIM = 128
A_WIDTH = D_MODEL // 4
A_HEADS = A_WIDTH // A_HEAD_DIM
IDX_HEADS = 8
IDX_DIM = 64
TOPK_MAX = 256

S5_GROUP = 16
S5_WIDTH = D_MODEL // 4
S5_GROUPS = S5_WIDTH // S5_GROUP
S5_STATE = 64
S5_NSTATE = S5_GROUPS * S5_STATE

C_QK_DIM = 64
C_V_DIM = 2 * C_QK_DIM
C_WIDTH = D_MODEL // 2
C_HEADS = C_WIDTH // C_V_DIM

X_HEADS = 4
X_HEAD_DIM = D_MODEL // X_HEADS
D_FF = 4 * D_MODEL

LANES = 128
SUBLANES = 8
VMEM_LIMIT = 58 * 1024 * 1024

KIW_W = LANES
OFF_QA = 0
OFF_KA = OFF_QA + A_WIDTH
OFF_VA = OFF_KA + A_WIDTH
OFF_QI = OFF_VA + A_WIDTH
OFF_KIW = OFF_QI + IDX_HEADS * IDX_DIM
OFF_US = OFF_KIW + KIW_W
OFF_QC = OFF_US + S5_WIDTH
OFF_KC = OFF_QC + C_WIDTH
OFF_VC = OFF_KC + C_WIDTH
IN_PACKED = OFF_VC + C_WIDTH
IN_SRC_SPLIT = 3 * A_WIDTH + IDX_HEADS * IDX_DIM + IDX_DIM + IDX_HEADS

LOG2E = math.log2(math.e)
INT_MIN = -2 ** 31
INT_MAX = 2 ** 31 - 1


def _cparams(sem):
    return pltpu.CompilerParams(dimension_semantics=sem, vmem_limit_bytes=VMEM_LIMIT)


def _resident(shape, index_map):
    return pl.BlockSpec(shape, index_map, pipeline_mode=pl.Buffered(1))


def _rms(x, g):
    inv = lax.rsqrt(jnp.mean(x * x, axis=-1, keepdims=True) + EPS)
    return x * inv * g


def _dot(a, b):
    return jnp.dot(a, b, preferred_element_type=F32)


def _dot_nt(a, b):
    return lax.dot_general(a, b, (((1,), (1,)), ((), ())), preferred_element_type=F32)


def _inproj_body(x_ref, g_ref, w_ref, c128_ref, s128_ref, c64_ref, sa64_ref, sb64_ref,
                 c64k_ref, sa64k_ref, sb64k_ref,
                 qa_ref, ka_ref, va_ref, qi_ref, ke_ref, ko_ref, wi_ref, us_ref, qc_ref, kc_ref, vc_ref):
    hn = _rms(x_ref[...], g_ref[...]).astype(BF16)
    c128, s128 = c128_ref[...], s128_ref[...]
    c64, sa64, sb64 = c64_ref[...], sa64_ref[...], sb64_ref[...]

    def rope128(t):
        return t * c128 + pltpu.roll(t, 64, 1) * s128

    def rope64(t):
        return t * c64 + pltpu.roll(t, 96, 1) * sa64 + pltpu.roll(t, 32, 1) * sb64

    def rope64_kiw(t):
        return t * c64k_ref[...] + pltpu.roll(t, 96, 1) * sa64k_ref[...] + pltpu.roll(t, 32, 1) * sb64k_ref[...]

    def emit(off, width, out_ref, fn, scale=None):
        step = 512 if width >= 512 else width
        for c0 in range(0, width, step):
            t = _dot(hn, w_ref[:, off + c0:off + c0 + step])
            for s0 in range(0, step, LANES):
                v = t[:, s0:s0 + LANES]
                if fn is not None:
                    v = fn(v)
                if scale is not None:
                    v = v * scale
                out_ref[:, c0 + s0:c0 + s0 + LANES] = v.astype(out_ref.dtype)

    emit(OFF_QA, A_WIDTH, qa_ref, rope128, A_HEAD_DIM ** -0.5 * LOG2E)
    emit(OFF_KA, A_WIDTH, ka_ref, rope128)
    emit(OFF_VA, A_WIDTH, va_ref, None)
    emit(OFF_QI, IDX_HEADS * IDX_DIM, qi_ref, rope64)
    kiw = rope64_kiw(_dot(hn, w_ref[:, OFF_KIW:OFF_KIW + KIW_W]))
    lane = lax.broadcasted_iota(jnp.int32, kiw.shape, 1)
    k_low = jnp.where(lane < IDX_DIM, kiw, 0.0)
    ke_ref[...] = k_low.astype(BF16)
    ko_ref[...] = pltpu.roll(k_low, IDX_DIM, 1).astype(BF16)
    wi_ref[...] = kiw
    emit(OFF_US, S5_WIDTH, us_ref, None)
    emit(OFF_QC, C_WIDTH, qc_ref, rope64, C_QK_DIM ** -0.5 * LOG2E)
    emit(OFF_KC, C_WIDTH, kc_ref, rope64)
    emit(OFF_VC, C_WIDTH, vc_ref, None)


def _in_proj(x, g, w_packed, layer, tabs, tm):
    L = x.shape[0]
    row = lambda i: (i, 0)
    tab_spec = pl.BlockSpec((tm, LANES), row)
    outs = [
        (A_WIDTH, BF16), (A_WIDTH, BF16), (A_WIDTH, BF16), (IDX_HEADS * IDX_DIM, BF16),
        (KIW_W, BF16), (KIW_W, BF16), (KIW_W, F32),
        (S5_WIDTH, F32), (C_WIDTH, BF16), (C_WIDTH, BF16), (C_WIDTH, BF16),
    ]
    return pl.pallas_call(
        _inproj_body,
        grid=(L // tm,),
        in_specs=[
            pl.BlockSpec((tm, D_MODEL), row),
            _resident((None, 1, D_MODEL), lambda i: (layer, 0, 0)),
            _resident((None, D_MODEL, IN_PACKED), lambda i: (layer, 0, 0)),
        ] + [tab_spec] * 8,
        out_specs=[pl.BlockSpec((tm, w), row) for w, _ in outs],
        out_shape=[jax.ShapeDtypeStruct((L, w), dt) for w, dt in outs],
        compiler_params=_cparams(("parallel",)),
        name="in_proj",
    )(x, g, w_packed, *tabs)


def _rope_tables(L):
    pos = jnp.arange(L, dtype=F32)[:, None]

    def cs(dim):
        inv = ROPE_THETA ** (-jnp.arange(0, dim, 2, dtype=F32) / dim)
        ang = pos * inv[None, :]
        return jnp.cos(ang), jnp.sin(ang)

    c, s = cs(A_HEAD_DIM)
    c128 = jnp.concatenate([c, c], axis=1)
    s128 = jnp.concatenate([-s, s], axis=1)
    c, s = cs(IDX_DIM)
    z = jnp.zeros_like(s)
    c64 = jnp.tile(jnp.concatenate([c, c], axis=1), (1, 2))
    sa64 = jnp.tile(jnp.concatenate([-s, z], axis=1), (1, 2))
    sb64 = jnp.tile(jnp.concatenate([z, s], axis=1), (1, 2))
    lane = jnp.arange(LANES)[None, :]
    c64k = jnp.where(lane < IDX_DIM, c64, 1.0)
    sa64k = jnp.where(lane < IDX_DIM, sa64, 0.0)
    sb64k = jnp.where(lane < IDX_DIM, sb64, 0.0)
    return (c128, s128, c64, sa64, sb64), (c64k, sa64k, sb64k)


COUNT_ROWS = 32
WRITE_TILES_PER_STEP = 4
COARSE_BITS = 16
COARSE_SPAN = 1 << COARSE_BITS
COUNT_TILES_PER_STEP = 4
assert COUNT_TILES_PER_STEP * 256 // COUNT_ROWS <= 256


def _dsa_index_body(qit_ref, wt_ref, ke_ref, ko_ref, bias_ref, skey_ref, coarse_ref, gmax_ref,
                    *, tq, top_k, n_tiles):
    i = pl.program_id(0)
    score_tiles_per_step = 2 if n_tiles % 2 == 0 else 1
    n_live = i + 1
    q_pos = i * tq + lax.broadcasted_iota(jnp.int32, (tq, tq), 1)
    k_off = lax.broadcasted_iota(jnp.int32, (tq, tq), 0)
    wv = wt_ref[...] * ((IDX_HEADS ** -0.5) * (IDX_DIM ** -0.5))
    gmax_ref[...] = jnp.full((tq, tq), INT_MIN, jnp.int32)

    def score_tiles(step, _):
        for jt in [step * score_tiles_per_step + u for u in range(score_tiles_per_step)]:
            rows = pl.ds(pl.multiple_of(jt * tq, tq), tq)
            ke = ke_ref[rows, :]
            ko = ko_ref[rows, :]
            acc = jnp.zeros((tq, tq), F32)
            for hp in range(IDX_HEADS // 2):
                q_pair = qit_ref[hp * LANES:(hp + 1) * LANES, :]
                d0 = _dot(ke, q_pair)
                d1 = _dot(ko, q_pair)
                acc = (acc + wv[2 * hp:2 * hp + 1, :] * jnp.maximum(d0, 0.0)
                       + wv[2 * hp + 1:2 * hp + 2, :] * jnp.maximum(d1, 0.0))
            allowed = ((jt * tq + k_off) >> CHUNK_SHIFT) <= (q_pos >> CHUNK_SHIFT)
            bits = lax.bitcast_convert_type(jnp.where(allowed, acc, NEG), jnp.int32)
            key = bits ^ ((bits >> 31) & jnp.int32(INT_MAX))
            skey_ref[jt] = key
            coarse_ref[jt] = lax.bitcast_convert_type(bits & jnp.int32(-COARSE_SPAN), F32).astype(BF16)
            gmax_ref[...] = jnp.maximum(gmax_ref[...], key)
        return 0

    lax.fori_loop(0, (n_live + score_tiles_per_step - 1) // score_tiles_per_step, score_tiles, 0)

    def count(pred):
        def tiles(first, n, acc):
            for jt in [first + u for u in range(n)]:
                for r in range(0, tq, COUNT_ROWS):
                    acc = acc + jnp.where(pred(skey_ref[jt, r:r + COUNT_ROWS, :]), 1.0, 0.0)
            return acc

        n_groups = n_live // COUNT_TILES_PER_STEP
        acc = lax.fori_loop(0, n_groups, lambda g, a: tiles(g * COUNT_TILES_PER_STEP, COUNT_TILES_PER_STEP, a),
                            jnp.zeros((COUNT_ROWS, tq), F32))
        acc = lax.fori_loop(n_groups * COUNT_TILES_PER_STEP, n_live, lambda jt, a: tiles(jt, 1, a), acc)
        return jnp.sum(acc, axis=0, keepdims=True)

    kf = float(top_k)

    gmax = gmax_ref[...]
    lo0 = jnp.min(gmax, axis=0, keepdims=True)
    hi0 = jnp.max(gmax, axis=0, keepdims=True)

    def narrowing(carry):
        lo, hi = carry
        return jnp.max(jnp.where(hi != lo, 1.0, 0.0)) > 0.0

    def bisect(carry):
        lo, hi = carry
        mid = (lo >> 1) + (hi >> 1) + (((lo & 1) + (hi & 1) + 1) >> 1)
        cnt = count(lambda t: t >= mid)
        ok = cnt >= kf
        hi_new = jnp.where(cnt == kf, mid, jnp.where(ok, hi, mid - 1))
        return jnp.where(ok, mid, lo), hi_new

    at_zero = count(lambda t: t >= 0) >= kf
    above_zero = count(lambda t: t >= 1) >= kf
    lo1 = jnp.where(above_zero, jnp.maximum(lo0, 1), jnp.where(at_zero, 0, lo0))
    hi1 = jnp.where(above_zero, hi0, jnp.where(at_zero, 0, jnp.minimum(hi0, -1)))

    def coarse_count(cand):
        cb = jnp.broadcast_to(cand, (COUNT_ROWS, tq))
        one, zero = jnp.ones((COUNT_ROWS, tq), cb.dtype), jnp.zeros((COUNT_ROWS, tq), cb.dtype)

        def tiles(first, n, acc):
            part = zero
            for jt in [first + u for u in range(n)]:
                for r in range(0, tq, COUNT_ROWS):
                    part = part + jnp.where(coarse_ref[jt, r:r + COUNT_ROWS, :] >= cb, one, zero)
            return acc + part.astype(F32)

        n_groups = n_live // COUNT_TILES_PER_STEP
        acc = lax.fori_loop(0, n_groups, lambda g, a: tiles(g * COUNT_TILES_PER_STEP, COUNT_TILES_PER_STEP, a),
                            jnp.zeros((COUNT_ROWS, tq), F32))
        acc = lax.fori_loop(n_groups * COUNT_TILES_PER_STEP, n_live, lambda jt, a: tiles(jt, 1, a), acc)
        return jnp.sum(acc, axis=0, keepdims=True)

    def coarse_value(p):
        k = (p << COARSE_BITS) | jnp.where(p < 0, COARSE_SPAN - 1, 0)
        return lax.bitcast_convert_type(k ^ ((k >> 31) & jnp.int32(INT_MAX)), F32).astype(BF16)

    def coarse_bisect(carry):
        lo, hi, done = carry
        mid = (lo >> 1) + (hi >> 1) + (((lo & 1) + (hi & 1) + 1) >> 1)
        cnt = coarse_count(coarse_value(mid))
        ok = cnt >= kf
        stop = (cnt == kf) & (mid != 0)
        return (jnp.where(ok, mid, lo), jnp.where(stop, mid, jnp.where(ok, hi, mid - 1)),
                jnp.where(stop, 1, done))

    c16, _, done = lax.while_loop(lambda c: narrowing(c[:2]), coarse_bisect,
                                  (lo1 >> COARSE_BITS, hi1 >> COARSE_BITS, jnp.zeros((1, tq), jnp.int32)))
    base = c16 << COARSE_BITS
    lo2 = jnp.maximum(jnp.where(c16 == 0, -COARSE_SPAN, base), lo1)
    hi2 = jnp.where(done == 1, lo2, jnp.minimum(base + (COARSE_SPAN - 1), hi1))
    lo, _ = lax.while_loop(narrowing, bisect, (lo2, hi2))

    need = kf - count(lambda t: t > lo)
    tri = jnp.where(k_off >= lax.broadcasted_iota(jnp.int32, (tq, tq), 1), 1.0, 0.0).astype(BF16)

    def write_tile(jt, seen, on_diagonal):
        t = skey_ref[jt]
        tied = t == lo
        rank = seen + _dot(tri, jnp.where(tied, 1.0, 0.0).astype(BF16))
        sel = (t > lo) | (tied & (rank <= need))
        if on_diagonal:
            sel = sel & (((jt * tq + k_off) >> CHUNK_SHIFT) <= (q_pos >> CHUNK_SHIFT))
        bias_ref[jt] = jnp.where(sel, 0.0, NEG).astype(BF16)
        return rank[tq - 1:tq, :]

    def write_group(step, seen):
        for u in range(WRITE_TILES_PER_STEP):
            seen = write_tile(step * WRITE_TILES_PER_STEP + u, seen, False)
        return seen

    n_groups = i // WRITE_TILES_PER_STEP
    seen = lax.fori_loop(0, n_groups, write_group, jnp.zeros((1, tq), F32))
    seen = lax.fori_loop(n_groups * WRITE_TILES_PER_STEP, i, lambda jt, s: write_tile(jt, s, False), seen)
    write_tile(i, seen, True)

    def blank_tile(jt, _):
        bias_ref[jt] = jnp.full((tq, tq), NEG, BF16)
        return 0

    lax.fori_loop(n_live, n_tiles, blank_tile, 0)


def _dsa_index(qi_t, w_t, ke, ko, tq):
    L = qi_t.shape[1]
    n_tiles = L // tq
    top_k = min(TOPK_MAX, L // 4)
    assert top_k <= tq
    body = functools.partial(_dsa_index_body, tq=tq, top_k=top_k, n_tiles=n_tiles)
    return pl.pallas_call(
        body,
        grid=(n_tiles,),
        in_specs=[
            pl.BlockSpec((IDX_HEADS * IDX_DIM, tq), lambda i: (0, i)),
            pl.BlockSpec((IDX_HEADS, tq), lambda i: (0, i)),
            _resident((L, LANES), lambda i: (0, 0)),
            _resident((L, LANES), lambda i: (0, 0)),
        ],
        out_specs=pl.BlockSpec((None, n_tiles, tq, tq), lambda i: (i, 0, 0, 0)),
        out_shape=jax.ShapeDtypeStruct((n_tiles, n_tiles, tq, tq), BF16),
        scratch_shapes=[pltpu.VMEM((n_tiles, tq, tq), jnp.int32), pltpu.VMEM((n_tiles, tq, tq), BF16),
                        pltpu.VMEM((tq, tq), jnp.int32)],
        compiler_params=_cparams(("parallel",)),
        name="dsa_index",
    )(qi_t, w_t, ke, ko)


FLASH_TILE = 256
ONES_ROWS = 16


def _values_t(v, heads):
    L = v.shape[0]
    vt = jnp.transpose(v).reshape(heads, -1, L)
    return jnp.concatenate([vt, jnp.ones((heads, ONES_ROWS, L), v.dtype)], axis=1)


class _Flash:
    def __init__(self, s_ref, m_ref, acc_ref):
        self.s_ref, self.m_ref, self.acc_ref = s_ref, m_ref, acc_ref
        m_ref[...] = jnp.full(m_ref.shape, NEG, F32)
        acc_ref[...] = jnp.zeros(acc_ref.shape, F32)

    def step(self, cur, nxt):
        for c, n in zip(cur, nxt):
            if c is not None:
                buf, slot, vt = c
                s = self.s_ref[buf]
                m_prev = self.m_ref[slot]
                m_new = jnp.maximum(m_prev, jnp.max(s, axis=0, keepdims=True))
                alpha = jnp.exp2(m_prev - m_new)
                p = jnp.exp2(s - m_new).astype(BF16)
                self.m_ref[slot] = m_new
            if n is not None:
                buf_n, kt, qt, add, mask = n
                s_n = _dot(kt, qt)
                if add is not None:
                    s_n = s_n + add
                if mask is not None:
                    s_n = jnp.where(mask, s_n, NEG)
                self.s_ref[buf_n] = s_n
            if c is not None:
                self.acc_ref[slot] = alpha * self.acc_ref[slot] + _dot(vt, p)


def _dsa_attn_body(qt_ref, k_ref, vt_ref, bias_ref, o_ref, s_ref, m_ref, acc_ref, *, tq, tiles_per_step):
    i = pl.program_id(0)
    flash = _Flash(s_ref, m_ref, acc_ref)
    tk = FLASH_TILE
    dv = A_HEAD_DIM
    heads = [slice(h * dv, (h + 1) * dv) for h in range(A_HEADS)]

    def key_rows(step, u):
        return pl.ds(pl.multiple_of((step * tiles_per_step + u) * tk, tk), tk)

    def cur_side(step):
        return [(u * A_HEADS + h, h, vt_ref[h, :, key_rows(step, u)])
                for u in range(tiles_per_step) for h in range(A_HEADS)]

    def nxt_side(step):
        out = []
        for u in range(tiles_per_step):
            rows = key_rows(step, u)
            b = bias_ref[rows, :].astype(F32)
            out += [(u * A_HEADS + h, k_ref[rows, hs], qt_ref[hs, :], b, None) for h, hs in enumerate(heads)]
        return out

    idle = [None] * (tiles_per_step * A_HEADS)
    n_live = ((i + 1) * tq + tk - 1) // tk
    n_steps = (n_live + tiles_per_step - 1) // tiles_per_step
    flash.step(idle, nxt_side(0))

    def body(step, _):
        flash.step(cur_side(step), nxt_side(step + 1))
        return 0

    lax.fori_loop(0, n_steps - 1, body, 0)
    flash.step(cur_side(n_steps - 1), idle)
    for h, hs in enumerate(heads):
        o_ref[:, hs] = jnp.transpose(acc_ref[h, :dv, :] / acc_ref[h, dv:dv + 1, :]).astype(o_ref.dtype)


def _dsa_attn(q_t, k, v_t, bias, tq):
    L = k.shape[0]
    n_tiles = L // tq
    key_tiles = L // FLASH_TILE
    tiles_per_step = 4 if key_tiles % 4 == 0 else (2 if key_tiles % 2 == 0 else 1)
    return pl.pallas_call(
        functools.partial(_dsa_attn_body, tq=tq, tiles_per_step=tiles_per_step),
        grid=(n_tiles,),
        in_specs=[
            pl.BlockSpec((A_WIDTH, tq), lambda i: (0, i)),
            _resident((L, A_WIDTH), lambda i: (0, 0)),
            _resident((A_HEADS, A_HEAD_DIM + ONES_ROWS, L), lambda i: (0, 0, 0)),
            pl.BlockSpec((None, L, tq), lambda i: (i, 0, 0)),
        ],
        out_specs=pl.BlockSpec((tq, A_WIDTH), lambda i: (i, 0)),
        out_shape=jax.ShapeDtypeStruct((L, A_WIDTH), BF16),
        scratch_shapes=[
            pltpu.VMEM((tiles_per_step * A_HEADS, FLASH_TILE, tq), F32),
            pltpu.VMEM((A_HEADS, 1, tq), F32),
            pltpu.VMEM((A_HEADS, A_HEAD_DIM + ONES_ROWS, tq), F32),
        ],
        compiler_params=_cparams(("parallel",)),
        name="dsa_attn",
    )(q_t, k, v_t, bias)


def _diff_attn_body(lq1_ref, lk1_ref, lq2_ref, lk2_ref, g_ref, qt_ref, k_ref, vt_ref, o_ref,
                    s_ref, m_ref, acc_ref, *, tq, lam_init):
    i = pl.program_id(1)
    flash = _Flash(s_ref, m_ref, acc_ref)
    dv = C_V_DIM
    tk = FLASH_TILE
    n_chunks = tq // FLASH_TILE
    n_full = i * n_chunks
    qt = qt_ref[...]
    row = lax.broadcasted_iota(jnp.int32, qt.shape, 0)
    zero = jnp.zeros_like(qt)
    qts = (jnp.where(row < C_QK_DIM, qt, zero), jnp.where(row >= C_QK_DIM, qt, zero))
    kk = lax.broadcasted_iota(jnp.int32, (tk, FLASH_TILE), 0)
    qq = lax.broadcasted_iota(jnp.int32, (tk, FLASH_TILE), 1)
    streams = [(c * n_chunks + qc, qc, qts[c][:, qc * FLASH_TILE:(qc + 1) * FLASH_TILE])
               for qc in range(n_chunks) for c in range(2)]

    def tile_start(jt):
        return pl.multiple_of(jt * tk, tk)

    def cur_side(jt, first_chunk=0):
        vt = vt_ref[:, pl.ds(tile_start(jt), tk)]
        return [(slot, slot, vt) if qc >= first_chunk else None for slot, qc, _ in streams]

    def nxt_side(jt, first_chunk=0, mask_of=lambda qc: None):
        kt = k_ref[pl.ds(tile_start(jt), tk), :]
        return [(slot, kt, q, None, mask_of(qc)) if qc >= first_chunk else None for slot, qc, q in streams]

    idle = [None] * len(streams)

    def first_mask(qc):
        q_pos = (i * n_chunks + qc) * FLASH_TILE + qq
        return (kk >> CHUNK_SHIFT) <= (q_pos >> CHUNK_SHIFT)

    flash.step(idle, nxt_side(0, mask_of=first_mask))

    def body(jt, _):
        flash.step(cur_side(jt), nxt_side(jt + 1))
        return 0

    lax.fori_loop(0, jnp.maximum(n_full - 1, 0), body, 0)

    diag = lambda d: (lambda qc: ((kk >> CHUNK_SHIFT) <= (qq >> CHUNK_SHIFT)) if qc == d else None)

    @pl.when(i > 0)
    def _():
        flash.step(cur_side(n_full - 1), nxt_side(n_full, mask_of=diag(0)))

    for d in range(1, n_chunks):
        flash.step(cur_side(n_full + d - 1, d - 1), nxt_side(n_full + d, d, diag(d)))
    flash.step(cur_side(n_full + n_chunks - 1, n_chunks - 1), idle)

    lam = (jnp.exp(jnp.sum(lq1_ref[...] * lk1_ref[...], axis=1, keepdims=True))
           - jnp.exp(jnp.sum(lq2_ref[...] * lk2_ref[...], axis=1, keepdims=True)) + lam_init)
    for qc in range(n_chunks):
        s0, s1 = qc, n_chunks + qc
        o = (acc_ref[s0, :dv, :] / acc_ref[s0, dv:dv + 1, :]
             - lam * (acc_ref[s1, :dv, :] / acc_ref[s1, dv:dv + 1, :]))
        inv = lax.rsqrt(jnp.mean(o * o, axis=0, keepdims=True) + EPS)
        o_ref[qc * FLASH_TILE:(qc + 1) * FLASH_TILE, :] = (
            jnp.transpose(o * inv) * g_ref[...] * (1.0 - lam_init)).astype(o_ref.dtype)


def _diff_attn(q_t, k, v_t, lam_vecs, sub_gain, layer, lam_init, tq):
    L = k.shape[0]
    vec = lambda w: _resident((None, 1, w), lambda h, i: (layer, 0, 0))
    return pl.pallas_call(
        functools.partial(_diff_attn_body, tq=tq, lam_init=lam_init),
        grid=(C_HEADS, L // tq),
        in_specs=[
            vec(C_QK_DIM), vec(C_QK_DIM), vec(C_QK_DIM), vec(C_QK_DIM), vec(C_V_DIM),
            pl.BlockSpec((C_V_DIM, tq), lambda h, i: (h, i)),
            pl.BlockSpec((L, C_V_DIM), lambda h, i: (0, h)),
            pl.BlockSpec((None, C_V_DIM + ONES_ROWS, L), lambda h, i: (h, 0, 0)),
        ],
        out_specs=pl.BlockSpec((tq, C_V_DIM), lambda h, i: (i, h)),
        out_shape=jax.ShapeDtypeStruct((L, C_WIDTH), BF16),
        scratch_shapes=[
            pltpu.VMEM((2 * (tq // FLASH_TILE), FLASH_TILE, FLASH_TILE), F32),
            pltpu.VMEM((2 * (tq // FLASH_TILE), 1, FLASH_TILE), F32),
            pltpu.VMEM((2 * (tq // FLASH_TILE), C_V_DIM + ONES_ROWS, FLASH_TILE), F32),
        ],
        compiler_params=_cparams(("parallel", "parallel")),
        name="diff_attn",
    )(*lam_vecs, sub_gain, q_t, k, v_t)


def _s5_prep_body(are_ref, aim_ref, ldt_ref, bre_ref, bim_ref, bb_ref, lvl_ref, pw_ref):
    a_re, a_im = are_ref[...], aim_ref[...]
    dt = jnp.exp(ldt_ref[...])
    mag = jnp.exp(a_re * dt)
    lb_re = mag * jnp.cos(a_im * dt)
    lb_im = mag * jnp.sin(a_im * dt)
    den = a_re * a_re + a_im * a_im
    f_re = ((lb_re - 1.0) * a_re + lb_im * a_im) / den
    f_im = (lb_im * a_re - (lb_re - 1.0) * a_im) / den
    b_re, b_im = bre_ref[...], bim_ref[...]
    bb_ref[:, :S5_NSTATE] = (f_re * b_re - f_im * b_im).astype(BF16)
    bb_ref[:, S5_NSTATE:] = (f_re * b_im + f_im * b_re).astype(BF16)

    pows = [(lb_re, lb_im)]
    for _ in range(SUBLANES - 1):
        pr, pi = pows[-1]
        pows.append((pr * lb_re - pi * lb_im, pr * lb_im + pi * lb_re))
    row = lax.broadcasted_iota(jnp.int32, (SUBLANES, S5_NSTATE), 0)
    zero = jnp.zeros((SUBLANES, S5_NSTATE), F32)
    for lvl in range(3):
        sh = 1 << lvl
        pr, pi = pows[sh - 1]
        lvl_ref[lvl, 0] = jnp.where(row >= sh, pr, zero)
        lvl_ref[lvl, 1] = jnp.where(row >= sh, pi, zero)
    cr, ci = zero, zero
    for r in range(SUBLANES):
        cr = jnp.where(row == r, pows[r][0], cr)
        ci = jnp.where(row == r, pows[r][1], ci)
    pw_ref[0] = cr
    pw_ref[1] = ci


def _s5_prep(a_re, a_im, log_dt, b_re_bd, b_im_bd):
    depth = a_re.shape[0]
    vec = pl.BlockSpec((None, 1, S5_NSTATE), lambda l: (l, 0, 0))
    mat = pl.BlockSpec((None, S5_WIDTH, S5_NSTATE), lambda l: (l, 0, 0))
    return pl.pallas_call(
        _s5_prep_body,
        grid=(depth,),
        in_specs=[vec, vec, vec, mat, mat],
        out_specs=[
            pl.BlockSpec((None, S5_WIDTH, 2 * S5_NSTATE), lambda l: (l, 0, 0)),
            pl.BlockSpec((None, 3, 2, SUBLANES, S5_NSTATE), lambda l: (l, 0, 0, 0, 0)),
            pl.BlockSpec((None, 2, SUBLANES, S5_NSTATE), lambda l: (l, 0, 0, 0)),
        ],
        out_shape=[
            jax.ShapeDtypeStruct((depth, S5_WIDTH, 2 * S5_NSTATE), BF16),
            jax.ShapeDtypeStruct((depth, 3, 2, SUBLANES, S5_NSTATE), F32),
            jax.ShapeDtypeStruct((depth, 2, SUBLANES, S5_NSTATE), F32),
        ],
        compiler_params=_cparams(("parallel",)),
        name="s5_prep",
    )(a_re, a_im, log_dt, b_re_bd, b_im_bd)


S5_LANE_CHUNK = 512


def _s5_body(u_ref, bb_ref, lvl_ref, pw_ref, cre_ref, cim_ref, d_ref, wg_ref, bg_ref, o_ref,
             x_ref, carry_ref, *, tl):
    @pl.when(pl.program_id(0) == 0)
    def _():
        carry_ref[...] = jnp.zeros(carry_ref.shape, F32)

    u = u_ref[...]
    x_ref[...] = _dot(u.astype(BF16), bb_ref[...])

    for c0 in range(0, S5_NSTATE, S5_LANE_CHUNK):
        re_sl = slice(c0, c0 + S5_LANE_CHUNK)
        im_sl = slice(S5_NSTATE + c0, S5_NSTATE + c0 + S5_LANE_CHUNK)

        def block(t, carry, re_sl=re_sl, im_sl=im_sl):
            c_re, c_im = carry
            rows = pl.ds(pl.multiple_of(t * SUBLANES, SUBLANES), SUBLANES)
            re = x_ref[rows, re_sl]
            im = x_ref[rows, im_sl]
            for lvl in range(3):
                sh = 1 << lvl
                s_re = pltpu.roll(re, sh, 0)
                s_im = pltpu.roll(im, sh, 0)
                a_re = lvl_ref[lvl, 0, :, re_sl]
                a_im = lvl_ref[lvl, 1, :, re_sl]
                re, im = re + (a_re * s_re - a_im * s_im), im + (a_re * s_im + a_im * s_re)
            p_re = pw_ref[0, :, re_sl]
            p_im = pw_ref[1, :, re_sl]
            re, im = re + (p_re * c_re - p_im * c_im), im + (p_re * c_im + p_im * c_re)
            x_ref[rows, re_sl] = re
            x_ref[rows, im_sl] = im
            last = SUBLANES - 1
            return (jnp.broadcast_to(re[last:, :], re.shape), jnp.broadcast_to(im[last:, :], im.shape))

        carry = lax.fori_loop(0, tl // SUBLANES, block,
                              (carry_ref[0, :, re_sl], carry_ref[1, :, re_sl]))
        carry_ref[0, :, re_sl] = carry[0]
        carry_ref[1, :, re_sl] = carry[1]

    y = (_dot(x_ref[:, :S5_NSTATE].astype(BF16), cre_ref[...])
         - _dot(x_ref[:, S5_NSTATE:].astype(BF16), cim_ref[...])
         + d_ref[...] * u)
    y = 0.5 * y * (1.0 + jnp.tanh(math.sqrt(2.0 / math.pi) * (y + 0.044715 * (y * y * y))))
    z = _dot(y.astype(BF16), wg_ref[...]) + bg_ref[...]
    o_ref[...] = (y * (1.0 / (1.0 + jnp.exp(-z)))).astype(o_ref.dtype)


def _s5_mix(u, bb, lvl, pw, c_re_bd, c_im_bd, d_row, w_glu, b_glu, layer, tl):
    L = u.shape[0]
    lay = lambda *rest: (lambda t: (layer,) + rest)
    return pl.pallas_call(
        functools.partial(_s5_body, tl=tl),
        grid=(L // tl,),
        in_specs=[
            pl.BlockSpec((tl, S5_WIDTH), lambda t: (t, 0)),
            _resident((None, S5_WIDTH, 2 * S5_NSTATE), lay(0, 0)),
            _resident((None, 3, 2, SUBLANES, S5_NSTATE), lay(0, 0, 0, 0)),
            _resident((None, 2, SUBLANES, S5_NSTATE), lay(0, 0, 0)),
            _resident((None, S5_NSTATE, S5_WIDTH), lay(0, 0)),
            _resident((None, S5_NSTATE, S5_WIDTH), lay(0, 0)),
            _resident((None, 1, S5_WIDTH), lay(0, 0)),
            _resident((None, S5_WIDTH, S5_WIDTH), lay(0, 0)),
            _resident((None, 1, S5_WIDTH), lay(0, 0)),
        ],
        out_specs=pl.BlockSpec((tl, S5_WIDTH), lambda t: (t, 0)),
        out_shape=jax.ShapeDtypeStruct((L, S5_WIDTH), BF16),
        scratch_shapes=[
            pltpu.VMEM((tl, 2 * S5_NSTATE), F32),
            pltpu.VMEM((2, SUBLANES, S5_NSTATE), F32),
        ],
        compiler_params=_cparams(("arbitrary",)),
        name="s5_mix",
    )(u, bb, lvl, pw, c_re_bd, c_im_bd, d_row, w_glu, b_glu)


def _outproj_body(x_ref, oa_ref, os_ref, oc_ref, w_ref, o_ref):
    a0, a1, a2 = A_WIDTH, A_WIDTH + S5_WIDTH, A_WIDTH + S5_WIDTH + C_WIDTH
    o_ref[...] = (x_ref[...] + _dot(oa_ref[...], w_ref[:a0, :]) + _dot(os_ref[...], w_ref[a0:a1, :])
                  + _dot(oc_ref[...], w_ref[a1:a2, :]))


def _out_proj(x, o_a, o_s, o_c, w_out, layer, tm):
    L = x.shape[0]
    row = lambda i: (i, 0)
    return pl.pallas_call(
        _outproj_body,
        grid=(L // tm,),
        in_specs=[
            pl.BlockSpec((tm, D_MODEL), row),
            pl.BlockSpec((tm, A_WIDTH), row),
            pl.BlockSpec((tm, S5_WIDTH), row),
            pl.BlockSpec((tm, C_WIDTH), row),
            _resident((None, D_MODEL, D_MODEL), lambda i: (layer, 0, 0)),
        ],
        out_specs=pl.BlockSpec((tm, D_MODEL), row),
        out_shape=jax.ShapeDtypeStruct((L, D_MODEL), F32),
        compiler_params=_cparams(("parallel",)),
        name="out_proj",
    )(x, o_a, o_s, o_c, w_out)


def _norm_matmul_body(x_ref, g_ref, w_ref, o_ref):
    o_ref[...] = _dot(_rms(x_ref[...], g_ref[...]).astype(BF16), w_ref[...]).astype(o_ref.dtype)


def _norm_matmul(x, g, w, layer, tn):
    M, K = x.shape
    N = w.shape[-1]
    return pl.pallas_call(
        _norm_matmul_body,
        grid=(N // tn,),
        in_specs=[
            _resident((M, K), lambda j: (0, 0)),
            _resident((None, 1, K), lambda j: (layer, 0, 0)),
            pl.BlockSpec((None, K, tn), lambda j: (layer, 0, j)),
        ],
        out_specs=pl.BlockSpec((M, tn), lambda j: (0, j)),
        out_shape=jax.ShapeDtypeStruct((M, N), BF16),
        compiler_params=_cparams(("parallel",)),
        name="mem_proj",
    )(x, g, w)


def _xattn_body(x_ref, g_ref, wq_ref, k_ref, v_ref, wo_ref, o_ref):
    x = x_ref[...]
    q = _dot(_rms(x, g_ref[...]).astype(BF16), wq_ref[...]).astype(BF16)
    scale = X_HEAD_DIM ** -0.5
    out = x
    for h in range(X_HEADS):
        hs = slice(h * X_HEAD_DIM, (h + 1) * X_HEAD_DIM)
        s = _dot_nt(q[:, hs], k_ref[:, hs]) * scale
        p = jnp.exp(s - jnp.max(s, axis=1, keepdims=True))
        p = p / jnp.sum(p, axis=1, keepdims=True)
        o_h = _dot(p.astype(BF16), v_ref[:, hs]).astype(BF16)
        out = out + _dot(o_h, wo_ref[hs, :])
    o_ref[...] = out


def _cross_attn(x, g, wq, k_mem, v_mem, wo, layer, tm):
    L = x.shape[0]
    M = k_mem.shape[0]
    row = lambda i: (i, 0)
    return pl.pallas_call(
        _xattn_body,
        grid=(L // tm,),
        in_specs=[
            pl.BlockSpec((tm, D_MODEL), row),
            _resident((None, 1, D_MODEL), lambda i: (layer, 0, 0)),
            _resident((None, D_MODEL, D_MODEL), lambda i: (layer, 0, 0)),
            _resident((M, D_MODEL), lambda i: (0, 0)),
            _resident((M, D_MODEL), lambda i: (0, 0)),
            _resident((None, D_MODEL, D_MODEL), lambda i: (layer, 0, 0)),
        ],
        out_specs=pl.BlockSpec((tm, D_MODEL), row),
        out_shape=jax.ShapeDtypeStruct((L, D_MODEL), F32),
        compiler_params=_cparams(("parallel",)),
        name="cross_attn",
    )(x, g, wq, k_mem, v_mem, wo)


def _mlp_body(x_ref, g_ref, w1_ref, w2_ref, gout_ref, o_ref, hn_ref, *, norm_output):
    f = pl.program_id(1)

    @pl.when(f == 0)
    def _():
        x = x_ref[...]
        hn_ref[...] = _rms(x, g_ref[...]).astype(BF16)
        o_ref[...] = x

    a = jnp.maximum(_dot(hn_ref[...], w1_ref[...]), 0.0)
    o_ref[...] += _dot((a * a).astype(BF16), w2_ref[...])

    if norm_output:
        @pl.when(f == pl.num_programs(1) - 1)
        def _():
            o_ref[...] = _rms(o_ref[...], gout_ref[...])


def _mlp(x, g, w1, w2, g_out, layer, tm, tf, norm_output):
    L = x.shape[0]
    return pl.pallas_call(
        functools.partial(_mlp_body, norm_output=norm_output),
        grid=(L // tm, D_FF // tf),
        in_specs=[
            pl.BlockSpec((tm, D_MODEL), lambda i, f: (i, 0)),
            _resident((None, 1, D_MODEL), lambda i, f: (layer, 0, 0)),
            pl.BlockSpec((None, D_MODEL, tf), lambda i, f: (layer, 0, f)),
            pl.BlockSpec((None, tf, D_MODEL), lambda i, f: (layer, f, 0)),
            _resident((1, D_MODEL), lambda i, f: (0, 0)),
        ],
        out_specs=pl.BlockSpec((tm, D_MODEL), lambda i, f: (i, 0)),
        out_shape=jax.ShapeDtypeStruct((L, D_MODEL), F32),
        scratch_shapes=[pltpu.VMEM((tm, D_MODEL), BF16)],
        compiler_params=_cparams(("parallel", "arbitrary")),
        name="mlp",
    )(x, g, w1, w2, g_out)


def _block_diag_in(b):
    eye = jnp.eye(S5_GROUPS, dtype=b.dtype)
    t = jnp.transpose(b, (0, 1, 3, 2))
    bd = t[:, :, :, None, :] * eye[None, :, None, :, None]
    return bd.reshape(b.shape[0], S5_WIDTH, S5_NSTATE)


def _block_diag_out(c):
    eye = jnp.eye(S5_GROUPS, dtype=c.dtype)
    t = jnp.transpose(c, (0, 1, 3, 2))
    bd = t[:, :, :, None, :] * eye[None, :, None, :, None]
    return bd.reshape(c.shape[0], S5_NSTATE, S5_WIDTH)


def _pick_tile(n, want):
    t = min(n, want)
    assert n % t == 0, (n, t)
    return t


def kernel(x, mem, norm_mix, w_in, s5_a_re, s5_a_im, s5_log_dt, s5_b_re, s5_b_im, s5_c_re, s5_c_im, s5_d, s5_w_glu, s5_b_glu, diff_lam_q1, diff_lam_k1, diff_lam_q2, diff_lam_k2, diff_subln, w_out, norm_xattn, norm_mem, xattn_q, xattn_k, xattn_v, xattn_o, norm_mlp, w_ff1, w_ff2, norm_final):
    B, L, _ = x.shape
    assert B == 1
    depth = w_in.shape[0]
    tq = _pick_tile(L, 256)

    pad = jnp.zeros((depth, D_MODEL, KIW_W - IDX_DIM - IDX_HEADS), BF16)
    w_in_p = jnp.concatenate([w_in[..., :IN_SRC_SPLIT].astype(BF16), pad, w_in[..., IN_SRC_SPLIT:].astype(BF16)],
                             axis=-1)
    w_out_b = w_out.astype(BF16)
    wq_b, wk_b, wv_b, wo_b = (w.astype(BF16) for w in (xattn_q, xattn_k, xattn_v, xattn_o))
    w1_b, w2_b = w_ff1.astype(BF16), w_ff2.astype(BF16)
    wg_b = s5_w_glu.astype(BF16)
    row3 = lambda a: a.reshape(depth, 1, -1)

    tabs, tabs_k = _rope_tables(L)

    rep = lambda a: row3(a.astype(F32))
    a_re, a_im = rep(s5_a_re), rep(s5_a_im)
    log_dt = row3(jnp.broadcast_to(s5_log_dt.astype(F32)[:, :, None], (depth, S5_GROUPS, S5_STATE)))
    bb, lvl, pw = _s5_prep(a_re, a_im, log_dt, _block_diag_in(s5_b_re.astype(F32)),
                           _block_diag_in(s5_b_im.astype(F32)))
    c_re_bd = _block_diag_out(s5_c_re).astype(BF16)
    c_im_bd = _block_diag_out(s5_c_im).astype(BF16)
    d_row = row3(s5_d.astype(F32))
    lam_vecs = tuple(row3(v.astype(F32)) for v in (diff_lam_q1, diff_lam_k1, diff_lam_q2, diff_lam_k2))

    xs = x[0]
    mem2 = mem[0]
    for l in range(depth):
        qa, ka, va, qi, ke, ko, wi, us, qc, kc, vc = _in_proj(xs, row3(norm_mix), w_in_p, l, tabs + tabs_k,
                                                               _pick_tile(L, 256))

        w_t = jnp.transpose(wi[:, IDX_DIM:IDX_DIM + IDX_HEADS])
        bias = _dsa_index(jnp.transpose(qi), w_t, ke, ko, tq).reshape(L // tq, L, tq)
        o_a = _dsa_attn(jnp.transpose(qa), ka, _values_t(va, A_HEADS), bias, tq)

        o_s = _s5_mix(us, bb, lvl, pw, c_re_bd, c_im_bd, d_row, wg_b, row3(s5_b_glu.astype(F32)), l,
                      _pick_tile(L, 512))

        lam_init = 0.8 - 0.6 * math.exp(-0.3 * l)
        o_c = _diff_attn(jnp.transpose(qc), kc, _values_t(vc, C_HEADS), lam_vecs, row3(diff_subln.astype(F32)), l,
                         lam_init, _pick_tile(L, 2048))

        xs = _out_proj(xs, o_a, o_s, o_c, w_out_b, l, _pick_tile(L, 512))

        k_mem = _norm_matmul(mem2, row3(norm_mem), wk_b, l, 512)
        v_mem = _norm_matmul(mem2, row3(norm_mem), wv_b, l, 512)
        xs = _cross_attn(xs, row3(norm_xattn), wq_b, k_mem, v_mem, wo_b, l, _pick_tile(L, 512))

        xs = _mlp(xs, row3(norm_mlp), w1_b, w2_b, norm_final.reshape(1, -1), l, _pick_tile(L, 1024), 512,
                  norm_output=(l == depth - 1))

    return xs[None]
```

```python
import functools
import math

import jax
import jax.numpy as jnp
from jax import lax
from jax.experimental import pallas as pl
from jax.experimental.pallas import tpu as pltpu

F32 = jnp.float32
BF16 = jnp.bfloat16

D_MODEL = 2048
CHUNK = 64
CHUNK_SHIFT = CHUNK.bit_length() - 1
assert 1 << CHUNK_SHIFT == CHUNK
ROPE_THETA = 10000.0
EPS = 1e-6
NEG = -1e30

A_HEAD_DIM = 128
A_WIDTH = D_MODEL // 4
A_HEADS = A_WIDTH // A_HEAD_DIM
IDX_HEADS = 8
IDX_DIM = 64
TOPK_MAX = 256

S5_GROUP = 16
S5_WIDTH = D_MODEL // 4
S5_GROUPS = S5_WIDTH // S5_GROUP
S5_STATE = 64
S5_NSTATE = S5_GROUPS * S5_STATE

C_QK_DIM = 64
C_V_DIM = 2 * C_QK_DIM
C_WIDTH = D_MODEL // 2
C_HEADS = C_WIDTH // C_V_DIM

X_HEADS = 4
X_HEAD_DIM = D_MODEL // X_HEADS
D_FF = 4 * D_MODEL

LANES = 128
SUBLANES = 8
VMEM_LIMIT = 58 * 1024 * 1024

KIW_W = LANES
OFF_QA = 0
OFF_KA = OFF_QA + A_WIDTH
OFF_VA = OFF_KA + A_WIDTH
OFF_QI = OFF_VA + A_WIDTH
OFF_KIW = OFF_QI + IDX_HEADS * IDX_DIM
OFF_US = OFF_KIW + KIW_W
OFF_QC = OFF_US + S5_WIDTH
OFF_KC = OFF_QC + C_WIDTH
OFF_VC = OFF_KC + C_WIDTH
IN_PACKED = OFF_VC + C_WIDTH
IN_SRC_SPLIT = 3 * A_WIDTH + IDX_HEADS * IDX_DIM + IDX_DIM + IDX_HEADS

LOG2E = math.log2(math.e)
INT_MIN = -2 ** 31
INT_MAX = 2 ** 31 - 1


def _cparams(sem):
    return pltpu.CompilerParams(dimension_semantics=sem, vmem_limit_bytes=VMEM_LIMIT)


def _resident(shape, index_map):
    return pl.BlockSpec(shape, index_map, pipeline_mode=pl.Buffered(1))


def _rms(x, g):
    inv = lax.rsqrt(jnp.mean(x * x, axis=-1, keepdims=True) + EPS)
    return x * inv * g


def _dot(a, b):
    return jnp.dot(a, b, preferred_element_type=F32)


def _dot_nt(a, b):
    return lax.dot_general(a, b, (((1,), (1,)), ((), ())), preferred_element_type=F32)


def _inproj_body(x_ref, g_ref, w_ref, c128_ref, s128_ref, c64_ref, sa64_ref, sb64_ref,
                 c64k_ref, sa64k_ref, sb64k_ref,
                 qa_ref, ka_ref, va_ref, qi_ref, ke_ref, ko_ref, wi_ref, us_ref, qc_ref, kc_ref, vc_ref):
    hn = _rms(x_ref[...], g_ref[...]).astype(BF16)
    c128, s128 = c128_ref[...], s128_ref[...]
    c64, sa64, sb64 = c64_ref[...], sa64_ref[...], sb64_ref[...]

    def rope128(t):
        return t * c128 + pltpu.roll(t, 64, 1) * s128

    def rope64(t):
        return t * c64 + pltpu.roll(t, 96, 1) * sa64 + pltpu.roll(t, 32, 1) * sb64

    def rope64_kiw(t):
        return t * c64k_ref[...] + pltpu.roll(t, 96, 1) * sa64k_ref[...] + pltpu.roll(t, 32, 1) * sb64k_ref[...]

    def emit(off, width, out_ref, fn, scale=None):
        step = 512 if width >= 512 else width
        for c0 in range(0, width, step):
            t = _dot(hn, w_ref[:, off + c0:off + c0 + step])
            for s0 in range(0, step, LANES):
                v = t[:, s0:s0 + LANES]
                if fn is not None:
                    v = fn(v)
                if scale is not None:
                    v = v * scale
                out_ref[:, c0 + s0:c0 + s0 + LANES] = v.astype(out_ref.dtype)

    emit(OFF_QA, A_WIDTH, qa_ref, rope128, A_HEAD_DIM ** -0.5 * LOG2E)
    emit(OFF_KA, A_WIDTH, ka_ref, rope128)
    emit(OFF_VA, A_WIDTH, va_ref, None)
    emit(OFF_QI, IDX_HEADS * IDX_DIM, qi_ref, rope64)
    kiw = rope64_kiw(_dot(hn, w_ref[:, OFF_KIW:OFF_KIW + KIW_W]))
    lane = lax.broadcasted_iota(jnp.int32, kiw.shape, 1)
    k_low = jnp.where(lane < IDX_DIM, kiw, 0.0)
    ke_ref[...] = k_low.astype(BF16)
    ko_ref[...] = pltpu.roll(k_low, IDX_DIM, 1).astype(BF16)
    wi_ref[...] = kiw
    emit(OFF_US, S5_WIDTH, us_ref, None)
    emit(OFF_QC, C_WIDTH, qc_ref, rope64, C_QK_DIM ** -0.5 * LOG2E)
    emit(OFF_KC, C_WIDTH, kc_ref, rope64)
    emit(OFF_VC, C_WIDTH, vc_ref, None)


def _in_proj(x, g, w_packed, layer, tabs, tm):
    L = x.shape[0]
    row = lambda i: (i, 0)
    tab_spec = pl.BlockSpec((tm, LANES), row)
    outs = [
        (A_WIDTH, BF16), (A_WIDTH, BF16), (A_WIDTH, BF16), (IDX_HEADS * IDX_DIM, BF16),
        (KIW_W, BF16), (KIW_W, BF16), (KIW_W, F32),
        (S5_WIDTH, F32), (C_WIDTH, BF16), (C_WIDTH, BF16), (C_WIDTH, BF16),
    ]
    return pl.pallas_call(
        _inproj_body,
        grid=(L // tm,),
        in_specs=[
            pl.BlockSpec((tm, D_MODEL), row),
            _resident((None, 1, D_MODEL), lambda i: (layer, 0, 0)),
            _resident((None, D_MODEL, IN_PACKED), lambda i: (layer, 0, 0)),
        ] + [tab_spec] * 8,
        out_specs=[pl.BlockSpec((tm, w), row) for w, _ in outs],
        out_shape=[jax.ShapeDtypeStruct((L, w), dt) for w, dt in outs],
        compiler_params=_cparams(("parallel",)),
        name="in_proj",
    )(x, g, w_packed, *tabs)


def _rope_tables(L):
    pos = jnp.arange(L, dtype=F32)[:, None]

    def cs(dim):
        inv = ROPE_THETA ** (-jnp.arange(0, dim, 2, dtype=F32) / dim)
        ang = pos * inv[None, :]
        return jnp.cos(ang), jnp.sin(ang)

    c, s = cs(A_HEAD_DIM)
    c128 = jnp.concatenate([c, c], axis=1)
    s128 = jnp.concatenate([-s, s], axis=1)
    c, s = cs(IDX_DIM)
    z = jnp.zeros_like(s)
    c64 = jnp.tile(jnp.concatenate([c, c], axis=1), (1, 2))
    sa64 = jnp.tile(jnp.concatenate([-s, z], axis=1), (1, 2))
    sb64 = jnp.tile(jnp.concatenate([z, s], axis=1), (1, 2))
    lane = jnp.arange(LANES)[None, :]
    c64k = jnp.where(lane < IDX_DIM, c64, 1.0)
    sa64k = jnp.where(lane < IDX_DIM, sa64, 0.0)
    sb64k = jnp.where(lane < IDX_DIM, sb64, 0.0)
    return (c128, s128, c64, sa64, sb64), (c64k, sa64k, sb64k)


COUNT_ROWS = 32
WRITE_TILES_PER_STEP = 4
COARSE_BITS = 16
COARSE_SPAN = 1 << COARSE_BITS
COUNT_TILES_PER_STEP = 4
assert COUNT_TILES_PER_STEP * 256 // COUNT_ROWS <= 256


def _dsa_index_body(qit_ref, wt_ref, ke_ref, ko_ref, bias_ref, skey_ref, coarse_ref, gmax_ref,
                    *, tq, top_k, n_tiles):
    i = pl.program_id(0)
    score_tiles_per_step = 2 if n_tiles % 2 == 0 else 1
    n_live = i + 1
    q_pos = i * tq + lax.broadcasted_iota(jnp.int32, (tq, tq), 1)
    k_off = lax.broadcasted_iota(jnp.int32, (tq, tq), 0)
    wv = wt_ref[...] * ((IDX_HEADS ** -0.5) * (IDX_DIM ** -0.5))
    gmax_ref[...] = jnp.full((tq, tq), INT_MIN, jnp.int32)

    def score_tiles(step, masked):
        for jt in [step * score_tiles_per_step + u for u in range(score_tiles_per_step)]:
            rows = pl.ds(pl.multiple_of(jt * tq, tq), tq)
            ke = ke_ref[rows, :]
            ko = ko_ref[rows, :]
            acc = jnp.zeros((tq, tq), F32)
            for hp in range(IDX_HEADS // 2):
                q_pair = qit_ref[hp * LANES:(hp + 1) * LANES, :]
                d0 = _dot(ke, q_pair)
                d1 = _dot(ko, q_pair)
                acc = (acc + wv[2 * hp:2 * hp + 1, :] * jnp.maximum(d0, 0.0)
                       + wv[2 * hp + 1:2 * hp + 2, :] * jnp.maximum(d1, 0.0))
            if masked:
                allowed = ((jt * tq + k_off) >> CHUNK_SHIFT) <= (q_pos >> CHUNK_SHIFT)
                acc = jnp.where(allowed, acc, NEG)
            bits = lax.bitcast_convert_type(acc, jnp.int32)
            key = bits ^ ((bits >> 31) & jnp.int32(INT_MAX))
            skey_ref[jt] = key
            coarse_ref[jt] = lax.bitcast_convert_type(bits & jnp.int32(-COARSE_SPAN), F32).astype(BF16)
            gmax_ref[...] = jnp.maximum(gmax_ref[...], key)
        return 0

    n_plain = i // score_tiles_per_step
    lax.fori_loop(0, n_plain, lambda step, _: score_tiles(step, False), 0)
    score_tiles(n_plain, True)

    def count(pred):
        def tiles(first, n, acc):
            for jt in [first + u for u in range(n)]:
                for r in range(0, tq, COUNT_ROWS):
                    acc = acc + jnp.where(pred(skey_ref[jt, r:r + COUNT_ROWS, :]), 1.0, 0.0)
            return acc

        n_groups = n_live // COUNT_TILES_PER_STEP
        acc = lax.fori_loop(0, n_groups, lambda g, a: tiles(g * COUNT_TILES_PER_STEP, COUNT_TILES_PER_STEP, a),
                            jnp.zeros((COUNT_ROWS, tq), F32))
        acc = lax.fori_loop(n_groups * COUNT_TILES_PER_STEP, n_live, lambda jt, a: tiles(jt, 1, a), acc)
        return jnp.sum(acc, axis=0, keepdims=True)

    kf = float(top_k)

    gmax = gmax_ref[...]
    lo0 = jnp.min(gmax, axis=0, keepdims=True)
    hi0 = jnp.max(gmax, axis=0, keepdims=True)

    def narrowing(carry):
        lo, hi = carry
        return jnp.max(jnp.where(hi != lo, 1.0, 0.0)) > 0.0

    def bisect(carry):
        lo, hi = carry
        mid = (lo >> 1) + (hi >> 1) + (((lo & 1) + (hi & 1) + 1) >> 1)
        cnt = count(lambda t: t >= mid)
        ok = cnt >= kf
        hi_new = jnp.where(cnt == kf, mid, jnp.where(ok, hi, mid - 1))
        return jnp.where(ok, mid, lo), hi_new

    at_zero = count(lambda t: t >= 0) >= kf
    above_zero = count(lambda t: t >= 1) >= kf
    lo1 = jnp.where(above_zero, jnp.maximum(lo0, 1), jnp.where(at_zero, 0, lo0))
    hi1 = jnp.where(above_zero, hi0, jnp.where(at_zero, 0, jnp.minimum(hi0, -1)))

    def coarse_count(cand):
        cb = jnp.broadcast_to(cand, (COUNT_ROWS, tq))
        one, zero = jnp.ones((COUNT_ROWS, tq), cb.dtype), jnp.zeros((COUNT_ROWS, tq), cb.dtype)

        def tiles(first, n, acc):
            part = zero
            for jt in [first + u for u in range(n)]:
                for r in range(0, tq, COUNT_ROWS):
                    part = part + jnp.where(coarse_ref[jt, r:r + COUNT_ROWS, :] >= cb, one, zero)
            return acc + part.astype(F32)

        n_groups = n_live // COUNT_TILES_PER_STEP
        acc = lax.fori_loop(0, n_groups, lambda g, a: tiles(g * COUNT_TILES_PER_STEP, COUNT_TILES_PER_STEP, a),
                            jnp.zeros((COUNT_ROWS, tq), F32))
        acc = lax.fori_loop(n_groups * COUNT_TILES_PER_STEP, n_live, lambda jt, a: tiles(jt, 1, a), acc)
        return jnp.sum(acc, axis=0, keepdims=True)

    def coarse_value(p):
        k = (p << COARSE_BITS) | jnp.where(p < 0, COARSE_SPAN - 1, 0)
        return lax.bitcast_convert_type(k ^ ((k >> 31) & jnp.int32(INT_MAX)), F32).astype(BF16)

    def coarse_bisect(carry):
        lo, hi, done = carry
        mid = (lo >> 1) + (hi >> 1) + (((lo & 1) + (hi & 1) + 1) >> 1)
        cnt = coarse_count(coarse_value(mid))
        ok = cnt >= kf
        stop = (cnt == kf) & (mid != 0)
        return (jnp.where(ok, mid, lo), jnp.where(stop, mid, jnp.where(ok, hi, mid - 1)),
                jnp.where(stop, 1, done))

    c16, _, done = lax.while_loop(lambda c: narrowing(c[:2]), coarse_bisect,
                                  (lo1 >> COARSE_BITS, hi1 >> COARSE_BITS, jnp.zeros((1, tq), jnp.int32)))
    base = c16 << COARSE_BITS
    lo2 = jnp.maximum(jnp.where(c16 == 0, -COARSE_SPAN, base), lo1)
    hi2 = jnp.where(done == 1, lo2, jnp.minimum(base + (COARSE_SPAN - 1), hi1))
    lo, _ = lax.while_loop(narrowing, bisect, (lo2, hi2))

    need = kf - count(lambda t: t > lo)
    tri = jnp.where(k_off >= lax.broadcasted_iota(jnp.int32, (tq, tq), 1), 1.0, 0.0).astype(BF16)

    def write_tile(jt, seen, on_diagonal):
        t = skey_ref[jt]
        tied = t == lo
        rank = seen + _dot(tri, jnp.where(tied, 1.0, 0.0).astype(BF16))
        sel = (t > lo) | (tied & (rank <= need))
        if on_diagonal:
            sel = sel & (((jt * tq + k_off) >> CHUNK_SHIFT) <= (q_pos >> CHUNK_SHIFT))
        bias_ref[jt] = jnp.where(sel, 0.0, NEG).astype(BF16)
        return rank[tq - 1:tq, :]

    def write_group(step, seen):
        for u in range(WRITE_TILES_PER_STEP):
            seen = write_tile(step * WRITE_TILES_PER_STEP + u, seen, False)
        return seen

    n_groups = i // WRITE_TILES_PER_STEP
    seen = lax.fori_loop(0, n_groups, write_group, jnp.zeros((1, tq), F32))
    seen = lax.fori_loop(n_groups * WRITE_TILES_PER_STEP, i, lambda jt, s: write_tile(jt, s, False), seen)
    write_tile(i, seen, True)

    def blank_tile(jt, _):
        bias_ref[jt] = jnp.full((tq, tq), NEG, BF16)
        return 0

    lax.fori_loop(n_live, n_tiles, blank_tile, 0)


def _dsa_index(qi_t, w_t, ke, ko, tq):
    L = qi_t.shape[1]
    n_tiles = L // tq
    top_k = min(TOPK_MAX, L // 4)
    assert top_k <= tq
    body = functools.partial(_dsa_index_body, tq=tq, top_k=top_k, n_tiles=n_tiles)
    return pl.pallas_call(
        body,
        grid=(n_tiles,),
        in_specs=[
            pl.BlockSpec((IDX_HEADS * IDX_DIM, tq), lambda i: (0, i)),
            pl.BlockSpec((IDX_HEADS, tq), lambda i: (0, i)),
            _resident((L, LANES), lambda i: (0, 0)),
            _resident((L, LANES), lambda i: (0, 0)),
        ],
        out_specs=pl.BlockSpec((None, n_tiles, tq, tq), lambda i: (i, 0, 0, 0)),
        out_shape=jax.ShapeDtypeStruct((n_tiles, n_tiles, tq, tq), BF16),
        scratch_shapes=[pltpu.VMEM((n_tiles, tq, tq), jnp.int32), pltpu.VMEM((n_tiles, tq, tq), BF16),
                        pltpu.VMEM((tq, tq), jnp.int32)],
        compiler_params=_cparams(("parallel",)),
        name="dsa_index",
    )(qi_t, w_t, ke, ko)


FLASH_TILE = 256
ONES_ROWS = 16


def _values_t(v, heads):
    L = v.shape[0]
    vt = jnp.transpose(v).reshape(heads, -1, L)
    return jnp.concatenate([vt, jnp.ones((heads, ONES_ROWS, L), v.dtype)], axis=1)


class _Flash:
    def __init__(self, s_ref, m_ref, acc_ref):
        self.s_ref, self.m_ref, self.acc_ref = s_ref, m_ref, acc_ref
        m_ref[...] = jnp.full(m_ref.shape, NEG, F32)
        acc_ref[...] = jnp.zeros(acc_ref.shape, F32)

    def step(self, cur, nxt):
        for c, n in zip(cur, nxt):
            if c is not None:
                buf, slot, vt = c
                s = self.s_ref[buf]
                m_prev = self.m_ref[slot]
                m_new = jnp.maximum(m_prev, jnp.max(s, axis=0, keepdims=True))
                alpha = jnp.exp2(m_prev - m_new)
                p = jnp.exp2(s - m_new).astype(BF16)
                self.m_ref[slot] = m_new
            if n is not None:
                buf_n, kt, qt, add, mask = n
                s_n = _dot(kt, qt)
                if add is not None:
                    s_n = s_n + add
                if mask is not None:
                    s_n = jnp.where(mask, s_n, NEG)
                self.s_ref[buf_n] = s_n
            if c is not None:
                self.acc_ref[slot] = alpha * self.acc_ref[slot] + _dot(vt, p)


def _dsa_attn_body(qt_ref, k_ref, vt_ref, bias_ref, o_ref, s_ref, m_ref, acc_ref, *, tq, tiles_per_step):
    i = pl.program_id(0)
    flash = _Flash(s_ref, m_ref, acc_ref)
    tk = FLASH_TILE
    dv = A_HEAD_DIM
    heads = [slice(h * dv, (h + 1) * dv) for h in range(A_HEADS)]

    def key_rows(step, u):
        return pl.ds(pl.multiple_of((step * tiles_per_step + u) * tk, tk), tk)

    def cur_side(step):
        return [(u * A_HEADS + h, h, vt_ref[h, :, key_rows(step, u)])
                for u in range(tiles_per_step) for h in range(A_HEADS)]

    def nxt_side(step):
        out = []
        for u in range(tiles_per_step):
            rows = key_rows(step, u)
            b = bias_ref[rows, :].astype(F32)
            out += [(u * A_HEADS + h, k_ref[rows, hs], qt_ref[hs, :], b, None) for h, hs in enumerate(heads)]
        return out

    idle = [None] * (tiles_per_step * A_HEADS)
    n_live = ((i + 1) * tq + tk - 1) // tk
    n_steps = (n_live + tiles_per_step - 1) // tiles_per_step
    flash.step(idle, nxt_side(0))

    def body(step, _):
        flash.step(cur_side(step), nxt_side(step + 1))
        return 0

    lax.fori_loop(0, n_steps - 1, body, 0)
    flash.step(cur_side(n_steps - 1), idle)
    for h, hs in enumerate(heads):
        o_ref[:, hs] = jnp.transpose(acc_ref[h, :dv, :] / acc_ref[h, dv:dv + 1, :]).astype(o_ref.dtype)


def _dsa_attn(q_t, k, v_t, bias, tq):
    L = k.shape[0]
    n_tiles = L // tq
    key_tiles = L // FLASH_TILE
    tiles_per_step = 4 if key_tiles % 4 == 0 else (2 if key_tiles % 2 == 0 else 1)
    return pl.pallas_call(
        functools.partial(_dsa_attn_body, tq=tq, tiles_per_step=tiles_per_step),
        grid=(n_tiles,),
        in_specs=[
            pl.BlockSpec((A_WIDTH, tq), lambda i: (0, i)),
            _resident((L, A_WIDTH), lambda i: (0, 0)),
            _resident((A_HEADS, A_HEAD_DIM + ONES_ROWS, L), lambda i: (0, 0, 0)),
            pl.BlockSpec((None, L, tq), lambda i: (i, 0, 0)),
        ],
        out_specs=pl.BlockSpec((tq, A_WIDTH), lambda i: (i, 0)),
        out_shape=jax.ShapeDtypeStruct((L, A_WIDTH), BF16),
        scratch_shapes=[
            pltpu.VMEM((tiles_per_step * A_HEADS, FLASH_TILE, tq), F32),
            pltpu.VMEM((A_HEADS, 1, tq), F32),
            pltpu.VMEM((A_HEADS, A_HEAD_DIM + ONES_ROWS, tq), F32),
        ],
        compiler_params=_cparams(("parallel",)),
        name="dsa_attn",
    )(q_t, k, v_t, bias)


def _diff_attn_body(lq1_ref, lk1_ref, lq2_ref, lk2_ref, g_ref, qt_ref, k_ref, vt_ref, o_ref,
                    s_ref, m_ref, acc_ref, *, tq, lam_init):
    i = pl.program_id(1)
    flash = _Flash(s_ref, m_ref, acc_ref)
    dv = C_V_DIM
    tk = FLASH_TILE
    n_chunks = tq // FLASH_TILE
    n_full = i * n_chunks
    qt = qt_ref[...]
    row = lax.broadcasted_iota(jnp.int32, qt.shape, 0)
    zero = jnp.zeros_like(qt)
    qts = (jnp.where(row < C_QK_DIM, qt, zero), jnp.where(row >= C_QK_DIM, qt, zero))
    kk = lax.broadcasted_iota(jnp.int32, (tk, FLASH_TILE), 0)
    qq = lax.broadcasted_iota(jnp.int32, (tk, FLASH_TILE), 1)
    streams = [(c * n_chunks + qc, qc, qts[c][:, qc * FLASH_TILE:(qc + 1) * FLASH_TILE])
               for qc in range(n_chunks) for c in range(2)]

    def tile_start(jt):
        return pl.multiple_of(jt * tk, tk)

    def cur_side(jt, first_chunk=0):
        vt = vt_ref[:, pl.ds(tile_start(jt), tk)]
        return [(slot, slot, vt) if qc >= first_chunk else None for slot, qc, _ in streams]

    def nxt_side(jt, first_chunk=0, mask_of=lambda qc: None):
        kt = k_ref[pl.ds(tile_start(jt), tk), :]
        return [(slot, kt, q, None, mask_of(qc)) if qc >= first_chunk else None for slot, qc, q in streams]

    idle = [None] * len(streams)

    def first_mask(qc):
        q_pos = (i * n_chunks + qc) * FLASH_TILE + qq
        return (kk >> CHUNK_SHIFT) <= (q_pos >> CHUNK_SHIFT)

    flash.step(idle, nxt_side(0, mask_of=first_mask))

    def body(jt, _):
        flash.step(cur_side(jt), nxt_side(jt + 1))
        return 0

    lax.fori_loop(0, jnp.maximum(n_full - 1, 0), body, 0)

    diag = lambda d: (lambda qc: ((kk >> CHUNK_SHIFT) <= (qq >> CHUNK_SHIFT)) if qc == d else None)

    @pl.when(i > 0)
    def _():
        flash.step(cur_side(n_full - 1), nxt_side(n_full, mask_of=diag(0)))

    for d in range(1, n_chunks):
        flash.step(cur_side(n_full + d - 1, d - 1), nxt_side(n_full + d, d, diag(d)))
    flash.step(cur_side(n_full + n_chunks - 1, n_chunks - 1), idle)

    lam = (jnp.exp(jnp.sum(lq1_ref[...] * lk1_ref[...], axis=1, keepdims=True))
           - jnp.exp(jnp.sum(lq2_ref[...] * lk2_ref[...], axis=1, keepdims=True)) + lam_init)
    for qc in range(n_chunks):
        s0, s1 = qc, n_chunks + qc
        o = (acc_ref[s0, :dv, :] / acc_ref[s0, dv:dv + 1, :]
             - lam * (acc_ref[s1, :dv, :] / acc_ref[s1, dv:dv + 1, :]))
        inv = lax.rsqrt(jnp.mean(o * o, axis=0, keepdims=True) + EPS)
        o_ref[qc * FLASH_TILE:(qc + 1) * FLASH_TILE, :] = (
            jnp.transpose(o * inv) * g_ref[...] * (1.0 - lam_init)).astype(o_ref.dtype)


def _diff_attn(q_t, k, v_t, lam_vecs, sub_gain, layer, lam_init, tq):
    L = k.shape[0]
    vec = lambda w: _resident((None, 1, w), lambda h, i: (layer, 0, 0))
    return pl.pallas_call(
        functools.partial(_diff_attn_body, tq=tq, lam_init=lam_init),
        grid=(C_HEADS, L // tq),
        in_specs=[
            vec(C_QK_DIM), vec(C_QK_DIM), vec(C_QK_DIM), vec(C_QK_DIM), vec(C_V_DIM),
            pl.BlockSpec((C_V_DIM, tq), lambda h, i: (h, i)),
            pl.BlockSpec((L, C_V_DIM), lambda h, i: (0, h)),
            pl.BlockSpec((None, C_V_DIM + ONES_ROWS, L), lambda h, i: (h, 0, 0)),
        ],
        out_specs=pl.BlockSpec((tq, C_V_DIM), lambda h, i: (i, h)),
        out_shape=jax.ShapeDtypeStruct((L, C_WIDTH), BF16),
        scratch_shapes=[
            pltpu.VMEM((2 * (tq // FLASH_TILE), FLASH_TILE, FLASH_TILE), F32),
            pltpu.VMEM((2 * (tq // FLASH_TILE), 1, FLASH_TILE), F32),
            pltpu.VMEM((2 * (tq // FLASH_TILE), C_V_DIM + ONES_ROWS, FLASH_TILE), F32),
        ],
        compiler_params=_cparams(("parallel", "parallel")),
        name="diff_attn",
    )(*lam_vecs, sub_gain, q_t, k, v_t)


def _s5_prep_body(are_ref, aim_ref, ldt_ref, bre_ref, bim_ref, bb_ref, lvl_ref, pw_ref):
    a_re, a_im = are_ref[...], aim_ref[...]
    dt = jnp.exp(ldt_ref[...])
    mag = jnp.exp(a_re * dt)
    lb_re = mag * jnp.cos(a_im * dt)
    lb_im = mag * jnp.sin(a_im * dt)
    den = a_re * a_re + a_im * a_im
    f_re = ((lb_re - 1.0) * a_re + lb_im * a_im) / den
    f_im = (lb_im * a_re - (lb_re - 1.0) * a_im) / den
    b_re, b_im = bre_ref[...], bim_ref[...]
    bb_ref[:, :S5_NSTATE] = (f_re * b_re - f_im * b_im).astype(BF16)
    bb_ref[:, S5_NSTATE:] = (f_re * b_im + f_im * b_re).astype(BF16)

    pows = [(lb_re, lb_im)]
    for _ in range(SUBLANES - 1):
        pr, pi = pows[-1]
        pows.append((pr * lb_re - pi * lb_im, pr * lb_im + pi * lb_re))
    row = lax.broadcasted_iota(jnp.int32, (SUBLANES, S5_NSTATE), 0)
    zero = jnp.zeros((SUBLANES, S5_NSTATE), F32)
    for lvl in range(3):
        sh = 1 << lvl
        pr, pi = pows[sh - 1]
        lvl_ref[lvl, 0] = jnp.where(row >= sh, pr, zero)
        lvl_ref[lvl, 1] = jnp.where(row >= sh, pi, zero)
    cr, ci = zero, zero
    for r in range(SUBLANES):
        cr = jnp.where(row == r, pows[r][0], cr)
        ci = jnp.where(row == r, pows[r][1], ci)
    pw_ref[0] = cr
    pw_ref[1] = ci


def _s5_prep(a_re, a_im, log_dt, b_re_bd, b_im_bd):
    depth = a_re.shape[0]
    vec = pl.BlockSpec((None, 1, S5_NSTATE), lambda l: (l, 0, 0))
    mat = pl.BlockSpec((None, S5_WIDTH, S5_NSTATE), lambda l: (l, 0, 0))
    return pl.pallas_call(
        _s5_prep_body,
        grid=(depth,),
        in_specs=[vec, vec, vec, mat, mat],
        out_specs=[
            pl.BlockSpec((None, S5_WIDTH, 2 * S5_NSTATE), lambda l: (l, 0, 0)),
            pl.BlockSpec((None, 3, 2, SUBLANES, S5_NSTATE), lambda l: (l, 0, 0, 0, 0)),
            pl.BlockSpec((None, 2, SUBLANES, S5_NSTATE), lambda l: (l, 0, 0, 0)),
        ],
        out_shape=[
            jax.ShapeDtypeStruct((depth, S5_WIDTH, 2 * S5_NSTATE), BF16),
            jax.ShapeDtypeStruct((depth, 3, 2, SUBLANES, S5_NSTATE), F32),
            jax.ShapeDtypeStruct((depth, 2, SUBLANES, S5_NSTATE), F32),
        ],
        compiler_params=_cparams(("parallel",)),
        name="s5_prep",
    )(a_re, a_im, log_dt, b_re_bd, b_im_bd)


S5_LANE_CHUNK = 512


def _s5_body(u_ref, bb_ref, lvl_ref, pw_ref, cre_ref, cim_ref, d_ref, wg_ref, bg_ref, o_ref,
             x_ref, carry_ref, *, tl):
    @pl.when(pl.program_id(0) == 0)
    def _():
        carry_ref[...] = jnp.zeros(carry_ref.shape, F32)

    u = u_ref[...]
    x_ref[...] = _dot(u.astype(BF16), bb_ref[...])

    for c0 in range(0, S5_NSTATE, S5_LANE_CHUNK):
        re_sl = slice(c0, c0 + S5_LANE_CHUNK)
        im_sl = slice(S5_NSTATE + c0, S5_NSTATE + c0 + S5_LANE_CHUNK)

        def block(t, carry, re_sl=re_sl, im_sl=im_sl):
            c_re, c_im = carry
            rows = pl.ds(pl.multiple_of(t * SUBLANES, SUBLANES), SUBLANES)
            re = x_ref[rows, re_sl]
            im = x_ref[rows, im_sl]
            for lvl in range(3):
                sh = 1 << lvl
                s_re = pltpu.roll(re, sh, 0)
                s_im = pltpu.roll(im, sh, 0)
                a_re = lvl_ref[lvl, 0, :, re_sl]
                a_im = lvl_ref[lvl, 1, :, re_sl]
                re, im = re + (a_re * s_re - a_im * s_im), im + (a_re * s_im + a_im * s_re)
            p_re = pw_ref[0, :, re_sl]
            p_im = pw_ref[1, :, re_sl]
            re, im = re + (p_re * c_re - p_im * c_im), im + (p_re * c_im + p_im * c_re)
            x_ref[rows, re_sl] = re
            x_ref[rows, im_sl] = im
            last = SUBLANES - 1
            return (jnp.broadcast_to(re[last:, :], re.shape), jnp.broadcast_to(im[last:, :], im.shape))

        carry = lax.fori_loop(0, tl // SUBLANES, block,
                              (carry_ref[0, :, re_sl], carry_ref[1, :, re_sl]))
        carry_ref[0, :, re_sl] = carry[0]
        carry_ref[1, :, re_sl] = carry[1]

    y = (_dot(x_ref[:, :S5_NSTATE].astype(BF16), cre_ref[...])
         - _dot(x_ref[:, S5_NSTATE:].astype(BF16), cim_ref[...])
         + d_ref[...] * u)
    y = 0.5 * y * (1.0 + jnp.tanh(math.sqrt(2.0 / math.pi) * (y + 0.044715 * (y * y * y))))
    z = _dot(y.astype(BF16), wg_ref[...]) + bg_ref[...]
    o_ref[...] = (y * (1.0 / (1.0 + jnp.exp(-z)))).astype(o_ref.dtype)


def _s5_mix(u, bb, lvl, pw, c_re_bd, c_im_bd, d_row, w_glu, b_glu, layer, tl):
    L = u.shape[0]
    lay = lambda *rest: (lambda t: (layer,) + rest)
    return pl.pallas_call(
        functools.partial(_s5_body, tl=tl),
        grid=(L // tl,),
        in_specs=[
            pl.BlockSpec((tl, S5_WIDTH), lambda t: (t, 0)),
            _resident((None, S5_WIDTH, 2 * S5_NSTATE), lay(0, 0)),
            _resident((None, 3, 2, SUBLANES, S5_NSTATE), lay(0, 0, 0, 0)),
            _resident((None, 2, SUBLANES, S5_NSTATE), lay(0, 0, 0)),
            _resident((None, S5_NSTATE, S5_WIDTH), lay(0, 0)),
            _resident((None, S5_NSTATE, S5_WIDTH), lay(0, 0)),
            _resident((None, 1, S5_WIDTH), lay(0, 0)),
            _resident((None, S5_WIDTH, S5_WIDTH), lay(0, 0)),
            _resident((None, 1, S5_WIDTH), lay(0, 0)),
        ],
        out_specs=pl.BlockSpec((tl, S5_WIDTH), lambda t: (t, 0)),
        out_shape=jax.ShapeDtypeStruct((L, S5_WIDTH), BF16),
        scratch_shapes=[
            pltpu.VMEM((tl, 2 * S5_NSTATE), F32),
            pltpu.VMEM((2, SUBLANES, S5_NSTATE), F32),
        ],
        compiler_params=_cparams(("arbitrary",)),
        name="s5_mix",
    )(u, bb, lvl, pw, c_re_bd, c_im_bd, d_row, w_glu, b_glu)


def _outproj_body(x_ref, oa_ref, os_ref, oc_ref, w_ref, o_ref):
    a0, a1, a2 = A_WIDTH, A_WIDTH + S5_WIDTH, A_WIDTH + S5_WIDTH + C_WIDTH
    o_ref[...] = (x_ref[...] + _dot(oa_ref[...], w_ref[:a0, :]) + _dot(os_ref[...], w_ref[a0:a1, :])
                  + _dot(oc_ref[...], w_ref[a1:a2, :]))


def _out_proj(x, o_a, o_s, o_c, w_out, layer, tm):
    L = x.shape[0]
    row = lambda i: (i, 0)
    return pl.pallas_call(
        _outproj_body,
        grid=(L // tm,),
        in_specs=[
            pl.BlockSpec((tm, D_MODEL), row),
            pl.BlockSpec((tm, A_WIDTH), row),
            pl.BlockSpec((tm, S5_WIDTH), row),
            pl.BlockSpec((tm, C_WIDTH), row),
            _resident((None, D_MODEL, D_MODEL), lambda i: (layer, 0, 0)),
        ],
        out_specs=pl.BlockSpec((tm, D_MODEL), row),
        out_shape=jax.ShapeDtypeStruct((L, D_MODEL), F32),
        compiler_params=_cparams(("parallel",)),
        name="out_proj",
    )(x, o_a, o_s, o_c, w_out)


def _norm_matmul_body(x_ref, g_ref, w_ref, o_ref):
    o_ref[...] = _dot(_rms(x_ref[...], g_ref[...]).astype(BF16), w_ref[...]).astype(o_ref.dtype)


def _norm_matmul(x, g, w, layer, tn):
    M, K = x.shape
    N = w.shape[-1]
    return pl.pallas_call(
        _norm_matmul_body,
        grid=(N // tn,),
        in_specs=[
            _resident((M, K), lambda j: (0, 0)),
            _resident((None, 1, K), lambda j: (layer, 0, 0)),
            pl.BlockSpec((None, K, tn), lambda j: (layer, 0, j)),
        ],
        out_specs=pl.BlockSpec((M, tn), lambda j: (0, j)),
        out_shape=jax.ShapeDtypeStruct((M, N), BF16),
        compiler_params=_cparams(("parallel",)),
        name="mem_proj",
    )(x, g, w)


def _xattn_body(x_ref, g_ref, wq_ref, k_ref, v_ref, wo_ref, o_ref):
    x = x_ref[...]
    q = _dot(_rms(x, g_ref[...]).astype(BF16), wq_ref[...]).astype(BF16)
    scale = X_HEAD_DIM ** -0.5
    out = x
    for h in range(X_HEADS):
        hs = slice(h * X_HEAD_DIM, (h + 1) * X_HEAD_DIM)
        s = _dot_nt(q[:, hs], k_ref[:, hs]) * scale
        p = jnp.exp(s - jnp.max(s, axis=1, keepdims=True))
        p = p / jnp.sum(p, axis=1, keepdims=True)
        o_h = _dot(p.astype(BF16), v_ref[:, hs]).astype(BF16)
        out = out + _dot(o_h, wo_ref[hs, :])
    o_ref[...] = out


def _cross_attn(x, g, wq, k_mem, v_mem, wo, layer, tm):
    L = x.shape[0]
    M = k_mem.shape[0]
    row = lambda i: (i, 0)
    return pl.pallas_call(
        _xattn_body,
        grid=(L // tm,),
        in_specs=[
            pl.BlockSpec((tm, D_MODEL), row),
            _resident((None, 1, D_MODEL), lambda i: (layer, 0, 0)),
            _resident((None, D_MODEL, D_MODEL), lambda i: (layer, 0, 0)),
            _resident((M, D_MODEL), lambda i: (0, 0)),
            _resident((M, D_MODEL), lambda i: (0, 0)),
            _resident((None, D_MODEL, D_MODEL), lambda i: (layer, 0, 0)),
        ],
        out_specs=pl.BlockSpec((tm, D_MODEL), row),
        out_shape=jax.ShapeDtypeStruct((L, D_MODEL), F32),
        compiler_params=_cparams(("parallel",)),
        name="cross_attn",
    )(x, g, wq, k_mem, v_mem, wo)


def _mlp_body(x_ref, g_ref, w1_ref, w2_ref, gout_ref, o_ref, hn_ref, *, norm_output):
    f = pl.program_id(1)

    @pl.when(f == 0)
    def _():
        x = x_ref[...]
        hn_ref[...] = _rms(x, g_ref[...]).astype(BF16)
        o_ref[...] = x

    a = jnp.maximum(_dot(hn_ref[...], w1_ref[...]), 0.0)
    o_ref[...] += _dot((a * a).astype(BF16), w2_ref[...])

    if norm_output:
        @pl.when(f == pl.num_programs(1) - 1)
        def _():
            o_ref[...] = _rms(o_ref[...], gout_ref[...])


def _mlp(x, g, w1, w2, g_out, layer, tm, tf, norm_output):
    L = x.shape[0]
    return pl.pallas_call(
        functools.partial(_mlp_body, norm_output=norm_output),
        grid=(L // tm, D_FF // tf),
        in_specs=[
            pl.BlockSpec((tm, D_MODEL), lambda i, f: (i, 0)),
            _resident((None, 1, D_MODEL), lambda i, f: (layer, 0, 0)),
            pl.BlockSpec((None, D_MODEL, tf), lambda i, f: (layer, 0, f)),
            pl.BlockSpec((None, tf, D_MODEL), lambda i, f: (layer, f, 0)),
            _resident((1, D_MODEL), lambda i, f: (0, 0)),
        ],
        out_specs=pl.BlockSpec((tm, D_MODEL), lambda i, f: (i, 0)),
        out_shape=jax.ShapeDtypeStruct((L, D_MODEL), F32),
        scratch_shapes=[pltpu.VMEM((tm, D_MODEL), BF16)],
        compiler_params=_cparams(("parallel", "arbitrary")),
        name="mlp",
    )(x, g, w1, w2, g_out)


def _block_diag_in(b):
    eye = jnp.eye(S5_GROUPS, dtype=b.dtype)
    t = jnp.transpose(b, (0, 1, 3, 2))
    bd = t[:, :, :, None, :] * eye[None, :, None, :, None]
    return bd.reshape(b.shape[0], S5_WIDTH, S5_NSTATE)


def _block_diag_out(c):
    eye = jnp.eye(S5_GROUPS, dtype=c.dtype)
    t = jnp.transpose(c, (0, 1, 3, 2))
    bd = t[:, :, :, None, :] * eye[None, :, None, :, None]
    return bd.reshape(c.shape[0], S5_NSTATE, S5_WIDTH)


def _pick_tile(n, want):
    t = min(n, want)
    assert n % t == 0, (n, t)
    return t


def kernel(x, mem, norm_mix, w_in, s5_a_re, s5_a_im, s5_log_dt, s5_b_re, s5_b_im, s5_c_re, s5_c_im, s5_d, s5_w_glu, s5_b_glu, diff_lam_q1, diff_lam_k1, diff_lam_q2, diff_lam_k2, diff_subln, w_out, norm_xattn, norm_mem, xattn_q, xattn_k, xattn_v, xattn_o, norm_mlp, w_ff1, w_ff2, norm_final):
    B, L, _ = x.shape
    assert B == 1
    depth = w_in.shape[0]
    tq = _pick_tile(L, 256)

    pad = jnp.zeros((depth, D_MODEL, KIW_W - IDX_DIM - IDX_HEADS), BF16)
    w_in_p = jnp.concatenate([w_in[..., :IN_SRC_SPLIT].astype(BF16), pad, w_in[..., IN_SRC_SPLIT:].astype(BF16)],
                             axis=-1)
    w_out_b = w_out.astype(BF16)
    wq_b, wk_b, wv_b, wo_b = (w.astype(BF16) for w in (xattn_q, xattn_k, xattn_v, xattn_o))
    w1_b, w2_b = w_ff1.astype(BF16), w_ff2.astype(BF16)
    wg_b = s5_w_glu.astype(BF16)
    row3 = lambda a: a.reshape(depth, 1, -1)

    tabs, tabs_k = _rope_tables(L)

    rep = lambda a: row3(a.astype(F32))
    a_re, a_im = rep(s5_a_re), rep(s5_a_im)
    log_dt = row3(jnp.broadcast_to(s5_log_dt.astype(F32)[:, :, None], (depth, S5_GROUPS, S5_STATE)))
    bb, lvl, pw = _s5_prep(a_re, a_im, log_dt, _block_diag_in(s5_b_re.astype(F32)),
                           _block_diag_in(s5_b_im.astype(F32)))
    c_re_bd = _block_diag_out(s5_c_re).astype(BF16)
    c_im_bd = _block_diag_out(s5_c_im).astype(BF16)
    d_row = row3(s5_d.astype(F32))
    lam_vecs = tuple(row3(v.astype(F32)) for v in (diff_lam_q1, diff_lam_k1, diff_lam_q2, diff_lam_k2))

    xs = x[0]
    mem2 = mem[0]
    for l in range(depth):
        qa, ka, va, qi, ke, ko, wi, us, qc, kc, vc = _in_proj(xs, row3(norm_mix), w_in_p, l, tabs + tabs_k,
                                                               _pick_tile(L, 256))

        w_t = jnp.transpose(wi[:, IDX_DIM:IDX_DIM + IDX_HEADS])
        bias = _dsa_index(jnp.transpose(qi), w_t, ke, ko, tq).reshape(L // tq, L, tq)
        o_a = _dsa_attn(jnp.transpose(qa), ka, _values_t(va, A_HEADS), bias, tq)

        o_s = _s5_mix(us, bb, lvl, pw, c_re_bd, c_im_bd, d_row, wg_b, row3(s5_b_glu.astype(F32)), l,
                      _pick_tile(L, 512))

        lam_init = 0.8 - 0.6 * math.exp(-0.3 * l)
        o_c = _diff_attn(jnp.transpose(qc), kc, _values_t(vc, C_HEADS), lam_vecs, row3(diff_subln.astype(F32)), l,
                         lam_init, _pick_tile(L, 2048))

        xs = _out_proj(xs, o_a, o_s, o_c, w_out_b, l, _pick_tile(L, 512))

        k_mem = _norm_matmul(mem2, row3(norm_mem), wk_b, l, 512)
        v_mem = _norm_matmul(mem2, row3(norm_mem), wv_b, l, 512)
        xs = _cross_attn(xs, row3(norm_xattn), wq_b, k_mem, v_mem, wo_b, l, _pick_tile(L, 512))

        xs = _mlp(xs, row3(norm_mlp), w1_b, w2_b, norm_final.reshape(1, -1), l, _pick_tile(L, 1024), 512,
                  norm_output=(l == depth - 1))

    return xs[None]
```

```python
import functools
import math

import jax
import jax.numpy as jnp
from jax import lax
from jax.experimental import pallas as pl
from jax.experimental.pallas import tpu as pltpu

F32 = jnp.float32
BF16 = jnp.bfloat16

D_MODEL = 2048
CHUNK = 64
CHUNK_SHIFT = CHUNK.bit_length() - 1
assert 1 << CHUNK_SHIFT == CHUNK
ROPE_THETA = 10000.0
EPS = 1e-6
NEG = -1e30

A_HEAD_DIM = 128
A_WIDTH = D_MODEL // 4
A_HEADS = A_WIDTH // A_HEAD_DIM
IDX_HEADS = 8
IDX_DIM = 64
TOPK_MAX = 256

S5_GROUP = 16
S5_WIDTH = D_MODEL // 4
S5_GROUPS = S5_WIDTH // S5_GROUP
S5_STATE = 64
S5_NSTATE = S5_GROUPS * S5_STATE

C_QK_DIM = 64
C_V_DIM = 2 * C_QK_DIM
C_WIDTH = D_MODEL // 2
C_HEADS = C_WIDTH // C_V_DIM

X_HEADS = 4
X_HEAD_DIM = D_MODEL // X_HEADS
D_FF = 4 * D_MODEL

LANES = 128
SUBLANES = 8
VMEM_LIMIT = 58 * 1024 * 1024

KIW_W = LANES
OFF_QA = 0
OFF_KA = OFF_QA + A_WIDTH
OFF_VA = OFF_KA + A_WIDTH
OFF_QI = OFF_VA + A_WIDTH
OFF_KIW = OFF_QI + IDX_HEADS * IDX_DIM
OFF_US = OFF_KIW + KIW_W
OFF_QC = OFF_US + S5_WIDTH
OFF_KC = OFF_QC + C_WIDTH
OFF_VC = OFF_KC + C_WIDTH
IN_PACKED = OFF_VC + C_WIDTH
IN_SRC_SPLIT = 3 * A_WIDTH + IDX_HEADS * IDX_DIM + IDX_DIM + IDX_HEADS

LOG2E = math.log2(math.e)
INT_MIN = -2 ** 31
INT_MAX = 2 ** 31 - 1


def _cparams(sem):
    return pltpu.CompilerParams(dimension_semantics=sem, vmem_limit_bytes=VMEM_LIMIT)


def _resident(shape, index_map):
    return pl.BlockSpec(shape, index_map, pipeline_mode=pl.Buffered(1))


def _rms(x, g):
    inv = lax.rsqrt(jnp.mean(x * x, axis=-1, keepdims=True) + EPS)
    return x * inv * g


def _dot(a, b):
    return jnp.dot(a, b, preferred_element_type=F32)


def _dot_nt(a, b):
    return lax.dot_general(a, b, (((1,), (1,)), ((), ())), preferred_element_type=F32)


def _inproj_body(x_ref, g_ref, w_ref, c128_ref, s128_ref, c64_ref, sa64_ref, sb64_ref,
                 c64k_ref, sa64k_ref, sb64k_ref,
                 qa_ref, ka_ref, va_ref, qi_ref, ke_ref, ko_ref, wi_ref, us_ref, qc_ref, kc_ref, vc_ref):
    hn = _rms(x_ref[...], g_ref[...]).astype(BF16)
    c128, s128 = c128_ref[...], s128_ref[...]
    c64, sa64, sb64 = c64_ref[...], sa64_ref[...], sb64_ref[...]

    def rope128(t):
        return t * c128 + pltpu.roll(t, 64, 1) * s128

    def rope64(t):
        return t * c64 + pltpu.roll(t, 96, 1) * sa64 + pltpu.roll(t, 32, 1) * sb64

    def rope64_kiw(t):
        return t * c64k_ref[...] + pltpu.roll(t, 96, 1) * sa64k_ref[...] + pltpu.roll(t, 32, 1) * sb64k_ref[...]

    def emit(off, width, out_ref, fn, scale=None):
        step = 512 if width >= 512 else width
        for c0 in range(0, width, step):
            t = _dot(hn, w_ref[:, off + c0:off + c0 + step])
            for s0 in range(0, step, LANES):
                v = t[:, s0:s0 + LANES]
                if fn is not None:
                    v = fn(v)
                if scale is not None:
                    v = v * scale
                out_ref[:, c0 + s0:c0 + s0 + LANES] = v.astype(out_ref.dtype)

    emit(OFF_QA, A_WIDTH, qa_ref, rope128, A_HEAD_DIM ** -0.5 * LOG2E)
    emit(OFF_KA, A_WIDTH, ka_ref, rope128)
    emit(OFF_VA, A_WIDTH, va_ref, None)
    emit(OFF_QI, IDX_HEADS * IDX_DIM, qi_ref, rope64)
    kiw = rope64_kiw(_dot(hn, w_ref[:, OFF_KIW:OFF_KIW + KIW_W]))
    lane = lax.broadcasted_iota(jnp.int32, kiw.shape, 1)
    k_low = jnp.where(lane < IDX_DIM, kiw, 0.0)
    ke_ref[...] = k_low.astype(BF16)
    ko_ref[...] = pltpu.roll(k_low, IDX_DIM, 1).astype(BF16)
    wi_ref[...] = kiw
    emit(OFF_US, S5_WIDTH, us_ref, None)
    emit(OFF_QC, C_WIDTH, qc_ref, rope64, C_QK_DIM ** -0.5 * LOG2E)
    emit(OFF_KC, C_WIDTH, kc_ref, rope64)
    emit(OFF_VC, C_WIDTH, vc_ref, None)


def _in_proj(x, g, w_packed, layer, tabs, tm):
    L = x.shape[0]
    row = lambda i: (i, 0)
    tab_spec = pl.BlockSpec((tm, LANES), row)
    outs = [
        (A_WIDTH, BF16), (A_WIDTH, BF16), (A_WIDTH, BF16), (IDX_HEADS * IDX_DIM, BF16),
        (KIW_W, BF16), (KIW_W, BF16), (KIW_W, F32),
        (S5_WIDTH, F32), (C_WIDTH, BF16), (C_WIDTH, BF16), (C_WIDTH, BF16),
    ]
    return pl.pallas_call(
        _inproj_body,
        grid=(L // tm,),
        in_specs=[
            pl.BlockSpec((tm, D_MODEL), row),
            _resident((None, 1, D_MODEL), lambda i: (layer, 0, 0)),
            _resident((None, D_MODEL, IN_PACKED), lambda i: (layer, 0, 0)),
        ] + [tab_spec] * 8,
        out_specs=[pl.BlockSpec((tm, w), row) for w, _ in outs],
        out_shape=[jax.ShapeDtypeStruct((L, w), dt) for w, dt in outs],
        compiler_params=_cparams(("parallel",)),
        name="in_proj",
    )(x, g, w_packed, *tabs)


def _rope_tables(L):
    pos = jnp.arange(L, dtype=F32)[:, None]

    def cs(dim):
        inv = ROPE_THETA ** (-jnp.arange(0, dim, 2, dtype=F32) / dim)
        ang = pos * inv[None, :]
        return jnp.cos(ang), jnp.sin(ang)

    c, s = cs(A_HEAD_DIM)
    c128 = jnp.concatenate([c, c], axis=1)
    s128 = jnp.concatenate([-s, s], axis=1)
    c, s = cs(IDX_DIM)
    z = jnp.zeros_like(s)
    c64 = jnp.tile(jnp.concatenate([c, c], axis=1), (1, 2))
    sa64 = jnp.tile(jnp.concatenate([-s, z], axis=1), (1, 2))
    sb64 = jnp.tile(jnp.concatenate([z, s], axis=1), (1, 2))
    lane = jnp.arange(LANES)[None, :]
    c64k = jnp.where(lane < IDX_DIM, c64, 1.0)
    sa64k = jnp.where(lane < IDX_DIM, sa64, 0.0)
    sb64k = jnp.where(lane < IDX_DIM, sb64, 0.0)
    return (c128, s128, c64, sa64, sb64), (c64k, sa64k, sb64k)


COUNT_ROWS = 32
WRITE_TILES_PER_STEP = 4
COARSE_BITS = 16
COARSE_SPAN = 1 << COARSE_BITS
COUNT_TILES_PER_STEP = 4
assert COUNT_TILES_PER_STEP * 256 // COUNT_ROWS <= 256


def _dsa_index_body(qit_ref, wt_ref, ke_ref, ko_ref, bias_ref, skey_ref, coarse_ref, gmax_ref,
                    *, tq, top_k, n_tiles):
    i = pl.program_id(0)
    score_tiles_per_step = 2 if n_tiles % 2 == 0 else 1
    n_live = i + 1
    q_pos = i * tq + lax.broadcasted_iota(jnp.int32, (tq, tq), 1)
    k_off = lax.broadcasted_iota(jnp.int32, (tq, tq), 0)
    wv = wt_ref[...] * ((IDX_HEADS ** -0.5) * (IDX_DIM ** -0.5))
    gmax_ref[...] = jnp.full((tq, tq), INT_MIN, jnp.int32)

    def score_tiles(step, masked):
        for jt in [step * score_tiles_per_step + u for u in range(score_tiles_per_step)]:
            rows = pl.ds(pl.multiple_of(jt * tq, tq), tq)
            ke = ke_ref[rows, :]
            ko = ko_ref[rows, :]
            acc = jnp.zeros((tq, tq), F32)
            for hp in range(IDX_HEADS // 2):
                q_pair = qit_ref[hp * LANES:(hp + 1) * LANES, :]
                d0 = _dot(ke, q_pair)
                d1 = _dot(ko, q_pair)
                acc = (acc + wv[2 * hp:2 * hp + 1, :] * jnp.maximum(d0, 0.0)
                       + wv[2 * hp + 1:2 * hp + 2, :] * jnp.maximum(d1, 0.0))
            if masked:
                allowed = ((jt * tq + k_off) >> CHUNK_SHIFT) <= (q_pos >> CHUNK_SHIFT)
                acc = jnp.where(allowed, acc, NEG)
            bits = lax.bitcast_convert_type(acc, jnp.int32)
            key = bits ^ ((bits >> 31) & jnp.int32(INT_MAX))
            skey_ref[jt] = key
            coarse_ref[jt] = lax.bitcast_convert_type(bits & jnp.int32(-COARSE_SPAN), F32).astype(BF16)
            gmax_ref[...] = jnp.maximum(gmax_ref[...], key)
        return 0

    n_plain = i // score_tiles_per_step
    lax.fori_loop(0, n_plain, lambda step, _: score_tiles(step, False), 0)
    score_tiles(n_plain, True)

    def count(pred):
        def tiles(first, n, acc):
            for jt in [first + u for u in range(n)]:
                for r in range(0, tq, COUNT_ROWS):
                    acc = acc + jnp.where(pred(skey_ref[jt, r:r + COUNT_ROWS, :]), 1.0, 0.0)
            return acc

        n_groups = n_live // COUNT_TILES_PER_STEP
        acc = lax.fori_loop(0, n_groups, lambda g, a: tiles(g * COUNT_TILES_PER_STEP, COUNT_TILES_PER_STEP, a),
                            jnp.zeros((COUNT_ROWS, tq), F32))
        acc = lax.fori_loop(n_groups * COUNT_TILES_PER_STEP, n_live, lambda jt, a: tiles(jt, 1, a), acc)
        return jnp.sum(acc, axis=0, keepdims=True)

    kf = float(top_k)

    gmax = gmax_ref[...]
    lo0 = jnp.min(gmax, axis=0, keepdims=True)
    hi0 = jnp.max(gmax, axis=0, keepdims=True)

    def narrowing(carry):
        lo, hi = carry
        return jnp.max(jnp.where(hi != lo, 1.0, 0.0)) > 0.0

    def bisect(carry):
        lo, hi = carry
        mid = (lo >> 1) + (hi >> 1) + (((lo & 1) + (hi & 1) + 1) >> 1)
        cnt = count(lambda t: t >= mid)
        ok = cnt >= kf
        hi_new = jnp.where(cnt == kf, mid, jnp.where(ok, hi, mid - 1))
        return jnp.where(ok, mid, lo), hi_new

    at_zero = count(lambda t: t >= 0) >= kf
    above_zero = count(lambda t: t >= 1) >= kf
    lo1 = jnp.where(above_zero, jnp.maximum(lo0, 1), jnp.where(at_zero, 0, lo0))
    hi1 = jnp.where(above_zero, hi0, jnp.where(at_zero, 0, jnp.minimum(hi0, -1)))

    def coarse_count(cand):
        cb = jnp.broadcast_to(cand, (COUNT_ROWS, tq))
        one, zero = jnp.ones((COUNT_ROWS, tq), cb.dtype), jnp.zeros((COUNT_ROWS, tq), cb.dtype)

        def tiles(first, n, acc):
            part = zero
            for jt in [first + u for u in range(n)]:
                for r in range(0, tq, COUNT_ROWS):
                    part = part + jnp.where(coarse_ref[jt, r:r + COUNT_ROWS, :] >= cb, one, zero)
            return acc + part.astype(F32)

        n_groups = n_live // COUNT_TILES_PER_STEP
        acc = lax.fori_loop(0, n_groups, lambda g, a: tiles(g * COUNT_TILES_PER_STEP, COUNT_TILES_PER_STEP, a),
                            jnp.zeros((COUNT_ROWS, tq), F32))
        acc = lax.fori_loop(n_groups * COUNT_TILES_PER_STEP, n_live, lambda jt, a: tiles(jt, 1, a), acc)
        return jnp.sum(acc, axis=0, keepdims=True)

    def coarse_value(p):
        k = (p << COARSE_BITS) | jnp.where(p < 0, COARSE_SPAN - 1, 0)
        return lax.bitcast_convert_type(k ^ ((k >> 31) & jnp.int32(INT_MAX)), F32).astype(BF16)

    def coarse_bisect(carry):
        lo, hi, done = carry
        mid = (lo >> 1) + (hi >> 1) + (((lo & 1) + (hi & 1) + 1) >> 1)
        cnt = coarse_count(coarse_value(mid))
        ok = cnt >= kf
        stop = (cnt == kf) & (mid != 0)
        return (jnp.where(ok, mid, lo), jnp.where(stop, mid, jnp.where(ok, hi, mid - 1)),
                jnp.where(stop, 1, done))

    c16, _, done = lax.while_loop(lambda c: narrowing(c[:2]), coarse_bisect,
                                  (lo1 >> COARSE_BITS, hi1 >> COARSE_BITS, jnp.zeros((1, tq), jnp.int32)))
    base = c16 << COARSE_BITS
    lo2 = jnp.maximum(jnp.where(c16 == 0, -COARSE_SPAN, base), lo1)
    hi2 = jnp.where(done == 1, lo2, jnp.minimum(base + (COARSE_SPAN - 1), hi1))
    lo, _ = lax.while_loop(narrowing, bisect, (lo2, hi2))

    need = kf - count(lambda t: t > lo)
    tri = jnp.where(k_off >= lax.broadcasted_iota(jnp.int32, (tq, tq), 1), 1.0, 0.0).astype(BF16)

    def write_tile(jt, seen, on_diagonal):
        t = skey_ref[jt]
        tied = t == lo
        rank = seen + _dot(tri, jnp.where(tied, 1.0, 0.0).astype(BF16))
        sel = (t > lo) | (tied & (rank <= need))
        if on_diagonal:
            sel = sel & (((jt * tq + k_off) >> CHUNK_SHIFT) <= (q_pos >> CHUNK_SHIFT))
        bias_ref[jt] = jnp.where(sel, 0.0, NEG).astype(BF16)
        return rank[tq - 1:tq, :]

    def write_group(step, seen):
        for u in range(WRITE_TILES_PER_STEP):
            seen = write_tile(step * WRITE_TILES_PER_STEP + u, seen, False)
        return seen

    n_groups = i // WRITE_TILES_PER_STEP
    seen = lax.fori_loop(0, n_groups, write_group, jnp.zeros((1, tq), F32))
    seen = lax.fori_loop(n_groups * WRITE_TILES_PER_STEP, i, lambda jt, s: write_tile(jt, s, False), seen)
    write_tile(i, seen, True)

    def blank_tile(jt, _):
        bias_ref[jt] = jnp.full((tq, tq), NEG, BF16)
        return 0

    lax.fori_loop(n_live, n_tiles, blank_tile, 0)


def _dsa_index(qi_t, w_t, ke, ko, tq):
    L = qi_t.shape[1]
    n_tiles = L // tq
    top_k = min(TOPK_MAX, L // 4)
    assert top_k <= tq
    body = functools.partial(_dsa_index_body, tq=tq, top_k=top_k, n_tiles=n_tiles)
    return pl.pallas_call(
        body,
        grid=(n_tiles,),
        in_specs=[
            pl.BlockSpec((IDX_HEADS * IDX_DIM, tq), lambda i: (0, i)),
            pl.BlockSpec((IDX_HEADS, tq), lambda i: (0, i)),
            _resident((L, LANES), lambda i: (0, 0)),
            _resident((L, LANES), lambda i: (0, 0)),
        ],
        out_specs=pl.BlockSpec((None, n_tiles, tq, tq), lambda i: (i, 0, 0, 0)),
        out_shape=jax.ShapeDtypeStruct((n_tiles, n_tiles, tq, tq), BF16),
        scratch_shapes=[pltpu.VMEM((n_tiles, tq, tq), jnp.int32), pltpu.VMEM((n_tiles, tq, tq), BF16),
                        pltpu.VMEM((tq, tq), jnp.int32)],
        compiler_params=_cparams(("parallel",)),
        name="dsa_index",
    )(qi_t, w_t, ke, ko)


FLASH_TILE = 256
ONES_ROWS = 16


def _values_t(v, heads):
    L = v.shape[0]
    vt = jnp.transpose(v).reshape(heads, -1, L)
    return jnp.concatenate([vt, jnp.ones((heads, ONES_ROWS, L), v.dtype)], axis=1)


class _Flash:
    def __init__(self, s_ref, m_ref, acc_ref):
        self.s_ref, self.m_ref, self.acc_ref = s_ref, m_ref, acc_ref
        m_ref[...] = jnp.full(m_ref.shape, NEG, F32)
        acc_ref[...] = jnp.zeros(acc_ref.shape, F32)

    def step(self, cur, nxt):
        for c, n in zip(cur, nxt):
            if c is not None:
                buf, slot, vt = c
                s = self.s_ref[buf]
                m_prev = self.m_ref[slot]
                m_new = jnp.maximum(m_prev, jnp.max(s, axis=0, keepdims=True))
                alpha = jnp.exp2(m_prev - m_new)
                p = jnp.exp2(s - m_new).astype(BF16)
                self.m_ref[slot] = m_new
            if n is not None:
                buf_n, kt, qt, add, mask = n
                s_n = _dot(kt, qt)
                if add is not None:
                    s_n = s_n + add
                if mask is not None:
                    s_n = jnp.where(mask, s_n, NEG)
                self.s_ref[buf_n] = s_n
            if c is not None:
                self.acc_ref[slot] = alpha * self.acc_ref[slot] + _dot(vt, p)


def _dsa_attn_body(qt_ref, k_ref, vt_ref, bias_ref, o_ref, s_ref, m_ref, acc_ref, *, tq, tiles_per_step):
    i = pl.program_id(0)
    flash = _Flash(s_ref, m_ref, acc_ref)
    tk = FLASH_TILE
    dv = A_HEAD_DIM
    heads = [slice(h * dv, (h + 1) * dv) for h in range(A_HEADS)]

    def key_rows(step, u):
        return pl.ds(pl.multiple_of((step * tiles_per_step + u) * tk, tk), tk)

    def cur_side(step):
        return [(u * A_HEADS + h, h, vt_ref[h, :, key_rows(step, u)])
                for u in range(tiles_per_step) for h in range(A_HEADS)]

    def nxt_side(step):
        out = []
        for u in range(tiles_per_step):
            rows = key_rows(step, u)
            b = bias_ref[rows, :].astype(F32)
            out += [(u * A_HEADS + h, k_ref[rows, hs], qt_ref[hs, :], b, None) for h, hs in enumerate(heads)]
        return out

    idle = [None] * (tiles_per_step * A_HEADS)
    n_live = ((i + 1) * tq + tk - 1) // tk
    n_steps = (n_live + tiles_per_step - 1) // tiles_per_step
    flash.step(idle, nxt_side(0))

    def body(step, _):
        flash.step(cur_side(step), nxt_side(step + 1))
        return 0

    lax.fori_loop(0, n_steps - 1, body, 0)
    flash.step(cur_side(n_steps - 1), idle)
    for h, hs in enumerate(heads):
        o_ref[:, hs] = jnp.transpose(acc_ref[h, :dv, :] / acc_ref[h, dv:dv + 1, :]).astype(o_ref.dtype)


def _dsa_attn(q_t, k, v_t, bias, tq):
    L = k.shape[0]
    n_tiles = L // tq
    key_tiles = L // FLASH_TILE
    tiles_per_step = 4 if key_tiles % 4 == 0 else (2 if key_tiles % 2 == 0 else 1)
    return pl.pallas_call(
        functools.partial(_dsa_attn_body, tq=tq, tiles_per_step=tiles_per_step),
        grid=(n_tiles,),
        in_specs=[
            pl.BlockSpec((A_WIDTH, tq), lambda i: (0, i)),
            _resident((L, A_WIDTH), lambda i: (0, 0)),
            _resident((A_HEADS, A_HEAD_DIM + ONES_ROWS, L), lambda i: (0, 0, 0)),
            pl.BlockSpec((None, L, tq), lambda i: (i, 0, 0)),
        ],
        out_specs=pl.BlockSpec((tq, A_WIDTH), lambda i: (i, 0)),
        out_shape=jax.ShapeDtypeStruct((L, A_WIDTH), BF16),
        scratch_shapes=[
            pltpu.VMEM((tiles_per_step * A_HEADS, FLASH_TILE, tq), F32),
            pltpu.VMEM((A_HEADS, 1, tq), F32),
            pltpu.VMEM((A_HEADS, A_HEAD_DIM + ONES_ROWS, tq), F32),
        ],
        compiler_params=_cparams(("parallel",)),
        name="dsa_attn",
    )(q_t, k, v_t, bias)


def _diff_attn_body(lq1_ref, lk1_ref, lq2_ref, lk2_ref, g_ref, qt_ref, k_ref, vt_ref, o_ref,
                    s_ref, m_ref, acc_ref, *, tq, lam_init):
    i = pl.program_id(1)
    flash = _Flash(s_ref, m_ref, acc_ref)
    dv = C_V_DIM
    tk = FLASH_TILE
    n_chunks = tq // FLASH_TILE
    n_full = i * n_chunks
    qt = qt_ref[...]
    row = lax.broadcasted_iota(jnp.int32, qt.shape, 0)
    zero = jnp.zeros_like(qt)
    qts = (jnp.where(row < C_QK_DIM, qt, zero), jnp.where(row >= C_QK_DIM, qt, zero))
    kk = lax.broadcasted_iota(jnp.int32, (tk, FLASH_TILE), 0)
    qq = lax.broadcasted_iota(jnp.int32, (tk, FLASH_TILE), 1)
    streams = [(c * n_chunks + qc, qc, qts[c][:, qc * FLASH_TILE:(qc + 1) * FLASH_TILE])
               for qc in range(n_chunks) for c in range(2)]

    def tile_start(jt):
        return pl.multiple_of(jt * tk, tk)

    def cur_side(jt, first_chunk=0):
        vt = vt_ref[:, pl.ds(tile_start(jt), tk)]
        return [(slot, slot, vt) if qc >= first_chunk else None for slot, qc, _ in streams]

    def nxt_side(jt, first_chunk=0, mask_of=lambda qc: None):
        kt = k_ref[pl.ds(tile_start(jt), tk), :]
        return [(slot, kt, q, None, mask_of(qc)) if qc >= first_chunk else None for slot, qc, q in streams]

    idle = [None] * len(streams)

    def first_mask(qc):
        q_pos = (i * n_chunks + qc) * FLASH_TILE + qq
        return (kk >> CHUNK_SHIFT) <= (q_pos >> CHUNK_SHIFT)

    flash.step(idle, nxt_side(0, mask_of=first_mask))

    def body(jt, _):
        flash.step(cur_side(jt), nxt_side(jt + 1))
        return 0

    lax.fori_loop(0, jnp.maximum(n_full - 1, 0), body, 0)

    diag = lambda d: (lambda qc: ((kk >> CHUNK_SHIFT) <= (qq >> CHUNK_SHIFT)) if qc == d else None)

    @pl.when(i > 0)
    def _():
        flash.step(cur_side(n_full - 1), nxt_side(n_full, mask_of=diag(0)))

    for d in range(1, n_chunks):
        flash.step(cur_side(n_full + d - 1, d - 1), nxt_side(n_full + d, d, diag(d)))
    flash.step(cur_side(n_full + n_chunks - 1, n_chunks - 1), idle)

    lam = (jnp.exp(jnp.sum(lq1_ref[...] * lk1_ref[...], axis=1, keepdims=True))
           - jnp.exp(jnp.sum(lq2_ref[...] * lk2_ref[...], axis=1, keepdims=True)) + lam_init)
    for qc in range(n_chunks):
        s0, s1 = qc, n_chunks + qc
        o = (acc_ref[s0, :dv, :] / acc_ref[s0, dv:dv + 1, :]
             - lam * (acc_ref[s1, :dv, :] / acc_ref[s1, dv:dv + 1, :]))
        inv = lax.rsqrt(jnp.mean(o * o, axis=0, keepdims=True) + EPS)
        o_ref[qc * FLASH_TILE:(qc + 1) * FLASH_TILE, :] = (
            jnp.transpose(o * inv) * g_ref[...] * (1.0 - lam_init)).astype(o_ref.dtype)


def _diff_attn(q_t, k, v_t, lam_vecs, sub_gain, layer, lam_init, tq):
    L = k.shape[0]
    vec = lambda w: _resident((None, 1, w), lambda h, i: (layer, 0, 0))
    return pl.pallas_call(
        functools.partial(_diff_attn_body, tq=tq, lam_init=lam_init),
        grid=(C_HEADS, L // tq),
        in_specs=[
            vec(C_QK_DIM), vec(C_QK_DIM), vec(C_QK_DIM), vec(C_QK_DIM), vec(C_V_DIM),
            pl.BlockSpec((C_V_DIM, tq), lambda h, i: (h, i)),
            pl.BlockSpec((L, C_V_DIM), lambda h, i: (0, h)),
            pl.BlockSpec((None, C_V_DIM + ONES_ROWS, L), lambda h, i: (h, 0, 0)),
        ],
        out_specs=pl.BlockSpec((tq, C_V_DIM), lambda h, i: (i, h)),
        out_shape=jax.ShapeDtypeStruct((L, C_WIDTH), BF16),
        scratch_shapes=[
            pltpu.VMEM((2 * (tq // FLASH_TILE), FLASH_TILE, FLASH_TILE), F32),
            pltpu.VMEM((2 * (tq // FLASH_TILE), 1, FLASH_TILE), F32),
            pltpu.VMEM((2 * (tq // FLASH_TILE), C_V_DIM + ONES_ROWS, FLASH_TILE), F32),
        ],
        compiler_params=_cparams(("parallel", "parallel")),
        name="diff_attn",
    )(*lam_vecs, sub_gain, q_t, k, v_t)


def _s5_prep_body(are_ref, aim_ref, ldt_ref, bre_ref, bim_ref, bb_ref, lvl_ref, pw_ref):
    a_re, a_im = are_ref[...], aim_ref[...]
    dt = jnp.exp(ldt_ref[...])
    mag = jnp.exp(a_re * dt)
    lb_re = mag * jnp.cos(a_im * dt)
    lb_im = mag * jnp.sin(a_im * dt)
    den = a_re * a_re + a_im * a_im
    f_re = ((lb_re - 1.0) * a_re + lb_im * a_im) / den
    f_im = (lb_im * a_re - (lb_re - 1.0) * a_im) / den
    b_re, b_im = bre_ref[...], bim_ref[...]
    bb_ref[:, :S5_NSTATE] = (f_re * b_re - f_im * b_im).astype(BF16)
    bb_ref[:, S5_NSTATE:] = (f_re * b_im + f_im * b_re).astype(BF16)

    pows = [(lb_re, lb_im)]
    for _ in range(SUBLANES - 1):
        pr, pi = pows[-1]
        pows.append((pr * lb_re - pi * lb_im, pr * lb_im + pi * lb_re))
    row = lax.broadcasted_iota(jnp.int32, (SUBLANES, S5_NSTATE), 0)
    zero = jnp.zeros((SUBLANES, S5_NSTATE), F32)
    for lvl in range(3):
        sh = 1 << lvl
        pr, pi = pows[sh - 1]
        lvl_ref[lvl, 0] = jnp.where(row >= sh, pr, zero)
        lvl_ref[lvl, 1] = jnp.where(row >= sh, pi, zero)
    cr, ci = zero, zero
    for r in range(SUBLANES):
        cr = jnp.where(row == r, pows[r][0], cr)
        ci = jnp.where(row == r, pows[r][1], ci)
    pw_ref[0] = cr
    pw_ref[1] = ci


def _s5_prep(a_re, a_im, log_dt, b_re_bd, b_im_bd):
    depth = a_re.shape[0]
    vec = pl.BlockSpec((None, 1, S5_NSTATE), lambda l: (l, 0, 0))
    mat = pl.BlockSpec((None, S5_WIDTH, S5_NSTATE), lambda l: (l, 0, 0))
    return pl.pallas_call(
        _s5_prep_body,
        grid=(depth,),
        in_specs=[vec, vec, vec, mat, mat],
        out_specs=[
            pl.BlockSpec((None, S5_WIDTH, 2 * S5_NSTATE), lambda l: (l, 0, 0)),
            pl.BlockSpec((None, 3, 2, SUBLANES, S5_NSTATE), lambda l: (l, 0, 0, 0, 0)),
            pl.BlockSpec((None, 2, SUBLANES, S5_NSTATE), lambda l: (l, 0, 0, 0)),
        ],
        out_shape=[
            jax.ShapeDtypeStruct((depth, S5_WIDTH, 2 * S5_NSTATE), BF16),
            jax.ShapeDtypeStruct((depth, 3, 2, SUBLANES, S5_NSTATE), F32),
            jax.ShapeDtypeStruct((depth, 2, SUBLANES, S5_NSTATE), F32),
        ],
        compiler_params=_cparams(("parallel",)),
        name="s5_prep",
    )(a_re, a_im, log_dt, b_re_bd, b_im_bd)


S5_LANE_CHUNK = 512


def _s5_body(u_ref, bb_ref, lvl_ref, pw_ref, cre_ref, cim_ref, d_ref, wg_ref, bg_ref, o_ref,
             x_ref, carry_ref, *, tl):
    @pl.when(pl.program_id(0) == 0)
    def _():
        carry_ref[...] = jnp.zeros(carry_ref.shape, F32)

    u = u_ref[...]
    x_ref[...] = _dot(u.astype(BF16), bb_ref[...])

    chunks = [(slice(c0, c0 + S5_LANE_CHUNK), slice(S5_NSTATE + c0, S5_NSTATE + c0 + S5_LANE_CHUNK))
              for c0 in range(0, S5_NSTATE, S5_LANE_CHUNK)]

    def block(t, carries):
        rows = pl.ds(pl.multiple_of(t * SUBLANES, SUBLANES), SUBLANES)
        out = []
        for (re_sl, im_sl), (c_re, c_im) in zip(chunks, carries):
            re = x_ref[rows, re_sl]
            im = x_ref[rows, im_sl]
            for lvl in range(3):
                sh = 1 << lvl
                s_re = pltpu.roll(re, sh, 0)
                s_im = pltpu.roll(im, sh, 0)
                a_re = lvl_ref[lvl, 0, :, re_sl]
                a_im = lvl_ref[lvl, 1, :, re_sl]
                re, im = re + (a_re * s_re - a_im * s_im), im + (a_re * s_im + a_im * s_re)
            p_re = pw_ref[0, :, re_sl]
            p_im = pw_ref[1, :, re_sl]
            re, im = re + (p_re * c_re - p_im * c_im), im + (p_re * c_im + p_im * c_re)
            x_ref[rows, re_sl] = re
            x_ref[rows, im_sl] = im
            last = SUBLANES - 1
            out.append((jnp.broadcast_to(re[last:, :], re.shape), jnp.broadcast_to(im[last:, :], im.shape)))
        return tuple(out)

    carries = lax.fori_loop(0, tl // SUBLANES, block,
                            tuple((carry_ref[0, :, re_sl], carry_ref[1, :, re_sl]) for re_sl, _ in chunks))
    for (re_sl, _), (c_re, c_im) in zip(chunks, carries):
        carry_ref[0, :, re_sl] = c_re
        carry_ref[1, :, re_sl] = c_im

    y = (_dot(x_ref[:, :S5_NSTATE].astype(BF16), cre_ref[...])
         - _dot(x_ref[:, S5_NSTATE:].astype(BF16), cim_ref[...])
         + d_ref[...] * u)
    y = 0.5 * y * (1.0 + jnp.tanh(math.sqrt(2.0 / math.pi) * (y + 0.044715 * (y * y * y))))
    z = _dot(y.astype(BF16), wg_ref[...]) + bg_ref[...]
    o_ref[...] = (y * (1.0 / (1.0 + jnp.exp(-z)))).astype(o_ref.dtype)


def _s5_mix(u, bb, lvl, pw, c_re_bd, c_im_bd, d_row, w_glu, b_glu, layer, tl):
    L = u.shape[0]
    lay = lambda *rest: (lambda t: (layer,) + rest)
    return pl.pallas_call(
        functools.partial(_s5_body, tl=tl),
        grid=(L // tl,),
        in_specs=[
            pl.BlockSpec((tl, S5_WIDTH), lambda t: (t, 0)),
            _resident((None, S5_WIDTH, 2 * S5_NSTATE), lay(0, 0)),
            _resident((None, 3, 2, SUBLANES, S5_NSTATE), lay(0, 0, 0, 0)),
            _resident((None, 2, SUBLANES, S5_NSTATE), lay(0, 0, 0)),
            _resident((None, S5_NSTATE, S5_WIDTH), lay(0, 0)),
            _resident((None, S5_NSTATE, S5_WIDTH), lay(0, 0)),
            _resident((None, 1, S5_WIDTH), lay(0, 0)),
            _resident((None, S5_WIDTH, S5_WIDTH), lay(0, 0)),
            _resident((None, 1, S5_WIDTH), lay(0, 0)),
        ],
        out_specs=pl.BlockSpec((tl, S5_WIDTH), lambda t: (t, 0)),
        out_shape=jax.ShapeDtypeStruct((L, S5_WIDTH), BF16),
        scratch_shapes=[
            pltpu.VMEM((tl, 2 * S5_NSTATE), F32),
            pltpu.VMEM((2, SUBLANES, S5_NSTATE), F32),
        ],
        compiler_params=_cparams(("arbitrary",)),
        name="s5_mix",
    )(u, bb, lvl, pw, c_re_bd, c_im_bd, d_row, w_glu, b_glu)


def _outproj_body(x_ref, oa_ref, os_ref, oc_ref, w_ref, o_ref):
    a0, a1, a2 = A_WIDTH, A_WIDTH + S5_WIDTH, A_WIDTH + S5_WIDTH + C_WIDTH
    o_ref[...] = (x_ref[...] + _dot(oa_ref[...], w_ref[:a0, :]) + _dot(os_ref[...], w_ref[a0:a1, :])
                  + _dot(oc_ref[...], w_ref[a1:a2, :]))


def _out_proj(x, o_a, o_s, o_c, w_out, layer, tm):
    L = x.shape[0]
    row = lambda i: (i, 0)
    return pl.pallas_call(
        _outproj_body,
        grid=(L // tm,),
        in_specs=[
            pl.BlockSpec((tm, D_MODEL), row),
            pl.BlockSpec((tm, A_WIDTH), row),
            pl.BlockSpec((tm, S5_WIDTH), row),
            pl.BlockSpec((tm, C_WIDTH), row),
            _resident((None, D_MODEL, D_MODEL), lambda i: (layer, 0, 0)),
        ],
        out_specs=pl.BlockSpec((tm, D_MODEL), row),
        out_shape=jax.ShapeDtypeStruct((L, D_MODEL), F32),
        compiler_params=_cparams(("parallel",)),
        name="out_proj",
    )(x, o_a, o_s, o_c, w_out)


def _norm_matmul_body(x_ref, g_ref, w_ref, o_ref):
    o_ref[...] = _dot(_rms(x_ref[...], g_ref[...]).astype(BF16), w_ref[...]).astype(o_ref.dtype)


def _norm_matmul(x, g, w, layer, tn):
    M, K = x.shape
    N = w.shape[-1]
    return pl.pallas_call(
        _norm_matmul_body,
        grid=(N // tn,),
        in_specs=[
            _resident((M, K), lambda j: (0, 0)),
            _resident((None, 1, K), lambda j: (layer, 0, 0)),
            pl.BlockSpec((None, K, tn), lambda j: (layer, 0, j)),
        ],
        out_specs=pl.BlockSpec((M, tn), lambda j: (0, j)),
        out_shape=jax.ShapeDtypeStruct((M, N), BF16),
        compiler_params=_cparams(("parallel",)),
        name="mem_proj",
    )(x, g, w)


def _xattn_body(x_ref, g_ref, wq_ref, k_ref, v_ref, wo_ref, o_ref):
    x = x_ref[...]
    q = _dot(_rms(x, g_ref[...]).astype(BF16), wq_ref[...]).astype(BF16)
    scale = X_HEAD_DIM ** -0.5
    out = x
    for h in range(X_HEADS):
        hs = slice(h * X_HEAD_DIM, (h + 1) * X_HEAD_DIM)
        s = _dot_nt(q[:, hs], k_ref[:, hs]) * scale
        p = jnp.exp(s - jnp.max(s, axis=1, keepdims=True))
        p = p / jnp.sum(p, axis=1, keepdims=True)
        o_h = _dot(p.astype(BF16), v_ref[:, hs]).astype(BF16)
        out = out + _dot(o_h, wo_ref[hs, :])
    o_ref[...] = out


def _cross_attn(x, g, wq, k_mem, v_mem, wo, layer, tm):
    L = x.shape[0]
    M = k_mem.shape[0]
    row = lambda i: (i, 0)
    return pl.pallas_call(
        _xattn_body,
        grid=(L // tm,),
        in_specs=[
            pl.BlockSpec((tm, D_MODEL), row),
            _resident((None, 1, D_MODEL), lambda i: (layer, 0, 0)),
            _resident((None, D_MODEL, D_MODEL), lambda i: (layer, 0, 0)),
            _resident((M, D_MODEL), lambda i: (0, 0)),
            _resident((M, D_MODEL), lambda i: (0, 0)),
            _resident((None, D_MODEL, D_MODEL), lambda i: (layer, 0, 0)),
        ],
        out_specs=pl.BlockSpec((tm, D_MODEL), row),
        out_shape=jax.ShapeDtypeStruct((L, D_MODEL), F32),
        compiler_params=_cparams(("parallel",)),
        name="cross_attn",
    )(x, g, wq, k_mem, v_mem, wo)


def _mlp_body(x_ref, g_ref, w1_ref, w2_ref, gout_ref, o_ref, hn_ref, *, norm_output):
    f = pl.program_id(1)

    @pl.when(f == 0)
    def _():
        x = x_ref[...]
        hn_ref[...] = _rms(x, g_ref[...]).astype(BF16)
        o_ref[...] = x

    a = jnp.maximum(_dot(hn_ref[...], w1_ref[...]), 0.0)
    o_ref[...] += _dot((a * a).astype(BF16), w2_ref[...])

    if norm_output:
        @pl.when(f == pl.num_programs(1) - 1)
        def _():
            o_ref[...] = _rms(o_ref[...], gout_ref[...])


def _mlp(x, g, w1, w2, g_out, layer, tm, tf, norm_output):
    L = x.shape[0]
    return pl.pallas_call(
        functools.partial(_mlp_body, norm_output=norm_output),
        grid=(L // tm, D_FF // tf),
        in_specs=[
            pl.BlockSpec((tm, D_MODEL), lambda i, f: (i, 0)),
            _resident((None, 1, D_MODEL), lambda i, f: (layer, 0, 0)),
            pl.BlockSpec((None, D_MODEL, tf), lambda i, f: (layer, 0, f)),
            pl.BlockSpec((None, tf, D_MODEL), lambda i, f: (layer, f, 0)),
            _resident((1, D_MODEL), lambda i, f: (0, 0)),
        ],
        out_specs=pl.BlockSpec((tm, D_MODEL), lambda i, f: (i, 0)),
        out_shape=jax.ShapeDtypeStruct((L, D_MODEL), F32),
        scratch_shapes=[pltpu.VMEM((tm, D_MODEL), BF16)],
        compiler_params=_cparams(("parallel", "arbitrary")),
        name="mlp",
    )(x, g, w1, w2, g_out)


def _block_diag_in(b):
    eye = jnp.eye(S5_GROUPS, dtype=b.dtype)
    t = jnp.transpose(b, (0, 1, 3, 2))
    bd = t[:, :, :, None, :] * eye[None, :, None, :, None]
    return bd.reshape(b.shape[0], S5_WIDTH, S5_NSTATE)


def _block_diag_out(c):
    eye = jnp.eye(S5_GROUPS, dtype=c.dtype)
    t = jnp.transpose(c, (0, 1, 3, 2))
    bd = t[:, :, :, None, :] * eye[None, :, None, :, None]
    return bd.reshape(c.shape[0], S5_NSTATE, S5_WIDTH)


def _pick_tile(n, want):
    t = min(n, want)
    assert n % t == 0, (n, t)
    return t


def kernel(x, mem, norm_mix, w_in, s5_a_re, s5_a_im, s5_log_dt, s5_b_re, s5_b_im, s5_c_re, s5_c_im, s5_d, s5_w_glu, s5_b_glu, diff_lam_q1, diff_lam_k1, diff_lam_q2, diff_lam_k2, diff_subln, w_out, norm_xattn, norm_mem, xattn_q, xattn_k, xattn_v, xattn_o, norm_mlp, w_ff1, w_ff2, norm_final):
    B, L, _ = x.shape
    assert B == 1
    depth = w_in.shape[0]
    tq = _pick_tile(L, 256)

    pad = jnp.zeros((depth, D_MODEL, KIW_W - IDX_DIM - IDX_HEADS), BF16)
    w_in_p = jnp.concatenate([w_in[..., :IN_SRC_SPLIT].astype(BF16), pad, w_in[..., IN_SRC_SPLIT:].astype(BF16)],
                             axis=-1)
    w_out_b = w_out.astype(BF16)
    wq_b, wk_b, wv_b, wo_b = (w.astype(BF16) for w in (xattn_q, xattn_k, xattn_v, xattn_o))
    w1_b, w2_b = w_ff1.astype(BF16), w_ff2.astype(BF16)
    wg_b = s5_w_glu.astype(BF16)
    row3 = lambda a: a.reshape(depth, 1, -1)

    tabs, tabs_k = _rope_tables(L)

    rep = lambda a: row3(a.astype(F32))
    a_re, a_im = rep(s5_a_re), rep(s5_a_im)
    log_dt = row3(jnp.broadcast_to(s5_log_dt.astype(F32)[:, :, None], (depth, S5_GROUPS, S5_STATE)))
    bb, lvl, pw = _s5_prep(a_re, a_im, log_dt, _block_diag_in(s5_b_re.astype(F32)),
                           _block_diag_in(s5_b_im.astype(F32)))
    c_re_bd = _block_diag_out(s5_c_re).astype(BF16)
    c_im_bd = _block_diag_out(s5_c_im).astype(BF16)
    d_row = row3(s5_d.astype(F32))
    lam_vecs = tuple(row3(v.astype(F32)) for v in (diff_lam_q1, diff_lam_k1, diff_lam_q2, diff_lam_k2))

    xs = x[0]
    mem2 = mem[0]
    for l in range(depth):
        qa, ka, va, qi, ke, ko, wi, us, qc, kc, vc = _in_proj(xs, row3(norm_mix), w_in_p, l, tabs + tabs_k,
                                                               _pick_tile(L, 256))

        w_t = jnp.transpose(wi[:, IDX_DIM:IDX_DIM + IDX_HEADS])
        bias = _dsa_index(jnp.transpose(qi), w_t, ke, ko, tq).reshape(L // tq, L, tq)
        o_a = _dsa_attn(jnp.transpose(qa), ka, _values_t(va, A_HEADS), bias, tq)

        o_s = _s5_mix(us, bb, lvl, pw, c_re_bd, c_im_bd, d_row, wg_b, row3(s5_b_glu.astype(F32)), l,
                      _pick_tile(L, 512))

        lam_init = 0.8 - 0.6 * math.exp(-0.3 * l)
        o_c = _diff_attn(jnp.transpose(qc), kc, _values_t(vc, C_HEADS), lam_vecs, row3(diff_subln.astype(F32)), l,
                         lam_init, _pick_tile(L, 2048))

        xs = _out_proj(xs, o_a, o_s, o_c, w_out_b, l, _pick_tile(L, 512))

        k_mem = _norm_matmul(mem2, row3(norm_mem), wk_b, l, 512)
        v_mem = _norm_matmul(mem2, row3(norm_mem), wv_b, l, 512)
        xs = _cross_attn(xs, row3(norm_xattn), wq_b, k_mem, v_mem, wo_b, l, _pick_tile(L, 512))

        xs = _mlp(xs, row3(norm_mlp), w1_b, w2_b, norm_final.reshape(1, -1), l, _pick_tile(L, 1024), 512,
                  norm_output=(l == depth - 1))

    return xs[None]
```
